```python
import jax
import jax.numpy as jnp
from jax import lax
import numpy as np


D_MODEL = 2048
BATCH = 1
SEQ = 8192
DEPTH = 2

N_BRANCHES = 4
BRANCH_WIDTH = D_MODEL // 4

SB_HEADS = 4
SB_HEAD_DIM = BRANCH_WIDTH // SB_HEADS
SB_BLOCK = 128

GM_GROUPS = 4
GM_GROUP_DIM = BRANCH_WIDTH // GM_GROUPS
GM_CHUNK = 128

GDN_HEADS = 4
GDN_HEAD_DIM = BRANCH_WIDTH // GDN_HEADS
GDN_CONV = 4
GDN_CHUNK = 64

ML_HEADS = 4
ML_QK_DIM = BRANCH_WIDTH // ML_HEADS // 2
ML_V_DIM = BRANCH_WIDTH // ML_HEADS
ML_CHUNK = 64

N_EXPERTS = 64
N_EXPERT_GROUPS = 8
TOPK_GROUPS = 4
TOP_K = 6
EXPERT_DIM = 512
SHARED_DIM = 512
ROUTED_SCALE = 2.5

DEEPNORM_ALPHA = (2 * DEPTH) ** 0.25
DEEPNORM_BETA = (8 * DEPTH) ** -0.25
NORM_EPS = 1e-5

IN_SPLITS = (
    3 * BRANCH_WIDTH,
    2 * BRANCH_WIDTH,
    3 * BRANCH_WIDTH,
    BRANCH_WIDTH,
    GDN_HEADS,
    GDN_HEADS,
    ML_HEADS * ML_QK_DIM,
    ML_HEADS * ML_QK_DIM,
    BRANCH_WIDTH,
    BRANCH_WIDTH,
    ML_HEADS,
    ML_HEADS,
    N_BRANCHES * D_MODEL,
)
IN_COLS = sum(IN_SPLITS)

kernel_name = 'hybrid_sb_gmlp_gdn_mlstm_moe_deepnorm'


def layer_norm(x, g, b):
    x32 = x.astype(jnp.float32)
    mu = jnp.mean(x32, axis=-1, keepdims=True)
    var = jnp.mean(jnp.square(x32 - mu), axis=-1, keepdims=True)
    return ((x32 - mu) * lax.rsqrt(var + NORM_EPS) * g + b).astype(x.dtype)


def rms_norm(x, g):
    x32 = x.astype(jnp.float32)
    return x32 * lax.rsqrt(jnp.mean(jnp.square(x32), axis=-1, keepdims=True) + NORM_EPS) * g


def l2_normalize(x):
    return x * lax.rsqrt(jnp.sum(jnp.square(x), axis=-1, keepdims=True) + 1e-6)


def split_heads(x, h):
    b, t, f = x.shape
    return x.reshape(b, t, h, f // h).transpose(0, 2, 1, 3)


def merge_heads(x):
    b, h, t, d = x.shape
    return x.transpose(0, 2, 1, 3).reshape(b, t, h * d)


def causal_depthwise_conv(x, w):
    k, c = w.shape
    return lax.conv_general_dilated(
        x, w.astype(x.dtype)[:, None, :], window_strides=(1,), padding=[(k - 1, 0)],
        dimension_numbers=('NWC', 'WIO', 'NWC'), feature_group_count=c)


def stick_breaking_attention(q, k, v):
    B_, H, T, d = q.shape
    nb = T // SB_BLOCK
    k32 = k.astype(jnp.float32)
    q_blocks = jnp.moveaxis(q.astype(jnp.float32).reshape(B_, H, nb, SB_BLOCK, d), 2, 0)
    key_pos = jnp.arange(T)

    def block(args):
        q_blk, blk = args
        query_pos = blk * SB_BLOCK + jnp.arange(SB_BLOCK)
        z = jnp.einsum('bhqd,bhkd->bhqk', q_blk, k32) * d ** -0.5
        causal = key_pos[None, :] < query_pos[:, None]
        log_keep = jnp.where(causal, -jax.nn.softplus(z), 0.0)
        log_between = lax.cumsum(log_keep, axis=3, reverse=True) - log_keep
        a = jnp.where(causal, jnp.exp(jax.nn.log_sigmoid(z) + log_between), 0.0)
        return jnp.einsum('bhqk,bhkd->bhqd', a.astype(v.dtype), v)

    out = lax.map(block, (q_blocks, jnp.arange(nb)))
    return jnp.moveaxis(out, 0, 2).reshape(B_, H, T, d)


def chunked_spatial_gating(uv, norm_g, norm_b, w_s, b_s):
    u, v = jnp.split(jax.nn.gelu(uv), 2, axis=-1)
    v = layer_norm(v, norm_g, norm_b)
    B_, T, W = v.shape
    nc = T // GM_CHUNK
    v = v.reshape(B_, nc, GM_CHUNK, GM_GROUPS, GM_GROUP_DIM)
    tri = jnp.tril(jnp.ones((GM_CHUNK, GM_CHUNK), dtype=bool))
    w_causal = jnp.where(tri, w_s, 0.0).astype(v.dtype)
    mixed = jnp.einsum('gts,bnsgc->bntgc', w_causal, v) + b_s.T.astype(v.dtype)[None, None, :, :, None]
    return u * mixed.reshape(B_, T, W)


def chunk_gated_delta_rule(q, k, v, g, beta):
    B_, H, T, dk = q.shape
    dv = v.shape[-1]
    C = GDN_CHUNK
    nc = T // C
    q = q.reshape(B_, H, nc, C, dk)
    k = k.reshape(B_, H, nc, C, dk)
    v = v.reshape(B_, H, nc, C, dv)
    beta = beta.reshape(B_, H, nc, C)
    gc = jnp.cumsum(g.reshape(B_, H, nc, C), axis=-1)
    idx = jnp.arange(C)
    incl = idx[:, None] >= idx[None, :]
    strict = idx[:, None] > idx[None, :]
    decay = jnp.exp(jnp.where(incl, gc[..., :, None] - gc[..., None, :], -jnp.inf))
    kb = k * beta[..., None]
    lmat = jnp.where(strict, jnp.einsum('bhnid,bhnjd->bhnij', kb, k) * decay, 0.0)
    unit_lower = lmat + jnp.eye(C, dtype=lmat.dtype)
    u = lax.linalg.triangular_solve(unit_lower, v * beta[..., None], left_side=True,
                                    lower=True, unit_diagonal=True)
    w = lax.linalg.triangular_solve(unit_lower, kb * jnp.exp(gc)[..., None], left_side=True,
                                    lower=True, unit_diagonal=True)
    attn = jnp.einsum('bhnid,bhnjd->bhnij', q, k) * decay

    def step(s, xs):
        q_c, k_c, u_c, w_c, gc_c, attn_c = xs
        v_new = u_c - jnp.einsum('bhcd,bhde->bhce', w_c, s)
        o = (jnp.einsum('bhcd,bhde->bhce', q_c * jnp.exp(gc_c)[..., None], s)
             + jnp.einsum('bhij,bhje->bhie', attn_c, v_new))
        g_last = gc_c[..., -1:]
        s = (s * jnp.exp(g_last)[..., None]
             + jnp.einsum('bhcd,bhce->bhde', k_c * jnp.exp(g_last - gc_c)[..., None], v_new))
        return s, o

    xs = tuple(jnp.moveaxis(t, 2, 0) for t in (q, k, u, w, gc, attn))
    s0 = jnp.zeros((B_, H, dk, dv), jnp.float32)
    _, o = lax.scan(step, s0, xs)
    return jnp.moveaxis(o, 0, 2).reshape(B_, H, T, dv)


def mlstm_chunkwise(q, k, v, i_pre, f_pre):
    B_, H, T, dqk = q.shape
    dv = v.shape[-1]
    C = ML_CHUNK
    nc = T // C
    q = (q * dqk ** -0.5).reshape(B_, H, nc, C, dqk)
    k = k.reshape(B_, H, nc, C, dqk)
    v = v.reshape(B_, H, nc, C, dv)
    i_pre = i_pre.reshape(B_, H, nc, C)
    b = jnp.cumsum(jax.nn.log_sigmoid(f_pre).reshape(B_, H, nc, C), axis=-1)
    idx = jnp.arange(C)
    incl = idx[:, None] >= idx[None, :]

    def step(carry, xs):
        c_mat, n_vec, m = carry
        q_c, k_c, v_c, i_c, b_c = xs
        log_intra = jnp.where(incl, b_c[..., :, None] - b_c[..., None, :] + i_c[..., None, :], -jnp.inf)
        log_inter = b_c + m[..., None]
        m_out = jnp.maximum(log_inter, jnp.max(log_intra, axis=-1))
        w_inter = jnp.exp(log_inter - m_out)
        s = jnp.exp(log_intra - m_out[..., None]) * jnp.einsum('bhid,bhjd->bhij', q_c, k_c)
        num = (w_inter[..., None] * jnp.einsum('bhcd,bhde->bhce', q_c, c_mat)
               + jnp.einsum('bhij,bhje->bhie', s, v_c))
        den = w_inter * jnp.einsum('bhcd,bhd->bhc', q_c, n_vec) + jnp.sum(s, axis=-1)
        h = num / jnp.maximum(jnp.abs(den), jnp.exp(-m_out))[..., None]
        b_last = b_c[..., -1]
        log_kv = b_last[..., None] - b_c + i_c
        m_new = jnp.maximum(b_last + m, jnp.max(log_kv, axis=-1))
        w_kv = jnp.exp(log_kv - m_new[..., None])
        carry_decay = jnp.exp(b_last + m - m_new)
        c_mat = carry_decay[..., None, None] * c_mat + jnp.einsum('bhc,bhcd,bhce->bhde', w_kv, k_c, v_c)
        n_vec = carry_decay[..., None] * n_vec + jnp.einsum('bhc,bhcd->bhd', w_kv, k_c)
        return (c_mat, n_vec, m_new), h

    init = (jnp.zeros((B_, H, dqk, dv), jnp.float32),
            jnp.zeros((B_, H, dqk), jnp.float32),
            jnp.zeros((B_, H), jnp.float32))
    xs = tuple(jnp.moveaxis(t, 2, 0) for t in (q, k, v, i_pre, b))
    _, h = lax.scan(step, init, xs)
    return jnp.moveaxis(h, 0, 2).reshape(B_, H, T, dv)


def hybrid_mixer(x, w_in, gm_norm_g, gm_norm_b, gm_w_s, gm_b_s, gdn_conv_w, gdn_a_log,
                 gdn_dt_bias, gdn_norm_g, ml_i_bias, ml_f_bias, ml_norm_g, w_branch, w_out):
    B_, T, D = x.shape
    f32 = jnp.float32
    z = jnp.einsum('btd,dc->btc', x, w_in)
    split_points = [int(c) for c in np.cumsum(IN_SPLITS)[:-1]]
    (sb_qkv, gm_uv, gdn_qkv, gdn_gate, gdn_b, gdn_a,
     ml_q, ml_k, ml_v, ml_o, ml_i, ml_f, gate_logits) = jnp.split(z, split_points, axis=-1)

    sq, sk, sv = (split_heads(t, SB_HEADS) for t in jnp.split(sb_qkv, 3, axis=-1))
    out_a = merge_heads(stick_breaking_attention(sq, sk, sv))

    out_b = chunked_spatial_gating(gm_uv, gm_norm_g, gm_norm_b, gm_w_s, gm_b_s)

    qkv = jax.nn.silu(causal_depthwise_conv(gdn_qkv, gdn_conv_w))
    gq, gk, gv = jnp.split(qkv, 3, axis=-1)
    gq = l2_normalize(split_heads(gq, GDN_HEADS).astype(f32)) * GDN_HEAD_DIM ** -0.5
    gk = l2_normalize(split_heads(gk, GDN_HEADS).astype(f32))
    gv = split_heads(gv, GDN_HEADS).astype(f32)
    beta = jax.nn.sigmoid(gdn_b.astype(f32)).transpose(0, 2, 1)
    g = (-jnp.exp(gdn_a_log.astype(f32))
         * jax.nn.softplus(gdn_a.astype(f32) + gdn_dt_bias.astype(f32))).transpose(0, 2, 1)
    o = rms_norm(chunk_gated_delta_rule(gq, gk, gv, g, beta), gdn_norm_g)
    out_c = merge_heads(o).astype(x.dtype) * jax.nn.silu(gdn_gate)

    h = mlstm_chunkwise(split_heads(ml_q, ML_HEADS).astype(f32),
                        split_heads(ml_k, ML_HEADS).astype(f32),
                        split_heads(ml_v, ML_HEADS).astype(f32),
                        (ml_i.astype(f32) + ml_i_bias.astype(f32)).transpose(0, 2, 1),
                        (ml_f.astype(f32) + ml_f_bias.astype(f32)).transpose(0, 2, 1))
    h = rms_norm(h, ml_norm_g.reshape(ML_HEADS, 1, ML_V_DIM))
    out_d = merge_heads(h).astype(x.dtype) * jax.nn.sigmoid(ml_o)

    branches = jnp.stack([out_a, out_b, out_c, out_d], axis=2)
    proj = jnp.einsum('btgc,gcd->btgd', branches, w_branch)
    gates = jax.nn.sigmoid(gate_logits.reshape(B_, T, N_BRANCHES, D))
    return jnp.einsum('btd,de->bte', jnp.sum(gates * proj, axis=2), w_out)


def moe_ffn(x, w_router, router_bias, w_gate, w_up, w_down, ws_gate, ws_up, ws_down):
    B_, T, D = x.shape
    xt = x.reshape(B_ * T, D)
    n = xt.shape[0]
    scores = jax.nn.sigmoid(jnp.einsum('nd,de->ne', xt.astype(jnp.float32), w_router.astype(jnp.float32)))
    biased = scores + router_bias.astype(jnp.float32)
    per_group = N_EXPERTS // N_EXPERT_GROUPS
    group_score = jnp.sum(lax.top_k(biased.reshape(n, N_EXPERT_GROUPS, per_group), 2)[0], axis=-1)
    _, top_groups = lax.top_k(group_score, TOPK_GROUPS)
    group_mask = jnp.sum(jax.nn.one_hot(top_groups, N_EXPERT_GROUPS, dtype=jnp.float32), axis=1)
    expert_mask = jnp.repeat(group_mask, per_group, axis=-1) > 0
    _, top_idx = lax.top_k(jnp.where(expert_mask, biased, -jnp.inf), TOP_K)
    sel = jnp.take_along_axis(scores, top_idx, axis=-1)
    weights = sel / jnp.sum(sel, axis=-1, keepdims=True) * ROUTED_SCALE
    combine = jnp.einsum('nk,nke->ne', weights, jax.nn.one_hot(top_idx, N_EXPERTS, dtype=jnp.float32))

    shared = (jax.nn.silu(xt @ ws_gate) * (xt @ ws_up)) @ ws_down

    def expert(acc, p):
        wg, wu, wd, c = p
        hidden = jax.nn.silu(xt @ wg) * (xt @ wu)
        return acc + (hidden @ wd) * c.astype(acc.dtype)[:, None], None

    out, _ = lax.scan(expert, shared, (w_gate, w_up, w_down, combine.T))
    return out.reshape(B_, T, D)


def setup_inputs(seed: int = 0) -> dict:
    key = jax.random.key(seed)
    ks = jax.random.split(key, 32)
    f32 = jnp.float32
    L, D, W = DEPTH, D_MODEL, BRANCH_WIDTH

    def nrm(k, shape, scale):
        return jax.random.normal(k, shape, f32) * scale

    dt = jnp.exp(jax.random.uniform(ks[8], (L, GDN_HEADS), f32, float(np.log(1e-3)), float(np.log(1e-1))))
    return {
        'x': nrm(ks[0], (BATCH, SEQ, D), 1.0),
        'w_in': nrm(ks[1], (L, D, IN_COLS), D ** -0.5),
        'gm_norm_g': 1.0 + nrm(ks[2], (L, W), 0.02),
        'gm_norm_b': nrm(ks[3], (L, W), 0.02),
        'gm_w_s': nrm(ks[4], (L, GM_GROUPS, GM_CHUNK, GM_CHUNK), GM_CHUNK ** -0.5),
        'gm_b_s': 1.0 + nrm(ks[5], (L, GM_GROUPS, GM_CHUNK), 0.02),
        'gdn_conv_w': nrm(ks[6], (L, GDN_CONV, 3 * W), GDN_CONV ** -0.5),
        'gdn_a_log': jnp.log(jax.random.uniform(ks[7], (L, GDN_HEADS), f32, 1.0, 16.0)),
        'gdn_dt_bias': dt + jnp.log(-jnp.expm1(-dt)),
        'gdn_norm_g': 1.0 + nrm(ks[9], (L, GDN_HEAD_DIM), 0.02),
        'ml_i_bias': nrm(ks[10], (L, ML_HEADS), 0.1),
        'ml_f_bias': 3.0 + nrm(ks[11], (L, ML_HEADS), 0.5),
        'ml_norm_g': 1.0 + nrm(ks[12], (L, W), 0.02),
        'w_branch': nrm(ks[13], (L, N_BRANCHES, W, D), W ** -0.5 * DEEPNORM_BETA),
        'w_out': nrm(ks[14], (L, D, D), D ** -0.5 * DEEPNORM_BETA),
        'ln1_g': 1.0 + nrm(ks[15], (L, D), 0.02),
        'ln1_b': nrm(ks[16], (L, D), 0.02),
        'w_router': nrm(ks[17], (L, D, N_EXPERTS), D ** -0.5),
        'router_bias': nrm(ks[18], (L, N_EXPERTS), 0.01),
        'w_gate': nrm(ks[19], (L, N_EXPERTS, D, EXPERT_DIM), D ** -0.5),
        'w_up': nrm(ks[20], (L, N_EXPERTS, D, EXPERT_DIM), D ** -0.5),
        'w_down': nrm(ks[21], (L, N_EXPERTS, EXPERT_DIM, D), EXPERT_DIM ** -0.5 * DEEPNORM_BETA),
        'ws_gate': nrm(ks[22], (L, D, SHARED_DIM), D ** -0.5),
        'ws_up': nrm(ks[23], (L, D, SHARED_DIM), D ** -0.5),
        'ws_down': nrm(ks[24], (L, SHARED_DIM, D), SHARED_DIM ** -0.5 * DEEPNORM_BETA),
        'ln2_g': 1.0 + nrm(ks[25], (L, D), 0.02),
        'ln2_b': nrm(ks[26], (L, D), 0.02),
    }


def reference(x, w_in, gm_norm_g, gm_norm_b, gm_w_s, gm_b_s, gdn_conv_w, gdn_a_log,
              gdn_dt_bias, gdn_norm_g, ml_i_bias, ml_f_bias, ml_norm_g, w_branch, w_out,
              ln1_g, ln1_b, w_router, router_bias, w_gate, w_up, w_down,
              ws_gate, ws_up, ws_down, ln2_g, ln2_b):
    for l in range(DEPTH):
        mix = hybrid_mixer(x, w_in[l], gm_norm_g[l], gm_norm_b[l], gm_w_s[l], gm_b_s[l],
                           gdn_conv_w[l], gdn_a_log[l], gdn_dt_bias[l], gdn_norm_g[l],
                           ml_i_bias[l], ml_f_bias[l], ml_norm_g[l], w_branch[l], w_out[l])
        x = layer_norm(DEEPNORM_ALPHA * x + mix, ln1_g[l], ln1_b[l])
        ffn = moe_ffn(x, w_router[l], router_bias[l], w_gate[l], w_up[l], w_down[l],
                      ws_gate[l], ws_up[l], ws_down[l])
        x = layer_norm(DEEPNORM_ALPHA * x + ffn, ln2_g[l], ln2_b[l])
    return x
```

```python
import functools

import jax
import jax.numpy as jnp
from jax import lax
from jax.experimental import pallas as pl
from jax.experimental.pallas import tpu as pltpu

F32 = jnp.float32
BF16 = jnp.bfloat16

D_MODEL = 2048
N_BRANCHES = 4
BRANCH_WIDTH = 512
HEADS = 4
HEAD_DIM = 128
GM_CHUNK = 128
GDN_CHUNK = 64
GDN_CONV = 4
ML_CHUNK = 64
ML_QK = 64
N_EXPERTS = 64
N_GROUPS = 8
GROUP_SIZE = 8
TOPK_GROUPS = 4
TOP_K = 6
EXPERT_DIM = 512
SHARED_DIM = 512
ROUTED_SCALE = 2.5
DEPTH = 2
ALPHA = (2 * DEPTH) ** 0.25
NORM_EPS = 1e-5

A_Q, A_K, A_V = 0, 512, 1024
B_U, B_V = 1536, 2048
C_Q, C_K, C_V, C_G = 2560, 3072, 3584, 4096
D_Q, D_K, D_V, D_O = 4608, 4864, 5120, 5632
GATES = 6144
MAIN_COLS = GATES + N_BRANCHES * D_MODEL
S_BETA, S_A, S_I, S_F = 0, 4, 8, 12
SMALL_COLS = 128

VMEM_LIMIT = 56 * 1024 * 1024


def _cparams(n_axes):
    return pltpu.CompilerParams(
        dimension_semantics=("arbitrary",) * n_axes, vmem_limit_bytes=VMEM_LIMIT)


def _dot(a, b):
    return jnp.dot(a, b, preferred_element_type=F32)


def _dot_nt(a, b):
    return lax.dot_general(a, b, (((1,), (1,)), ((), ())), preferred_element_type=F32)


def _dot_tn(a, b):
    return lax.dot_general(a, b, (((0,), (0,)), ((), ())), preferred_element_type=F32)


def _split2(a):
    hi = a.astype(BF16)
    lo = (a - hi.astype(F32)).astype(BF16)
    return hi, lo


def _split3(a):
    hi = a.astype(BF16)
    r = a - hi.astype(F32)
    mid = r.astype(BF16)
    lo = (r - mid.astype(F32)).astype(BF16)
    return hi, mid, lo


def _dot_sel_lhs(sel, a):
    hi, mid, lo = _split3(a)
    return _dot(sel, hi) + _dot(sel, mid) + _dot(sel, lo)


def _dot_sel_rhs(a, sel):
    hi, lo = _split2(a)
    return _dot(hi, sel) + _dot(lo, sel)


def _dot_f32(a, b):
    ah, al = _split2(a)
    bh, bl = _split2(b)
    return _dot(ah, bh) + _dot(ah, bl) + _dot(al, bh)


def _softplus(x):
    return jnp.maximum(x, 0.0) + jnp.log1p(jnp.exp(-jnp.abs(x)))


def _sigmoid(x):
    return 1.0 / (1.0 + jnp.exp(-x))


def _silu(x):
    return x * _sigmoid(x)


def _layer_norm(x, g, b):
    mu = jnp.mean(x, axis=-1, keepdims=True)
    xc = x - mu
    var = jnp.mean(xc * xc, axis=-1, keepdims=True)
    return xc * lax.rsqrt(var + NORM_EPS) * g + b


def _mm_kernel(x_ref, w_ref, o_ref):
    o_ref[...] = _dot(x_ref[...], w_ref[...]).astype(o_ref.dtype)


def _matmul(x, w, out_dtype, tm, tn, name):
    m, k = x.shape
    n = w.shape[1]
    return pl.pallas_call(
        _mm_kernel,
        out_shape=jax.ShapeDtypeStruct((m, n), out_dtype),
        grid=(n // tn, m // tm),
        in_specs=[pl.BlockSpec((tm, k), lambda j, i: (i, 0)),
                  pl.BlockSpec((k, tn), lambda j, i: (0, j))],
        out_specs=pl.BlockSpec((tm, tn), lambda j, i: (i, j)),
        compiler_params=_cparams(2),
        name=name,
    )(x, w)


def _sb_kernel(q_ref, k_ref, v_ref, o_ref, *, tq, tk):
    i = pl.program_id(1)
    q = q_ref[...]
    scale = HEAD_DIM ** -0.5
    n_kb = (i + 1) * (tq // tk)
    row = i * tq + lax.broadcasted_iota(jnp.int32, (tq, tk), 0)
    col0 = lax.broadcasted_iota(jnp.int32, (tq, tk), 1)
    jj = lax.broadcasted_iota(jnp.int32, (tk, 2 * tk), 0)
    ss = lax.broadcasted_iota(jnp.int32, (tk, 2 * tk), 1)
    sel = jnp.where((jj > ss) | (ss >= tk), 1.0, 0.0).astype(BF16)

    def body(step, carry):
        acc, run = carry
        kb = n_kb - 1 - step
        ks = pl.multiple_of(kb * tk, tk)
        kblk = k_ref[pl.ds(ks, tk), :]
        vblk = v_ref[pl.ds(ks, tk), :]
        z = _dot_nt(q, kblk) * scale
        causal = (col0 + ks) < row
        log_keep = jnp.where(causal, -_softplus(z), 0.0)
        sums = _dot_sel_rhs(log_keep, sel)
        within = sums[:, :tk]
        total = sums[:, tk:]
        a = jnp.where(causal, jnp.exp(z + log_keep + within + run), 0.0)
        acc = acc + _dot(a.astype(BF16), vblk)
        return acc, run + total

    acc, _ = lax.fori_loop(
        0, n_kb, body, (jnp.zeros((tq, HEAD_DIM), F32), jnp.zeros((tq, tk), F32)))
    o_ref[...] = acc.astype(o_ref.dtype)


def _sb_attention(zm, tq, tk):
    t = zm.shape[0]
    cq, ck, cv = A_Q // HEAD_DIM, A_K // HEAD_DIM, A_V // HEAD_DIM
    return pl.pallas_call(
        functools.partial(_sb_kernel, tq=tq, tk=tk),
        out_shape=jax.ShapeDtypeStruct((t, BRANCH_WIDTH), BF16),
        grid=(HEADS, t // tq),
        in_specs=[pl.BlockSpec((tq, HEAD_DIM), lambda h, i: (i, cq + h)),
                  pl.BlockSpec((t, HEAD_DIM), lambda h, i: (0, ck + h)),
                  pl.BlockSpec((t, HEAD_DIM), lambda h, i: (0, cv + h))],
        out_specs=pl.BlockSpec((tq, HEAD_DIM), lambda h, i: (i, h)),
        compiler_params=_cparams(2),
        name="sb_attention",
    )(zm, zm, zm)


def _gm_kernel(u_ref, v_ref, ng_ref, nb_ref, ws_ref, bs_ref, o_ref, *, n_chunk):
    u = jax.nn.gelu(u_ref[...].astype(F32))
    v = jax.nn.gelu(v_ref[...].astype(F32))
    v = _layer_norm(v, ng_ref[...], nb_ref[...])
    ii = lax.broadcasted_iota(jnp.int32, (GM_CHUNK, GM_CHUNK), 0)
    jj = lax.broadcasted_iota(jnp.int32, (GM_CHUNK, GM_CHUNK), 1)
    for g in range(HEADS):
        w = jnp.where(ii >= jj, ws_ref[g], 0.0).astype(BF16)
        bias = bs_ref[g]
        for c in range(n_chunk):
            rows = slice(c * GM_CHUNK, (c + 1) * GM_CHUNK)
            cols = slice(g * HEAD_DIM, (g + 1) * HEAD_DIM)
            mixed = _dot(w, v[rows, cols].astype(BF16)) + bias
            o_ref[rows, cols] = (u[rows, cols] * mixed).astype(o_ref.dtype)


def _gmlp(zm, norm_g, norm_b, w_s, b_s, tb):
    t = zm.shape[0]
    bias = jnp.broadcast_to(b_s[:, :, None], (HEADS, GM_CHUNK, HEAD_DIM))
    cu, cv = B_U // BRANCH_WIDTH, B_V // BRANCH_WIDTH
    return pl.pallas_call(
        functools.partial(_gm_kernel, n_chunk=tb // GM_CHUNK),
        out_shape=jax.ShapeDtypeStruct((t, BRANCH_WIDTH), BF16),
        grid=(t // tb,),
        in_specs=[pl.BlockSpec((tb, BRANCH_WIDTH), lambda i: (i, cu)),
                  pl.BlockSpec((tb, BRANCH_WIDTH), lambda i: (i, cv)),
                  pl.BlockSpec((1, BRANCH_WIDTH), lambda i: (0, 0)),
                  pl.BlockSpec((1, BRANCH_WIDTH), lambda i: (0, 0)),
                  pl.BlockSpec((HEADS, GM_CHUNK, GM_CHUNK), lambda i: (0, 0, 0)),
                  pl.BlockSpec((HEADS, GM_CHUNK, HEAD_DIM), lambda i: (0, 0, 0))],
        out_specs=pl.BlockSpec((tb, BRANCH_WIDTH), lambda i: (i, 0)),
        compiler_params=_cparams(1),
        name="gmlp",
    )(zm, zm, norm_g.reshape(1, -1), norm_b.reshape(1, -1), w_s, bias)


def _gdn_kernel(q_ref, k_ref, v_ref, gate_ref, zs_ref, cw_ref, par_ref, ng_ref, o_ref,
                xs_ref, s_ref):
    c = GDN_CHUNK
    w3 = 3 * BRANCH_WIDTH

    @pl.when(pl.program_id(0) == 0)
    def _():
        xs_ref[0:8, :] = jnp.zeros((8, w3), F32)
        s_ref[...] = jnp.zeros(s_ref.shape, F32)

    xs_ref[8:8 + c, 0:BRANCH_WIDTH] = q_ref[...].astype(F32)
    xs_ref[8:8 + c, BRANCH_WIDTH:2 * BRANCH_WIDTH] = k_ref[...].astype(F32)
    xs_ref[8:8 + c, 2 * BRANCH_WIDTH:w3] = v_ref[...].astype(F32)
    cw = cw_ref[...]
    y = xs_ref[5:5 + c, :] * cw[0:1, :]
    for tap in range(1, GDN_CONV):
        y = y + xs_ref[5 + tap:5 + tap + c, :] * cw[tap:tap + 1, :]
    xs_ref[0:8, :] = xs_ref[c:c + 8, :]
    y = _silu(y)

    zs = zs_ref[...]
    beta_all = _sigmoid(zs)
    g_all = -jnp.exp(par_ref[0:1, :]) * _softplus(zs + par_ref[1:2, :])

    ii = lax.broadcasted_iota(jnp.int32, (c, c), 0)
    jj = lax.broadcasted_iota(jnp.int32, (c, c), 1)
    tri_incl = jnp.where(ii >= jj, 1.0, 0.0).astype(BF16)
    ones_cc = jnp.ones((c, c), BF16)

    for h in range(HEADS):
        lanes = slice(h * HEAD_DIM, (h + 1) * HEAD_DIM)
        qh = y[:, h * HEAD_DIM:(h + 1) * HEAD_DIM]
        kh = y[:, BRANCH_WIDTH + h * HEAD_DIM:BRANCH_WIDTH + (h + 1) * HEAD_DIM]
        vh = y[:, 2 * BRANCH_WIDTH + h * HEAD_DIM:2 * BRANCH_WIDTH + (h + 1) * HEAD_DIM]
        qh = qh * lax.rsqrt(jnp.sum(qh * qh, axis=-1, keepdims=True) + 1e-6) * HEAD_DIM ** -0.5
        kh = kh * lax.rsqrt(jnp.sum(kh * kh, axis=-1, keepdims=True) + 1e-6)
        beta = beta_all[:, S_BETA + h:S_BETA + h + 1]
        g_b = jnp.broadcast_to(g_all[:, S_A + h:S_A + h + 1], (c, HEAD_DIM))
        gcol = _dot_sel_lhs(tri_incl, g_b)
        grow = _dot_sel_lhs(ones_cc, jnp.where(ii <= jj, g_b[:, :c], 0.0))
        decay = jnp.where(ii >= jj, jnp.exp(gcol[:, :c] - grow), 0.0)
        expg = jnp.exp(gcol)
        kb = kh * beta
        kh16 = kh.astype(BF16)
        lmat = jnp.where(ii > jj, _dot_nt(kb.astype(BF16), kh16) * decay, 0.0)
        x = jnp.concatenate([vh * beta, kb * expg], axis=1)
        p = -lmat
        for r in range(6):
            x = x + _dot_f32(p, x)
            if r < 5:
                p = _dot_f32(p, p)
        u = x[:, :HEAD_DIM]
        w = x[:, HEAD_DIM:]
        attn = _dot_nt(qh.astype(BF16), kh16) * decay

        s = s_ref[h]
        s16 = s.astype(BF16)
        v_new = u - _dot(w.astype(BF16), s16)
        o = _dot((qh * expg).astype(BF16), s16) + _dot(attn.astype(BF16), v_new.astype(BF16))
        g_last = gcol[c - 1:c, :]
        k_dec = kh * jnp.exp(g_last - gcol)
        s_ref[h] = s * jnp.exp(g_last) + _dot_tn(k_dec.astype(BF16), v_new.astype(BF16))

        o = o * lax.rsqrt(jnp.mean(o * o, axis=-1, keepdims=True) + NORM_EPS) * ng_ref[...]
        o_ref[:, lanes] = (o * _silu(gate_ref[:, lanes].astype(F32))).astype(o_ref.dtype)


def _gdn(zm, zs, conv_w, a_log, dt_bias, norm_g):
    t = zm.shape[0]
    c = GDN_CHUNK
    par = jnp.zeros((8, SMALL_COLS), F32)
    par = par.at[0, S_A:S_A + HEADS].set(a_log).at[1, S_A:S_A + HEADS].set(dt_bias)
    blk = lambda col: pl.BlockSpec((c, BRANCH_WIDTH), lambda n: (n, col // BRANCH_WIDTH))
    return pl.pallas_call(
        _gdn_kernel,
        out_shape=jax.ShapeDtypeStruct((t, BRANCH_WIDTH), BF16),
        grid=(t // c,),
        in_specs=[blk(C_Q), blk(C_K), blk(C_V), blk(C_G),
                  pl.BlockSpec((c, SMALL_COLS), lambda n: (n, 0)),
                  pl.BlockSpec((GDN_CONV, 3 * BRANCH_WIDTH), lambda n: (0, 0)),
                  pl.BlockSpec((8, SMALL_COLS), lambda n: (0, 0)),
                  pl.BlockSpec((1, HEAD_DIM), lambda n: (0, 0))],
        out_specs=pl.BlockSpec((c, BRANCH_WIDTH), lambda n: (n, 0)),
        scratch_shapes=[pltpu.VMEM((c + 8, 3 * BRANCH_WIDTH), F32),
                        pltpu.VMEM((HEADS, HEAD_DIM, HEAD_DIM), F32)],
        compiler_params=_cparams(1),
        name="gated_deltanet",
    )(zm, zm, zm, zm, zs, conv_w, par, norm_g.reshape(1, -1))


def _ml_kernel(q_ref, k_ref, v_ref, og_ref, zs_ref, par_ref, ng_ref, o_ref,
               c_ref, n_ref, m_ref):
    c = ML_CHUNK

    @pl.when(pl.program_id(0) == 0)
    def _():
        c_ref[...] = jnp.zeros(c_ref.shape, F32)
        n_ref[...] = jnp.zeros(n_ref.shape, F32)
        m_ref[...] = jnp.zeros(m_ref.shape, F32)

    zz = zs_ref[...] + par_ref[2:3, :]
    logf_all = -_softplus(-zz)

    ii = lax.broadcasted_iota(jnp.int32, (c, c), 0)
    jj = lax.broadcasted_iota(jnp.int32, (c, c), 1)
    tri_incl = jnp.where(ii >= jj, 1.0, 0.0).astype(BF16)
    ones_cc = jnp.ones((c, c), BF16)
    ones_cl = jnp.ones((c, HEAD_DIM), BF16)

    for h in range(HEADS):
        lanes = slice(h * HEAD_DIM, (h + 1) * HEAD_DIM)
        qk_lanes = slice(h * ML_QK, (h + 1) * ML_QK)
        qs = (q_ref[:, qk_lanes].astype(F32) * ML_QK ** -0.5).astype(BF16)
        kh = k_ref[:, qk_lanes]
        vh = v_ref[:, lanes]
        i_b = jnp.broadcast_to(zz[:, S_I + h:S_I + h + 1], (c, HEAD_DIM))
        lf_b = jnp.broadcast_to(logf_all[:, S_F + h:S_F + h + 1], (c, HEAD_DIM))
        bcol = _dot_sel_lhs(tri_incl, lf_b)
        rv = i_b - bcol
        rrow = _dot_sel_lhs(ones_cc, jnp.where(ii == jj, rv[:, :c], 0.0))
        log_intra = jnp.where(ii >= jj, bcol[:, :c] + rrow, -jnp.inf)
        m_prev = m_ref[h][0:1, :]
        log_inter = bcol + m_prev
        m_out = jnp.maximum(log_inter, jnp.max(log_intra, axis=1, keepdims=True))
        w_inter = jnp.exp(log_inter - m_out)
        s = jnp.exp(log_intra - m_out[:, :c]) * _dot_nt(qs, kh)
        c_mat = c_ref[h]
        n_b = n_ref[h]
        num = w_inter * _dot(qs, c_mat.astype(BF16)) + _dot(s.astype(BF16), vh)
        den = w_inter * _dot(qs, n_b.astype(BF16)) + jnp.sum(s, axis=1, keepdims=True)
        hid = num / jnp.maximum(jnp.abs(den), jnp.exp(-m_out))

        b_last = bcol[c - 1:c, :]
        log_kv = b_last - bcol + i_b
        m_new = jnp.maximum(b_last + m_prev, jnp.max(log_kv, axis=0, keepdims=True))
        w_kv = jnp.exp(log_kv - m_new)
        carry_decay = jnp.exp(b_last + m_prev - m_new)
        kw = (kh.astype(F32) * w_kv[:, :ML_QK]).astype(BF16)
        c_ref[h] = carry_decay * c_mat + _dot_tn(kw, vh)
        n_ref[h] = carry_decay * n_b + _dot_tn(kw, ones_cl)
        m_ref[h] = jnp.broadcast_to(m_new, (8, HEAD_DIM))

        hid = hid * lax.rsqrt(jnp.mean(hid * hid, axis=-1, keepdims=True) + NORM_EPS)
        hid = hid * ng_ref[:, lanes]
        o_ref[:, lanes] = (hid * _sigmoid(og_ref[:, lanes].astype(F32))).astype(o_ref.dtype)


def _mlstm(zm, zs, i_bias, f_bias, norm_g):
    t = zm.shape[0]
    c = ML_CHUNK
    qk_w = HEADS * ML_QK
    par = jnp.zeros((8, SMALL_COLS), F32)
    par = par.at[2, S_I:S_I + HEADS].set(i_bias).at[2, S_F:S_F + HEADS].set(f_bias)
    return pl.pallas_call(
        _ml_kernel,
        out_shape=jax.ShapeDtypeStruct((t, BRANCH_WIDTH), BF16),
        grid=(t // c,),
        in_specs=[pl.BlockSpec((c, qk_w), lambda n: (n, D_Q // qk_w)),
                  pl.BlockSpec((c, qk_w), lambda n: (n, D_K // qk_w)),
                  pl.BlockSpec((c, BRANCH_WIDTH), lambda n: (n, D_V // BRANCH_WIDTH)),
                  pl.BlockSpec((c, BRANCH_WIDTH), lambda n: (n, D_O // BRANCH_WIDTH)),
                  pl.BlockSpec((c, SMALL_COLS), lambda n: (n, 0)),
                  pl.BlockSpec((8, SMALL_COLS), lambda n: (0, 0)),
                  pl.BlockSpec((1, BRANCH_WIDTH), lambda n: (0, 0))],
        out_specs=pl.BlockSpec((c, BRANCH_WIDTH), lambda n: (n, 0)),
        scratch_shapes=[pltpu.VMEM((HEADS, ML_QK, HEAD_DIM), F32),
                        pltpu.VMEM((HEADS, ML_QK, HEAD_DIM), F32),
                        pltpu.VMEM((HEADS, 8, HEAD_DIM), F32)],
        compiler_params=_cparams(1),
        name="mlstm",
    )(zm, zm, zm, zm, zs, par, norm_g.reshape(1, -1))


def _merge_kernel(a_ref, b_ref, c_ref, d_ref, g0_ref, g1_ref, g2_ref, g3_ref, wb_ref, o_ref):
    acc = None
    for g, (br, gr) in enumerate(((a_ref, g0_ref), (b_ref, g1_ref), (c_ref, g2_ref), (d_ref, g3_ref))):
        term = _sigmoid(gr[...].astype(F32)) * _dot(br[...], wb_ref[g])
        acc = term if acc is None else acc + term
    o_ref[...] = acc.astype(o_ref.dtype)


def _merge(outs, zm, wb16, tm):
    t = zm.shape[0]
    branch = pl.BlockSpec((tm, BRANCH_WIDTH), lambda i: (i, 0))
    gate = lambda g: pl.BlockSpec((tm, D_MODEL), lambda i: (i, GATES // D_MODEL + g))
    return pl.pallas_call(
        _merge_kernel,
        out_shape=jax.ShapeDtypeStruct((t, D_MODEL), BF16),
        grid=(t // tm,),
        in_specs=[branch] * 4 + [gate(0), gate(1), gate(2), gate(3),
                                 pl.BlockSpec((N_BRANCHES, BRANCH_WIDTH, D_MODEL), lambda i: (0, 0, 0))],
        out_specs=pl.BlockSpec((tm, D_MODEL), lambda i: (i, 0)),
        compiler_params=_cparams(1),
        name="merge_branches",
    )(*outs, zm, zm, zm, zm, wb16)


def _proj_ln_kernel(a_ref, w_ref, x_ref, g_ref, b_ref, o_ref):
    y = ALPHA * x_ref[...] + _dot(a_ref[...], w_ref[...])
    o_ref[...] = _layer_norm(y, g_ref[...], b_ref[...])


def _proj_ln(a, w16, x, g, b, tm):
    t, k = a.shape
    return pl.pallas_call(
        _proj_ln_kernel,
        out_shape=jax.ShapeDtypeStruct((t, D_MODEL), F32),
        grid=(t // tm,),
        in_specs=[pl.BlockSpec((tm, k), lambda i: (i, 0)),
                  pl.BlockSpec((k, D_MODEL), lambda i: (0, 0)),
                  pl.BlockSpec((tm, D_MODEL), lambda i: (i, 0)),
                  pl.BlockSpec((1, D_MODEL), lambda i: (0, 0)),
                  pl.BlockSpec((1, D_MODEL), lambda i: (0, 0))],
        out_specs=pl.BlockSpec((tm, D_MODEL), lambda i: (i, 0)),
        compiler_params=_cparams(1),
        name="out_proj_layernorm",
    )(a, w16, x, g.reshape(1, -1), b.reshape(1, -1))


def _router_kernel(x_ref, wr_ref, bias_ref, idx_ref, w_ref, *, tm):
    logits = _dot_f32_nt(wr_ref[...], x_ref[...])
    scores = _sigmoid(logits)
    biased = scores + bias_ref[...]
    neg = -jnp.inf

    b3 = biased.reshape(N_GROUPS, GROUP_SIZE, tm)
    pos = lax.broadcasted_iota(jnp.int32, (N_GROUPS, GROUP_SIZE, tm), 1)
    m1 = jnp.max(b3, axis=1, keepdims=True)
    first = jnp.min(jnp.where(b3 == m1, pos, GROUP_SIZE), axis=1, keepdims=True)
    m2 = jnp.max(jnp.where(pos == first, neg, b3), axis=1, keepdims=True)
    gscore = m1 + m2

    gidx = lax.broadcasted_iota(jnp.int32, (N_GROUPS, 1, tm), 0)
    gsel = jnp.zeros((N_GROUPS, 1, tm), F32)
    for _ in range(TOPK_GROUPS):
        gm = jnp.max(gscore, axis=0, keepdims=True)
        gfirst = jnp.min(jnp.where(gscore == gm, gidx, N_GROUPS), axis=0, keepdims=True)
        hit = gidx == gfirst
        gsel = jnp.where(hit, 1.0, gsel)
        gscore = jnp.where(hit, neg, gscore)

    allowed = jnp.broadcast_to(gsel, (N_GROUPS, GROUP_SIZE, tm)) > 0.0
    masked = jnp.where(allowed, b3, neg).reshape(N_EXPERTS, tm)
    eidx = lax.broadcasted_iota(jnp.int32, (N_EXPERTS, tm), 0)
    idx_rows, sel_rows = [], []
    for _ in range(TOP_K):
        em = jnp.max(masked, axis=0, keepdims=True)
        efirst = jnp.min(jnp.where(masked == em, eidx, N_EXPERTS), axis=0, keepdims=True)
        hit = eidx == efirst
        idx_rows.append(efirst)
        sel_rows.append(jnp.sum(jnp.where(hit, scores, 0.0), axis=0, keepdims=True))
        masked = jnp.where(hit, neg, masked)
    total = sel_rows[0]
    for r in sel_rows[1:]:
        total = total + r
    zero_i = jnp.zeros((1, tm), jnp.int32)
    zero_f = jnp.zeros((1, tm), F32)
    idx_ref[...] = jnp.concatenate(idx_rows + [zero_i, zero_i], axis=0)
    w_ref[...] = jnp.concatenate([r / total * ROUTED_SCALE for r in sel_rows] + [zero_f, zero_f], axis=0)


def _dot_f32_nt(a, b):
    ah, al = _split2(a)
    bh, bl = _split2(b)
    return _dot_nt(ah, bh) + _dot_nt(ah, bl) + _dot_nt(al, bh)


def _router(x, w_router, router_bias, tm):
    t = x.shape[0]
    return pl.pallas_call(
        functools.partial(_router_kernel, tm=tm),
        out_shape=(jax.ShapeDtypeStruct((8, t), jnp.int32), jax.ShapeDtypeStruct((8, t), F32)),
        grid=(t // tm,),
        in_specs=[pl.BlockSpec((tm, D_MODEL), lambda i: (i, 0)),
                  pl.BlockSpec((N_EXPERTS, D_MODEL), lambda i: (0, 0)),
                  pl.BlockSpec((N_EXPERTS, 1), lambda i: (0, 0))],
        out_specs=(pl.BlockSpec((8, tm), lambda i: (0, i)), pl.BlockSpec((8, tm), lambda i: (0, i))),
        compiler_params=_cparams(1),
        name="router_topk",
    )(x, w_router.T, router_bias.reshape(N_EXPERTS, 1))


def _expert_kernel(n_ref, tile_ref, exp_ref, lo_ref, hi_ref, src_ref, src_next_ref, dst_ref,
                   x_hbm, wg_ref, wu_ref, wd_ref, y_hbm,
                   xbuf, ybuf, wg16, wu16, wd16, gsem, ssem, *, tm, n_tiles, n_max):
    j = pl.program_id(0)
    n_items = n_ref[0]
    tile = tile_ref[j]
    prev = jnp.maximum(j - 1, 0)
    first = (j == 0) | (tile_ref[prev] != tile)
    last = (j + 1 == n_items) | (tile_ref[jnp.minimum(j + 1, n_max - 1)] != tile)
    slot = lax.rem(tile, 2)

    def gather(idx_ref, s):
        def issue(r, carry):
            pltpu.make_async_copy(x_hbm.at[pl.ds(idx_ref[0, r], 1), :],
                                  xbuf.at[s, pl.ds(r, 1), :], gsem.at[s]).start()
            return carry
        lax.fori_loop(0, tm, issue, 0, unroll=8)

    def scatter(s):
        def issue(r, carry):
            pltpu.make_async_copy(ybuf.at[s, pl.ds(r, 1), :],
                                  y_hbm.at[pl.ds(dst_ref[0, r], 1), :], ssem.at[s]).start()
            return carry
        lax.fori_loop(0, tm, issue, 0, unroll=8)

    def wait_gather(s):
        pltpu.make_async_copy(x_hbm.at[pl.ds(0, tm), :], xbuf.at[s], gsem.at[s]).wait()

    def wait_scatter(s):
        pltpu.make_async_copy(ybuf.at[s], y_hbm.at[pl.ds(0, tm), :], ssem.at[s]).wait()

    @pl.when(j < n_items)
    def _():
        @pl.when(first)
        def _():
            @pl.when(j == 0)
            def _():
                gather(src_ref, 0)

            @pl.when(tile + 1 < n_tiles)
            def _():
                gather(src_next_ref, 1 - slot)

            wait_gather(slot)

        @pl.when((j == 0) | (exp_ref[prev] != exp_ref[j]))
        def _():
            wg16[...] = wg_ref[...].astype(BF16)
            wu16[...] = wu_ref[...].astype(BF16)
            wd16[...] = wd_ref[...].astype(BF16)

        x = xbuf[slot].astype(BF16)
        hidden = _silu(_dot(x, wg16[...])) * _dot(x, wu16[...])
        y = _dot(hidden.astype(BF16), wd16[...])

        @pl.when(first)
        def _():
            ybuf[slot] = y

        @pl.when(jnp.logical_not(first))
        def _():
            row = lax.broadcasted_iota(jnp.int32, (tm, 1), 0)
            mine = (row >= lo_ref[j]) & (row < hi_ref[j])
            ybuf[slot] = jnp.where(mine, y, ybuf[slot])

        @pl.when(last)
        def _():
            @pl.when(tile >= 1)
            def _():
                wait_scatter(1 - slot)

            scatter(slot)

            @pl.when(j + 1 == n_items)
            def _():
                wait_scatter(slot)


def _experts(x, sched, w_gate, w_up, w_down, tm):
    n_items, item_tile, item_exp, item_lo, item_hi, row_src, row_dst = sched
    n_tiles = row_src.shape[0]
    n_max = item_tile.shape[0]
    idx_spec = lambda f: pl.BlockSpec((None, 1, tm), f, memory_space=pltpu.SMEM)
    tile_of = lambda j, n, tl, ex, lo, hi: tl[j]
    next_of = lambda j, n, tl, ex, lo, hi: jnp.minimum(tl[j] + 1, n_tiles - 1)
    e_of = lambda j, n, tl, ex, lo, hi: ex[j]
    return pl.pallas_call(
        functools.partial(_expert_kernel, tm=tm, n_tiles=n_tiles, n_max=n_max),
        out_shape=jax.ShapeDtypeStruct((n_tiles * tm, D_MODEL), F32),
        grid_spec=pltpu.PrefetchScalarGridSpec(
            num_scalar_prefetch=5,
            grid=(n_max,),
            in_specs=[idx_spec(lambda *a: (tile_of(*a), 0, 0)),
                      idx_spec(lambda *a: (next_of(*a), 0, 0)),
                      idx_spec(lambda *a: (tile_of(*a), 0, 0)),
                      pl.BlockSpec(memory_space=pl.ANY),
                      pl.BlockSpec((None, D_MODEL, EXPERT_DIM), lambda *a: (e_of(*a), 0, 0)),
                      pl.BlockSpec((None, D_MODEL, EXPERT_DIM), lambda *a: (e_of(*a), 0, 0)),
                      pl.BlockSpec((None, EXPERT_DIM, D_MODEL), lambda *a: (e_of(*a), 0, 0))],
            out_specs=pl.BlockSpec(memory_space=pl.ANY),
            scratch_shapes=[pltpu.VMEM((2, tm, D_MODEL), F32),
                            pltpu.VMEM((2, tm, D_MODEL), F32),
                            pltpu.VMEM((D_MODEL, EXPERT_DIM), BF16),
                            pltpu.VMEM((D_MODEL, EXPERT_DIM), BF16),
                            pltpu.VMEM((EXPERT_DIM, D_MODEL), BF16),
                            pltpu.SemaphoreType.DMA((2,)),
                            pltpu.SemaphoreType.DMA((2,))]),
        compiler_params=_cparams(1),
        name="routed_experts",
    )(n_items, item_tile, item_exp, item_lo, item_hi, row_src, row_src, row_dst, x, w_gate, w_up, w_down)


def _expert_schedule(idx, t, tm):
    i32 = jnp.int32
    n_rows = t * TOP_K
    n_tiles = n_rows // tm
    n_max = n_tiles + N_EXPERTS - 1
    flat_e = idx.reshape(-1)
    order = jnp.argsort(flat_e, stable=True).astype(i32)
    token = order // TOP_K
    row_src = token.reshape(n_tiles, 1, tm)
    row_dst = ((order % TOP_K) * t + token).reshape(n_tiles, 1, tm)
    counts = jnp.zeros((N_EXPERTS,), i32).at[flat_e].add(1)
    end = jnp.cumsum(counts)
    start = end - counts
    first_tile = start // tm
    items_per_e = jnp.where(counts > 0, (end - 1) // tm - first_tile + 1, 0)
    item_end = jnp.cumsum(items_per_e)
    n_items = item_end[-1]
    j = jnp.arange(n_max, dtype=i32)
    e = jnp.minimum(jnp.searchsorted(item_end, j, side="right").astype(i32), N_EXPERTS - 1)
    tile = first_tile[e] + j - (item_end[e] - items_per_e[e])
    lo = jnp.maximum(start[e], tile * tm) - tile * tm
    hi = jnp.minimum(end[e], (tile + 1) * tm) - tile * tm
    valid = j < n_items
    last = jnp.maximum(n_items - 1, 0)
    pick = lambda a: jnp.where(valid, a, a[last]).astype(i32)
    return (n_items.reshape(1).astype(i32), pick(tile), pick(e), pick(lo), pick(hi), row_src, row_dst)


def _combine_kernel(x_ref, w_ref, y0, y1, y2, y3, y4, y5, sg_ref, su_ref, sd_ref, g_ref, b_ref,
                    o_ref, o16_ref):
    x = x_ref[...]
    x16 = x.astype(BF16)
    hidden = _silu(_dot(x16, sg_ref[...])) * _dot(x16, su_ref[...])
    acc = _dot(hidden.astype(BF16), sd_ref[...])
    w = w_ref[...]
    for k, yr in enumerate((y0, y1, y2, y3, y4, y5)):
        acc = acc + yr[...] * w[:, k:k + 1]
    out = _layer_norm(ALPHA * x + acc, g_ref[...], b_ref[...])
    o_ref[...] = out
    o16_ref[...] = out.astype(BF16)


def _combine(x, w_tok, y_slots, sg16, su16, sd16, g, b, tt):
    t = x.shape[0]
    nb = t // tt
    tile = pl.BlockSpec((tt, D_MODEL), lambda i: (i, 0))
    plane = lambda k: pl.BlockSpec((tt, D_MODEL), lambda i: (k * nb + i, 0))
    vec = pl.BlockSpec((1, D_MODEL), lambda i: (0, 0))
    return pl.pallas_call(
        _combine_kernel,
        out_shape=(jax.ShapeDtypeStruct((t, D_MODEL), F32), jax.ShapeDtypeStruct((t, D_MODEL), BF16)),
        grid=(nb,),
        in_specs=[tile, pl.BlockSpec((tt, 8), lambda i: (i, 0))] + [plane(k) for k in range(TOP_K)] +
                 [pl.BlockSpec((D_MODEL, SHARED_DIM), lambda i: (0, 0)),
                  pl.BlockSpec((D_MODEL, SHARED_DIM), lambda i: (0, 0)),
                  pl.BlockSpec((SHARED_DIM, D_MODEL), lambda i: (0, 0)), vec, vec],
        out_specs=(tile, tile),
        compiler_params=_cparams(1),
        name="moe_combine_layernorm",
    )(x, w_tok, *([y_slots] * TOP_K), sg16, su16, sd16, g.reshape(1, -1), b.reshape(1, -1))


def _tile(t, pref):
    return min(t, pref)


def _repack_w_in(w_in):
    main = jnp.concatenate([w_in[:, :4608], w_in[:, 4616:6152], w_in[:, 6160:]], axis=1).astype(BF16)
    small = jnp.concatenate([w_in[:, 4608:4616], w_in[:, 6152:6160],
                             jnp.zeros((D_MODEL, SMALL_COLS - 16), F32)], axis=1).astype(BF16)
    return main, small


def _mixer(x, x16, p):
    t = x.shape[0]
    w_main, w_small = _repack_w_in(p["w_in"])
    zm = _matmul(x16, w_main, BF16, _tile(t, 512), 1024, "in_proj_main")
    zs = _matmul(x16, w_small, F32, _tile(t, 1024), SMALL_COLS, "in_proj_gates")
    out_a = _sb_attention(zm, _tile(t, 256), 128)
    out_b = _gmlp(zm, p["gm_norm_g"], p["gm_norm_b"], p["gm_w_s"], p["gm_b_s"], _tile(t, 512))
    out_c = _gdn(zm, zs, p["gdn_conv_w"], p["gdn_a_log"], p["gdn_dt_bias"], p["gdn_norm_g"])
    out_d = _mlstm(zm, zs, p["ml_i_bias"], p["ml_f_bias"], p["ml_norm_g"])
    merged = _merge((out_a, out_b, out_c, out_d), zm, p["w_branch"].astype(BF16), _tile(t, 256))
    return _proj_ln(merged, p["w_out"].astype(BF16), x, p["ln1_g"], p["ln1_b"], _tile(t, 256))


def _moe(x, p, tm):
    t = x.shape[0]
    idx_t, w_t = _router(x, p["w_router"], p["router_bias"], _tile(t, 512))
    idx = idx_t[:TOP_K].T
    w_tok = w_t.T
    y_slots = _experts(x, _expert_schedule(idx, t, tm), p["w_gate"], p["w_up"], p["w_down"], tm)
    return _combine(x, w_tok, y_slots, p["ws_gate"].astype(BF16), p["ws_up"].astype(BF16),
                    p["ws_down"].astype(BF16), p["ln2_g"], p["ln2_b"], _tile(t, 256))


def _layer(x, x16, p, tm_expert):
    x1 = _mixer(x, x16, p)
    return _moe(x1, p, tm_expert)


def kernel(x, w_in, gm_norm_g, gm_norm_b, gm_w_s, gm_b_s, gdn_conv_w, gdn_a_log, gdn_dt_bias, gdn_norm_g, ml_i_bias, ml_f_bias, ml_norm_g, w_branch, w_out, ln1_g, ln1_b, w_router, router_bias, w_gate, w_up, w_down, ws_gate, ws_up, ws_down, ln2_g, ln2_b):
    params = dict(w_in=w_in, gm_norm_g=gm_norm_g, gm_norm_b=gm_norm_b, gm_w_s=gm_w_s, gm_b_s=gm_b_s,
                  gdn_conv_w=gdn_conv_w, gdn_a_log=gdn_a_log, gdn_dt_bias=gdn_dt_bias,
                  gdn_norm_g=gdn_norm_g, ml_i_bias=ml_i_bias, ml_f_bias=ml_f_bias, ml_norm_g=ml_norm_g,
                  w_branch=w_branch, w_out=w_out, ln1_g=ln1_g, ln1_b=ln1_b, w_router=w_router,
                  router_bias=router_bias, w_gate=w_gate, w_up=w_up, w_down=w_down, ws_gate=ws_gate,
                  ws_up=ws_up, ws_down=ws_down, ln2_g=ln2_g, ln2_b=ln2_b)
    b, t, d = x.shape
    h = x.reshape(b * t, d)
    h16 = h.astype(BF16)
    for l in range(DEPTH):
        h, h16 = _layer(h, h16, {k: v[l] for k, v in params.items()}, 256)
    return h.reshape(b, t, d)
```

```python
import functools

import jax
import jax.numpy as jnp
from jax import lax
from jax.experimental import pallas as pl
from jax.experimental.pallas import tpu as pltpu

F32 = jnp.float32
BF16 = jnp.bfloat16

D_MODEL = 2048
N_BRANCHES = 4
BRANCH_WIDTH = 512
HEADS = 4
HEAD_DIM = 128
GM_CHUNK = 128
GDN_CHUNK = 64
GDN_CONV = 4
ML_CHUNK = 64
ML_QK = 64
N_EXPERTS = 64
N_GROUPS = 8
GROUP_SIZE = 8
TOPK_GROUPS = 4
TOP_K = 6
EXPERT_DIM = 512
SHARED_DIM = 512
ROUTED_SCALE = 2.5
DEPTH = 2
ALPHA = (2 * DEPTH) ** 0.25
NORM_EPS = 1e-5

A_Q, A_K, A_V = 0, 512, 1024
B_U, B_V = 1536, 2048
C_Q, C_K, C_V, C_G = 2560, 3072, 3584, 4096
D_Q, D_K, D_V, D_O = 4608, 4864, 5120, 5632
GATES = 6144
MAIN_COLS = GATES + N_BRANCHES * D_MODEL
S_BETA, S_A, S_I, S_F = 0, 4, 8, 12
SMALL_COLS = 128

VMEM_LIMIT = 56 * 1024 * 1024


def _cparams(n_axes):
    return pltpu.CompilerParams(
        dimension_semantics=("arbitrary",) * n_axes, vmem_limit_bytes=VMEM_LIMIT)


def _dot(a, b):
    return jnp.dot(a, b, preferred_element_type=F32)


def _dot_nt(a, b):
    return lax.dot_general(a, b, (((1,), (1,)), ((), ())), preferred_element_type=F32)


def _dot_tn(a, b):
    return lax.dot_general(a, b, (((0,), (0,)), ((), ())), preferred_element_type=F32)


def _split2(a):
    hi = a.astype(BF16)
    lo = (a - hi.astype(F32)).astype(BF16)
    return hi, lo


def _split3(a):
    hi = a.astype(BF16)
    r = a - hi.astype(F32)
    mid = r.astype(BF16)
    lo = (r - mid.astype(F32)).astype(BF16)
    return hi, mid, lo


def _dot_sel_lhs(sel, a):
    hi, mid, lo = _split3(a)
    return _dot(sel, hi) + _dot(sel, mid) + _dot(sel, lo)


def _dot_sel_rhs(a, sel):
    hi, lo = _split2(a)
    return _dot(hi, sel) + _dot(lo, sel)


def _dot_f32(a, b):
    ah, al = _split2(a)
    bh, bl = _split2(b)
    return _dot(ah, bh) + _dot(ah, bl) + _dot(al, bh)


def _softplus(x):
    return jnp.maximum(x, 0.0) + jnp.log1p(jnp.exp(-jnp.abs(x)))


def _sigmoid(x):
    return 1.0 / (1.0 + jnp.exp(-x))


def _silu(x):
    return x * _sigmoid(x)


def _layer_norm(x, g, b):
    mu = jnp.mean(x, axis=-1, keepdims=True)
    xc = x - mu
    var = jnp.mean(xc * xc, axis=-1, keepdims=True)
    return xc * lax.rsqrt(var + NORM_EPS) * g + b


def _mm_kernel(x_ref, w_ref, o_ref):
    o_ref[...] = _dot(x_ref[...], w_ref[...]).astype(o_ref.dtype)


def _matmul(x, w, out_dtype, tm, tn, name):
    m, k = x.shape
    n = w.shape[1]
    return pl.pallas_call(
        _mm_kernel,
        out_shape=jax.ShapeDtypeStruct((m, n), out_dtype),
        grid=(n // tn, m // tm),
        in_specs=[pl.BlockSpec((tm, k), lambda j, i: (i, 0)),
                  pl.BlockSpec((k, tn), lambda j, i: (0, j))],
        out_specs=pl.BlockSpec((tm, tn), lambda j, i: (i, j)),
        compiler_params=_cparams(2),
        name=name,
    )(x, w)


def _sb_kernel(q_ref, k_ref, v_ref, o_ref, *, tq, tk):
    i = pl.program_id(1)
    q = q_ref[...]
    scale = HEAD_DIM ** -0.5
    half = tk // 2
    row = i * tq + lax.broadcasted_iota(jnp.int32, (tq, tk), 0)
    col0 = lax.broadcasted_iota(jnp.int32, (tq, tk), 1)
    jj = lax.broadcasted_iota(jnp.int32, (tk, tk), 0) % half
    ss = lax.broadcasted_iota(jnp.int32, (tk, tk), 1)
    sel = jnp.where((jj > ss) | (ss >= half), 1.0, 0.0).astype(BF16)

    def block(ks, acc, run, on_diagonal):
        kblk = k_ref[pl.ds(ks, tk), :]
        vblk = v_ref[pl.ds(ks, tk), :]
        z = _dot_nt(q, kblk) * scale
        sp = jnp.maximum(z, 0.0) + jnp.log(1.0 + jnp.exp(-jnp.abs(z)))
        if on_diagonal:
            causal = (col0 + ks) < row
            sp = jnp.where(causal, sp, 0.0)
        hi, lo = _split2(sp)
        late = _dot(jnp.concatenate([hi[:, half:], lo[:, half:]], axis=1), sel)
        early = _dot(jnp.concatenate([hi[:, :half], lo[:, :half]], axis=1), sel)
        run_mid = run + late[:, half:]
        e_late = z[:, half:] - sp[:, half:] - late[:, :half] - run
        e_early = z[:, :half] - sp[:, :half] - early[:, :half] - run_mid
        a = jnp.exp(jnp.concatenate([e_early, e_late], axis=1))
        if on_diagonal:
            a = jnp.where(causal, a, 0.0)
        return acc + _dot(a.astype(BF16), vblk), run_mid + early[:, half:]

    carry = block(pl.multiple_of(i * tq, tq), jnp.zeros((tq, HEAD_DIM), F32),
                  jnp.zeros((tq, half), F32), True)

    def pair(step, carry):
        for s in range(2):
            carry = block(pl.multiple_of((i - 1 - 2 * step - s) * tk, tk), *carry, False)
        return carry

    def single(step, carry):
        return block(0, *carry, False)

    carry = lax.fori_loop(0, i // 2, pair, carry)
    acc, _ = lax.fori_loop(0, i % 2, single, carry)
    o_ref[...] = acc.astype(o_ref.dtype)


def _sb_attention(zm, tq, tk):
    t = zm.shape[0]
    assert tq == tk and tk % 256 == 0
    cq, ck, cv = A_Q // HEAD_DIM, A_K // HEAD_DIM, A_V // HEAD_DIM
    return pl.pallas_call(
        functools.partial(_sb_kernel, tq=tq, tk=tk),
        out_shape=jax.ShapeDtypeStruct((t, BRANCH_WIDTH), BF16),
        grid=(HEADS, t // tq),
        in_specs=[pl.BlockSpec((tq, HEAD_DIM), lambda h, i: (i, cq + h)),
                  pl.BlockSpec((t, HEAD_DIM), lambda h, i: (0, ck + h)),
                  pl.BlockSpec((t, HEAD_DIM), lambda h, i: (0, cv + h))],
        out_specs=pl.BlockSpec((tq, HEAD_DIM), lambda h, i: (i, h)),
        compiler_params=_cparams(2),
        name="sb_attention",
    )(zm, zm, zm)


def _gm_kernel(u_ref, v_ref, ng_ref, nb_ref, ws_ref, bs_ref, o_ref, *, n_chunk):
    u = jax.nn.gelu(u_ref[...].astype(F32))
    v = jax.nn.gelu(v_ref[...].astype(F32))
    v = _layer_norm(v, ng_ref[...], nb_ref[...])
    ii = lax.broadcasted_iota(jnp.int32, (GM_CHUNK, GM_CHUNK), 0)
    jj = lax.broadcasted_iota(jnp.int32, (GM_CHUNK, GM_CHUNK), 1)
    for g in range(HEADS):
        w = jnp.where(ii >= jj, ws_ref[g], 0.0).astype(BF16)
        bias = bs_ref[g]
        for c in range(n_chunk):
            rows = slice(c * GM_CHUNK, (c + 1) * GM_CHUNK)
            cols = slice(g * HEAD_DIM, (g + 1) * HEAD_DIM)
            mixed = _dot(w, v[rows, cols].astype(BF16)) + bias
            o_ref[rows, cols] = (u[rows, cols] * mixed).astype(o_ref.dtype)


def _gmlp(zm, norm_g, norm_b, w_s, b_s, tb):
    t = zm.shape[0]
    bias = jnp.broadcast_to(b_s[:, :, None], (HEADS, GM_CHUNK, HEAD_DIM))
    cu, cv = B_U // BRANCH_WIDTH, B_V // BRANCH_WIDTH
    return pl.pallas_call(
        functools.partial(_gm_kernel, n_chunk=tb // GM_CHUNK),
        out_shape=jax.ShapeDtypeStruct((t, BRANCH_WIDTH), BF16),
        grid=(t // tb,),
        in_specs=[pl.BlockSpec((tb, BRANCH_WIDTH), lambda i: (i, cu)),
                  pl.BlockSpec((tb, BRANCH_WIDTH), lambda i: (i, cv)),
                  pl.BlockSpec((1, BRANCH_WIDTH), lambda i: (0, 0)),
                  pl.BlockSpec((1, BRANCH_WIDTH), lambda i: (0, 0)),
                  pl.BlockSpec((HEADS, GM_CHUNK, GM_CHUNK), lambda i: (0, 0, 0)),
                  pl.BlockSpec((HEADS, GM_CHUNK, HEAD_DIM), lambda i: (0, 0, 0))],
        out_specs=pl.BlockSpec((tb, BRANCH_WIDTH), lambda i: (i, 0)),
        compiler_params=_cparams(1),
        name="gmlp",
    )(zm, zm, norm_g.reshape(1, -1), norm_b.reshape(1, -1), w_s, bias)


def _gdn_prep_kernel(q_ref, k_ref, v_ref, zs_ref, cw_ref, par_ref,
                     u_ref, w_ref, qe_ref, kd_ref, attn_ref, eg_ref, xs_ref, *, n_ch):
    c = GDN_CHUNK
    w3 = 3 * BRANCH_WIDTH
    rows_all = n_ch * c

    @pl.when(pl.program_id(0) == 0)
    def _():
        xs_ref[0:8, :] = jnp.zeros((8, w3), F32)

    xs_ref[8:8 + rows_all, 0:BRANCH_WIDTH] = q_ref[...].astype(F32)
    xs_ref[8:8 + rows_all, BRANCH_WIDTH:2 * BRANCH_WIDTH] = k_ref[...].astype(F32)
    xs_ref[8:8 + rows_all, 2 * BRANCH_WIDTH:w3] = v_ref[...].astype(F32)
    cw = cw_ref[...]
    y_all = xs_ref[5:5 + rows_all, :] * cw[0:1, :]
    for tap in range(1, GDN_CONV):
        y_all = y_all + xs_ref[5 + tap:5 + tap + rows_all, :] * cw[tap:tap + 1, :]
    xs_ref[0:8, :] = xs_ref[rows_all:rows_all + 8, :]
    y_all = _silu(y_all)

    zs_all = zs_ref[...]
    beta_full = _sigmoid(zs_all)
    g_full = -jnp.exp(par_ref[0:1, :]) * _softplus(zs_all + par_ref[1:2, :])

    ii = lax.broadcasted_iota(jnp.int32, (c, c), 0)
    jj = lax.broadcasted_iota(jnp.int32, (c, c), 1)
    tri_incl = jnp.where(ii >= jj, 1.0, 0.0).astype(BF16)
    ones_cc = jnp.ones((c, c), BF16)

    for ch, h in [(ch, h) for ch in range(n_ch) for h in range(HEADS)]:
        rows = slice(ch * c, (ch + 1) * c)
        y = y_all[rows]
        beta_all = beta_full[rows]
        g_all = g_full[rows]
        lanes = slice(h * HEAD_DIM, (h + 1) * HEAD_DIM)
        qh = y[:, h * HEAD_DIM:(h + 1) * HEAD_DIM]
        kh = y[:, BRANCH_WIDTH + h * HEAD_DIM:BRANCH_WIDTH + (h + 1) * HEAD_DIM]
        vh = y[:, 2 * BRANCH_WIDTH + h * HEAD_DIM:2 * BRANCH_WIDTH + (h + 1) * HEAD_DIM]
        qh = qh * lax.rsqrt(jnp.sum(qh * qh, axis=-1, keepdims=True) + 1e-6) * HEAD_DIM ** -0.5
        kh = kh * lax.rsqrt(jnp.sum(kh * kh, axis=-1, keepdims=True) + 1e-6)
        beta = beta_all[:, S_BETA + h:S_BETA + h + 1]
        g_b = jnp.broadcast_to(g_all[:, S_A + h:S_A + h + 1], (c, HEAD_DIM))
        gcol = _dot_sel_lhs(tri_incl, g_b)
        grow = _dot_sel_lhs(ones_cc, jnp.where(ii <= jj, g_b[:, :c], 0.0))
        decay = jnp.where(ii >= jj, jnp.exp(gcol[:, :c] - grow), 0.0)
        expg = jnp.exp(gcol)
        kb = kh * beta
        kh16 = kh.astype(BF16)
        lmat = jnp.where(ii > jj, _dot_nt(kb.astype(BF16), kh16) * decay, 0.0)
        x = jnp.concatenate([vh * beta, kb * expg], axis=1)
        p = -lmat
        for r in range(6):
            x = x + _dot_f32(p, x)
            if r < 5:
                p = _dot_f32(p, p)
        g_last = gcol[c - 1:c, :]
        u_ref[rows, lanes] = x[:, :HEAD_DIM]
        w_ref[rows, lanes] = x[:, HEAD_DIM:].astype(BF16)
        qe_ref[rows, lanes] = (qh * expg).astype(BF16)
        kd_ref[rows, lanes] = (kh * jnp.exp(g_last - gcol)).astype(BF16)
        attn_ref[h, rows, :] = (_dot_nt(qh.astype(BF16), kh16) * decay).astype(BF16)
        eg_ref[ch, :, lanes] = jnp.broadcast_to(jnp.exp(g_last), (8, HEAD_DIM))


def _gdn_scan_kernel(u_ref, w_ref, qe_ref, kd_ref, attn_ref, eg_ref, gate_ref, ng_ref, o_ref,
                     s_ref, *, n_ch):
    c = GDN_CHUNK

    @pl.when(pl.program_id(0) == 0)
    def _():
        s_ref[...] = jnp.zeros(s_ref.shape, F32)

    for h in range(HEADS):
        lanes = slice(h * HEAD_DIM, (h + 1) * HEAD_DIM)
        s = s_ref[h]
        for ch in range(n_ch):
            rows = slice(ch * c, (ch + 1) * c)
            s16 = s.astype(BF16)
            v_new = (u_ref[rows, lanes] - _dot(w_ref[rows, lanes], s16)).astype(BF16)
            o = _dot(qe_ref[rows, lanes], s16) + _dot(attn_ref[h, rows, :], v_new)
            s = s * eg_ref[ch, 0:1, lanes] + _dot_tn(kd_ref[rows, lanes], v_new)
            o = o * lax.rsqrt(jnp.mean(o * o, axis=-1, keepdims=True) + NORM_EPS) * ng_ref[...]
            o_ref[rows, lanes] = (o * _silu(gate_ref[rows, lanes].astype(F32))).astype(o_ref.dtype)
        s_ref[h] = s


def _gdn(zm, zs, conv_w, a_log, dt_bias, norm_g, prep_chunks, scan_chunks):
    t = zm.shape[0]
    c = GDN_CHUNK
    par = jnp.zeros((8, SMALL_COLS), F32)
    par = par.at[0, S_A:S_A + HEADS].set(a_log).at[1, S_A:S_A + HEADS].set(dt_bias)

    rp = prep_chunks * c
    blk = lambda col: pl.BlockSpec((rp, BRANCH_WIDTH), lambda n: (n, col // BRANCH_WIDTH))
    wide = pl.BlockSpec((rp, BRANCH_WIDTH), lambda n: (n, 0))
    u, w, qe, kd, attn, eg = pl.pallas_call(
        functools.partial(_gdn_prep_kernel, n_ch=prep_chunks),
        out_shape=(jax.ShapeDtypeStruct((t, BRANCH_WIDTH), F32),
                   jax.ShapeDtypeStruct((t, BRANCH_WIDTH), BF16),
                   jax.ShapeDtypeStruct((t, BRANCH_WIDTH), BF16),
                   jax.ShapeDtypeStruct((t, BRANCH_WIDTH), BF16),
                   jax.ShapeDtypeStruct((HEADS, t, c), BF16),
                   jax.ShapeDtypeStruct((t // c, 8, BRANCH_WIDTH), F32)),
        grid=(t // rp,),
        in_specs=[blk(C_Q), blk(C_K), blk(C_V),
                  pl.BlockSpec((rp, SMALL_COLS), lambda n: (n, 0)),
                  pl.BlockSpec((GDN_CONV, 3 * BRANCH_WIDTH), lambda n: (0, 0)),
                  pl.BlockSpec((8, SMALL_COLS), lambda n: (0, 0))],
        out_specs=(wide, wide, wide, wide,
                   pl.BlockSpec((HEADS, rp, c), lambda n: (0, n, 0)),
                   pl.BlockSpec((prep_chunks, 8, BRANCH_WIDTH), lambda n: (n, 0, 0))),
        scratch_shapes=[pltpu.VMEM((rp + 8, 3 * BRANCH_WIDTH), F32)],
        compiler_params=_cparams(1),
        name="gdn_chunk_prep",
    )(zm, zm, zm, zs, conv_w, par)

    rs = scan_chunks * c
    wide = pl.BlockSpec((rs, BRANCH_WIDTH), lambda n: (n, 0))
    return pl.pallas_call(
        functools.partial(_gdn_scan_kernel, n_ch=scan_chunks),
        out_shape=jax.ShapeDtypeStruct((t, BRANCH_WIDTH), BF16),
        grid=(t // rs,),
        in_specs=[wide, wide, wide, wide,
                  pl.BlockSpec((HEADS, rs, c), lambda n: (0, n, 0)),
                  pl.BlockSpec((scan_chunks, 8, BRANCH_WIDTH), lambda n: (n, 0, 0)),
                  pl.BlockSpec((rs, BRANCH_WIDTH), lambda n: (n, C_G // BRANCH_WIDTH)),
                  pl.BlockSpec((1, HEAD_DIM), lambda n: (0, 0))],
        out_specs=wide,
        scratch_shapes=[pltpu.VMEM((HEADS, HEAD_DIM, HEAD_DIM), F32)],
        compiler_params=_cparams(1),
        name="gdn_state_scan",
    )(u, w, qe, kd, attn, eg, zm, norm_g.reshape(1, -1))


def _ml_kernel(q_ref, k_ref, v_ref, og_ref, zs_ref, par_ref, ng_ref, o_ref,
               c_ref, n_ref, m_ref):
    c = ML_CHUNK

    @pl.when(pl.program_id(0) == 0)
    def _():
        c_ref[...] = jnp.zeros(c_ref.shape, F32)
        n_ref[...] = jnp.zeros(n_ref.shape, F32)
        m_ref[...] = jnp.zeros(m_ref.shape, F32)

    zz = zs_ref[...] + par_ref[2:3, :]
    logf_all = -_softplus(-zz)

    ii = lax.broadcasted_iota(jnp.int32, (c, c), 0)
    jj = lax.broadcasted_iota(jnp.int32, (c, c), 1)
    tri_incl = jnp.where(ii >= jj, 1.0, 0.0).astype(BF16)
    ones_cc = jnp.ones((c, c), BF16)
    ones_cl = jnp.ones((c, HEAD_DIM), BF16)

    for h in range(HEADS):
        lanes = slice(h * HEAD_DIM, (h + 1) * HEAD_DIM)
        qk_lanes = slice(h * ML_QK, (h + 1) * ML_QK)
        qs = (q_ref[:, qk_lanes].astype(F32) * ML_QK ** -0.5).astype(BF16)
        kh = k_ref[:, qk_lanes]
        vh = v_ref[:, lanes]
        i_b = jnp.broadcast_to(zz[:, S_I + h:S_I + h + 1], (c, HEAD_DIM))
        lf_b = jnp.broadcast_to(logf_all[:, S_F + h:S_F + h + 1], (c, HEAD_DIM))
        bcol = _dot_sel_lhs(tri_incl, lf_b)
        rv = i_b - bcol
        rrow = _dot_sel_lhs(ones_cc, jnp.where(ii == jj, rv[:, :c], 0.0))
        log_intra = jnp.where(ii >= jj, bcol[:, :c] + rrow, -jnp.inf)
        m_prev = m_ref[h][0:1, :]
        log_inter = bcol + m_prev
        m_out = jnp.maximum(log_inter, jnp.max(log_intra, axis=1, keepdims=True))
        w_inter = jnp.exp(log_inter - m_out)
        s = jnp.exp(log_intra - m_out[:, :c]) * _dot_nt(qs, kh)
        c_mat = c_ref[h]
        n_b = n_ref[h]
        num = w_inter * _dot(qs, c_mat.astype(BF16)) + _dot(s.astype(BF16), vh)
        den = w_inter * _dot(qs, n_b.astype(BF16)) + jnp.sum(s, axis=1, keepdims=True)
        hid = num / jnp.maximum(jnp.abs(den), jnp.exp(-m_out))

        b_last = bcol[c - 1:c, :]
        log_kv = b_last - bcol + i_b
        m_new = jnp.maximum(b_last + m_prev, jnp.max(log_kv, axis=0, keepdims=True))
        w_kv = jnp.exp(log_kv - m_new)
        carry_decay = jnp.exp(b_last + m_prev - m_new)
        kw = (kh.astype(F32) * w_kv[:, :ML_QK]).astype(BF16)
        c_ref[h] = carry_decay * c_mat + _dot_tn(kw, vh)
        n_ref[h] = carry_decay * n_b + _dot_tn(kw, ones_cl)
        m_ref[h] = jnp.broadcast_to(m_new, (8, HEAD_DIM))

        hid = hid * lax.rsqrt(jnp.mean(hid * hid, axis=-1, keepdims=True) + NORM_EPS)
        hid = hid * ng_ref[:, lanes]
        o_ref[:, lanes] = (hid * _sigmoid(og_ref[:, lanes].astype(F32))).astype(o_ref.dtype)


def _mlstm(zm, zs, i_bias, f_bias, norm_g):
    t = zm.shape[0]
    c = ML_CHUNK
    qk_w = HEADS * ML_QK
    par = jnp.zeros((8, SMALL_COLS), F32)
    par = par.at[2, S_I:S_I + HEADS].set(i_bias).at[2, S_F:S_F + HEADS].set(f_bias)
    return pl.pallas_call(
        _ml_kernel,
        out_shape=jax.ShapeDtypeStruct((t, BRANCH_WIDTH), BF16),
        grid=(t // c,),
        in_specs=[pl.BlockSpec((c, qk_w), lambda n: (n, D_Q // qk_w)),
                  pl.BlockSpec((c, qk_w), lambda n: (n, D_K // qk_w)),
                  pl.BlockSpec((c, BRANCH_WIDTH), lambda n: (n, D_V // BRANCH_WIDTH)),
                  pl.BlockSpec((c, BRANCH_WIDTH), lambda n: (n, D_O // BRANCH_WIDTH)),
                  pl.BlockSpec((c, SMALL_COLS), lambda n: (n, 0)),
                  pl.BlockSpec((8, SMALL_COLS), lambda n: (0, 0)),
                  pl.BlockSpec((1, BRANCH_WIDTH), lambda n: (0, 0))],
        out_specs=pl.BlockSpec((c, BRANCH_WIDTH), lambda n: (n, 0)),
        scratch_shapes=[pltpu.VMEM((HEADS, ML_QK, HEAD_DIM), F32),
                        pltpu.VMEM((HEADS, ML_QK, HEAD_DIM), F32),
                        pltpu.VMEM((HEADS, 8, HEAD_DIM), F32)],
        compiler_params=_cparams(1),
        name="mlstm",
    )(zm, zm, zm, zm, zs, par, norm_g.reshape(1, -1))


def _merge_kernel(a_ref, b_ref, c_ref, d_ref, g0_ref, g1_ref, g2_ref, g3_ref, wb_ref, o_ref):
    acc = None
    for g, (br, gr) in enumerate(((a_ref, g0_ref), (b_ref, g1_ref), (c_ref, g2_ref), (d_ref, g3_ref))):
        term = _sigmoid(gr[...].astype(F32)) * _dot(br[...], wb_ref[g])
        acc = term if acc is None else acc + term
    o_ref[...] = acc.astype(o_ref.dtype)


def _merge(outs, zm, wb16, tm):
    t = zm.shape[0]
    branch = pl.BlockSpec((tm, BRANCH_WIDTH), lambda i: (i, 0))
    gate = lambda g: pl.BlockSpec((tm, D_MODEL), lambda i: (i, GATES // D_MODEL + g))
    return pl.pallas_call(
        _merge_kernel,
        out_shape=jax.ShapeDtypeStruct((t, D_MODEL), BF16),
        grid=(t // tm,),
        in_specs=[branch] * 4 + [gate(0), gate(1), gate(2), gate(3),
                                 pl.BlockSpec((N_BRANCHES, BRANCH_WIDTH, D_MODEL), lambda i: (0, 0, 0))],
        out_specs=pl.BlockSpec((tm, D_MODEL), lambda i: (i, 0)),
        compiler_params=_cparams(1),
        name="merge_branches",
    )(*outs, zm, zm, zm, zm, wb16)


def _proj_ln_kernel(a_ref, w_ref, x_ref, g_ref, b_ref, o_ref):
    y = ALPHA * x_ref[...] + _dot(a_ref[...], w_ref[...])
    o_ref[...] = _layer_norm(y, g_ref[...], b_ref[...])


def _proj_ln(a, w16, x, g, b, tm):
    t, k = a.shape
    return pl.pallas_call(
        _proj_ln_kernel,
        out_shape=jax.ShapeDtypeStruct((t, D_MODEL), F32),
        grid=(t // tm,),
        in_specs=[pl.BlockSpec((tm, k), lambda i: (i, 0)),
                  pl.BlockSpec((k, D_MODEL), lambda i: (0, 0)),
                  pl.BlockSpec((tm, D_MODEL), lambda i: (i, 0)),
                  pl.BlockSpec((1, D_MODEL), lambda i: (0, 0)),
                  pl.BlockSpec((1, D_MODEL), lambda i: (0, 0))],
        out_specs=pl.BlockSpec((tm, D_MODEL), lambda i: (i, 0)),
        compiler_params=_cparams(1),
        name="out_proj_layernorm",
    )(a, w16, x, g.reshape(1, -1), b.reshape(1, -1))


def _router_kernel(x_ref, wr_ref, bias_ref, idx_ref, w_ref, *, tm):
    logits = _dot_f32_nt(wr_ref[...], x_ref[...])
    scores = _sigmoid(logits)
    biased = scores + bias_ref[...]
    neg = -jnp.inf

    b3 = biased.reshape(N_GROUPS, GROUP_SIZE, tm)
    pos = lax.broadcasted_iota(jnp.int32, (N_GROUPS, GROUP_SIZE, tm), 1)
    m1 = jnp.max(b3, axis=1, keepdims=True)
    first = jnp.min(jnp.where(b3 == m1, pos, GROUP_SIZE), axis=1, keepdims=True)
    m2 = jnp.max(jnp.where(pos == first, neg, b3), axis=1, keepdims=True)
    gscore = m1 + m2

    gidx = lax.broadcasted_iota(jnp.int32, (N_GROUPS, 1, tm), 0)
    gsel = jnp.zeros((N_GROUPS, 1, tm), F32)
    for _ in range(TOPK_GROUPS):
        gm = jnp.max(gscore, axis=0, keepdims=True)
        gfirst = jnp.min(jnp.where(gscore == gm, gidx, N_GROUPS), axis=0, keepdims=True)
        hit = gidx == gfirst
        gsel = jnp.where(hit, 1.0, gsel)
        gscore = jnp.where(hit, neg, gscore)

    allowed = jnp.broadcast_to(gsel, (N_GROUPS, GROUP_SIZE, tm)) > 0.0
    masked = jnp.where(allowed, b3, neg).reshape(N_EXPERTS, tm)
    eidx = lax.broadcasted_iota(jnp.int32, (N_EXPERTS, tm), 0)
    idx_rows, sel_rows = [], []
    for _ in range(TOP_K):
        em = jnp.max(masked, axis=0, keepdims=True)
        efirst = jnp.min(jnp.where(masked == em, eidx, N_EXPERTS), axis=0, keepdims=True)
        hit = eidx == efirst
        idx_rows.append(efirst)
        sel_rows.append(jnp.sum(jnp.where(hit, scores, 0.0), axis=0, keepdims=True))
        masked = jnp.where(hit, neg, masked)
    total = sel_rows[0]
    for r in sel_rows[1:]:
        total = total + r
    zero_i = jnp.zeros((1, tm), jnp.int32)
    zero_f = jnp.zeros((1, tm), F32)
    idx_ref[...] = jnp.concatenate(idx_rows + [zero_i, zero_i], axis=0)
    w_ref[...] = jnp.concatenate([r / total * ROUTED_SCALE for r in sel_rows] + [zero_f, zero_f], axis=0)


def _dot_f32_nt(a, b):
    ah, al = _split2(a)
    bh, bl = _split2(b)
    return _dot_nt(ah, bh) + _dot_nt(ah, bl) + _dot_nt(al, bh)


def _router(x, w_router, router_bias, tm):
    t = x.shape[0]
    return pl.pallas_call(
        functools.partial(_router_kernel, tm=tm),
        out_shape=(jax.ShapeDtypeStruct((8, t), jnp.int32), jax.ShapeDtypeStruct((8, t), F32)),
        grid=(t // tm,),
        in_specs=[pl.BlockSpec((tm, D_MODEL), lambda i: (i, 0)),
                  pl.BlockSpec((N_EXPERTS, D_MODEL), lambda i: (0, 0)),
                  pl.BlockSpec((N_EXPERTS, 1), lambda i: (0, 0))],
        out_specs=(pl.BlockSpec((8, tm), lambda i: (0, i)), pl.BlockSpec((8, tm), lambda i: (0, i))),
        compiler_params=_cparams(1),
        name="router_topk",
    )(x, w_router.T, router_bias.reshape(N_EXPERTS, 1))


def _expert_kernel(n_ref, tile_ref, exp_ref, lo_ref, hi_ref, src_ref, src_next_ref, dst_ref,
                   x_hbm, wg_ref, wu_ref, wd_ref, y_hbm,
                   xbuf, ybuf, wg16, wu16, wd16, gsem, ssem, *, tm, n_tiles, n_max):
    j = pl.program_id(0)
    n_items = n_ref[0]
    tile = tile_ref[j]
    prev = jnp.maximum(j - 1, 0)
    first = (j == 0) | (tile_ref[prev] != tile)
    last = (j + 1 == n_items) | (tile_ref[jnp.minimum(j + 1, n_max - 1)] != tile)
    slot = lax.rem(tile, 2)

    def gather(idx_ref, s):
        def issue(r, carry):
            pltpu.make_async_copy(x_hbm.at[pl.ds(idx_ref[0, r], 1), :],
                                  xbuf.at[s, pl.ds(r, 1), :], gsem.at[s]).start()
            return carry
        lax.fori_loop(0, tm, issue, 0, unroll=8)

    def scatter(s):
        def issue(r, carry):
            pltpu.make_async_copy(ybuf.at[s, pl.ds(r, 1), :],
                                  y_hbm.at[pl.ds(dst_ref[0, r], 1), :], ssem.at[s]).start()
            return carry
        lax.fori_loop(0, tm, issue, 0, unroll=8)

    def wait_gather(s):
        pltpu.make_async_copy(x_hbm.at[pl.ds(0, tm), :], xbuf.at[s], gsem.at[s]).wait()

    def wait_scatter(s):
        pltpu.make_async_copy(ybuf.at[s], y_hbm.at[pl.ds(0, tm), :], ssem.at[s]).wait()

    @pl.when(j < n_items)
    def _():
        @pl.when(first)
        def _():
            @pl.when(j == 0)
            def _():
                gather(src_ref, 0)

            @pl.when(tile + 1 < n_tiles)
            def _():
                gather(src_next_ref, 1 - slot)

            wait_gather(slot)

        @pl.when((j == 0) | (exp_ref[prev] != exp_ref[j]))
        def _():
            wg16[...] = wg_ref[...].astype(BF16)
            wu16[...] = wu_ref[...].astype(BF16)
            wd16[...] = wd_ref[...].astype(BF16)

        x = xbuf[slot].astype(BF16)
        hidden = _silu(_dot(x, wg16[...])) * _dot(x, wu16[...])
        y = _dot(hidden.astype(BF16), wd16[...])

        @pl.when(first)
        def _():
            ybuf[slot] = y

        @pl.when(jnp.logical_not(first))
        def _():
            row = lax.broadcasted_iota(jnp.int32, (tm, 1), 0)
            mine = (row >= lo_ref[j]) & (row < hi_ref[j])
            ybuf[slot] = jnp.where(mine, y, ybuf[slot])

        @pl.when(last)
        def _():
            @pl.when(tile >= 1)
            def _():
                wait_scatter(1 - slot)

            scatter(slot)

            @pl.when(j + 1 == n_items)
            def _():
                wait_scatter(slot)


def _experts(x, sched, w_gate, w_up, w_down, layer, tm):
    n_items, item_tile, item_exp, item_lo, item_hi, row_src, row_dst = sched
    n_tiles = row_src.shape[0]
    n_max = item_tile.shape[0]
    idx_spec = lambda f: pl.BlockSpec((None, 1, tm), f, memory_space=pltpu.SMEM)
    tile_of = lambda j, n, tl, ex, lo, hi: tl[j]
    next_of = lambda j, n, tl, ex, lo, hi: jnp.minimum(tl[j] + 1, n_tiles - 1)
    e_of = lambda j, n, tl, ex, lo, hi: ex[j]
    return pl.pallas_call(
        functools.partial(_expert_kernel, tm=tm, n_tiles=n_tiles, n_max=n_max),
        out_shape=jax.ShapeDtypeStruct((n_tiles * tm, D_MODEL), F32),
        grid_spec=pltpu.PrefetchScalarGridSpec(
            num_scalar_prefetch=5,
            grid=(n_max,),
            in_specs=[idx_spec(lambda *a: (tile_of(*a), 0, 0)),
                      idx_spec(lambda *a: (next_of(*a), 0, 0)),
                      idx_spec(lambda *a: (tile_of(*a), 0, 0)),
                      pl.BlockSpec(memory_space=pl.ANY),
                      pl.BlockSpec((None, None, D_MODEL, EXPERT_DIM), lambda *a: (layer, e_of(*a), 0, 0)),
                      pl.BlockSpec((None, None, D_MODEL, EXPERT_DIM), lambda *a: (layer, e_of(*a), 0, 0)),
                      pl.BlockSpec((None, None, EXPERT_DIM, D_MODEL), lambda *a: (layer, e_of(*a), 0, 0))],
            out_specs=pl.BlockSpec(memory_space=pl.ANY),
            scratch_shapes=[pltpu.VMEM((2, tm, D_MODEL), F32),
                            pltpu.VMEM((2, tm, D_MODEL), F32),
                            pltpu.VMEM((D_MODEL, EXPERT_DIM), BF16),
                            pltpu.VMEM((D_MODEL, EXPERT_DIM), BF16),
                            pltpu.VMEM((EXPERT_DIM, D_MODEL), BF16),
                            pltpu.SemaphoreType.DMA((2,)),
                            pltpu.SemaphoreType.DMA((2,))]),
        compiler_params=_cparams(1),
        name="routed_experts",
    )(n_items, item_tile, item_exp, item_lo, item_hi, row_src, row_src, row_dst, x, w_gate, w_up, w_down)


def _expert_schedule(idx, t, tm):
    i32 = jnp.int32
    n_rows = t * TOP_K
    n_tiles = n_rows // tm
    n_max = n_tiles + N_EXPERTS - 1
    flat_e = idx.reshape(-1)
    onehot = (flat_e[:, None] == jnp.arange(N_EXPERTS, dtype=i32)[None, :]).astype(i32)
    seen = jnp.cumsum(onehot, axis=0)
    counts = seen[-1]
    end = jnp.cumsum(counts)
    start = end - counts
    pos = jnp.sum(onehot * (seen - 1 + start[None, :]), axis=1)
    order = jnp.zeros((n_rows,), i32).at[pos].set(jnp.arange(n_rows, dtype=i32))
    token = order // TOP_K
    row_src = token.reshape(n_tiles, 1, tm)
    row_dst = ((order % TOP_K) * t + token).reshape(n_tiles, 1, tm)
    first_tile = start // tm
    items_per_e = jnp.where(counts > 0, (end - 1) // tm - first_tile + 1, 0)
    item_end = jnp.cumsum(items_per_e)
    n_items = item_end[-1]
    j = jnp.arange(n_max, dtype=i32)
    e = jnp.minimum(jnp.sum((item_end[None, :] <= j[:, None]).astype(i32), axis=1), N_EXPERTS - 1)
    tile = first_tile[e] + j - (item_end[e] - items_per_e[e])
    lo = jnp.maximum(start[e], tile * tm) - tile * tm
    hi = jnp.minimum(end[e], (tile + 1) * tm) - tile * tm
    valid = j < n_items
    last = jnp.maximum(n_items - 1, 0)
    pick = lambda a: jnp.where(valid, a, a[last]).astype(i32)
    return (n_items.reshape(1).astype(i32), pick(tile), pick(e), pick(lo), pick(hi), row_src, row_dst)


def _combine_kernel(x_ref, w_ref, y0, y1, y2, y3, y4, y5, sg_ref, su_ref, sd_ref, g_ref, b_ref,
                    o_ref, o16_ref):
    x = x_ref[...]
    x16 = x.astype(BF16)
    hidden = _silu(_dot(x16, sg_ref[...])) * _dot(x16, su_ref[...])
    acc = _dot(hidden.astype(BF16), sd_ref[...])
    w = w_ref[...]
    for k, yr in enumerate((y0, y1, y2, y3, y4, y5)):
        acc = acc + yr[...] * w[:, k:k + 1]
    out = _layer_norm(ALPHA * x + acc, g_ref[...], b_ref[...])
    o_ref[...] = out
    o16_ref[...] = out.astype(BF16)


def _combine(x, w_tok, y_slots, sg16, su16, sd16, g, b, tt):
    t = x.shape[0]
    nb = t // tt
    tile = pl.BlockSpec((tt, D_MODEL), lambda i: (i, 0))
    plane = lambda k: pl.BlockSpec((tt, D_MODEL), lambda i: (k * nb + i, 0))
    vec = pl.BlockSpec((1, D_MODEL), lambda i: (0, 0))
    return pl.pallas_call(
        _combine_kernel,
        out_shape=(jax.ShapeDtypeStruct((t, D_MODEL), F32), jax.ShapeDtypeStruct((t, D_MODEL), BF16)),
        grid=(nb,),
        in_specs=[tile, pl.BlockSpec((tt, 8), lambda i: (i, 0))] + [plane(k) for k in range(TOP_K)] +
                 [pl.BlockSpec((D_MODEL, SHARED_DIM), lambda i: (0, 0)),
                  pl.BlockSpec((D_MODEL, SHARED_DIM), lambda i: (0, 0)),
                  pl.BlockSpec((SHARED_DIM, D_MODEL), lambda i: (0, 0)), vec, vec],
        out_specs=(tile, tile),
        compiler_params=_cparams(1),
        name="moe_combine_layernorm",
    )(x, w_tok, *([y_slots] * TOP_K), sg16, su16, sd16, g.reshape(1, -1), b.reshape(1, -1))


def _tile(t, pref):
    return min(t, pref)


def _repack_w_in(w_in):
    main = jnp.concatenate([w_in[:, :4608], w_in[:, 4616:6152], w_in[:, 6160:]], axis=1).astype(BF16)
    small = jnp.concatenate([w_in[:, 4608:4616], w_in[:, 6152:6160],
                             jnp.zeros((D_MODEL, SMALL_COLS - 16), F32)], axis=1).astype(BF16)
    return main, small


def _mixer(x, x16, p):
    t = x.shape[0]
    w_main, w_small = _repack_w_in(p["w_in"])
    zm = _matmul(x16, w_main, BF16, _tile(t, 512), 1024, "in_proj_main")
    zs = _matmul(x16, w_small, F32, _tile(t, 1024), SMALL_COLS, "in_proj_gates")
    out_a = _sb_attention(zm, 256, 256)
    out_b = _gmlp(zm, p["gm_norm_g"], p["gm_norm_b"], p["gm_w_s"], p["gm_b_s"], _tile(t, 512))
    out_c = _gdn(zm, zs, p["gdn_conv_w"], p["gdn_a_log"], p["gdn_dt_bias"], p["gdn_norm_g"], 2, 4)
    out_d = _mlstm(zm, zs, p["ml_i_bias"], p["ml_f_bias"], p["ml_norm_g"])
    merged = _merge((out_a, out_b, out_c, out_d), zm, p["w_branch"].astype(BF16), _tile(t, 256))
    return _proj_ln(merged, p["w_out"].astype(BF16), x, p["ln1_g"], p["ln1_b"], _tile(t, 256))


def _moe(x, p, tm):
    t = x.shape[0]
    idx_t, w_t = _router(x, p["w_router"], p["router_bias"], _tile(t, 512))
    idx = idx_t[:TOP_K].T
    w_tok = w_t.T
    y_slots = _experts(x, _expert_schedule(idx, t, tm), p["w_gate_all"], p["w_up_all"],
                       p["w_down_all"], p["layer"], tm)
    return _combine(x, w_tok, y_slots, p["ws_gate"].astype(BF16), p["ws_up"].astype(BF16),
                    p["ws_down"].astype(BF16), p["ln2_g"], p["ln2_b"], _tile(t, 256))


def _layer(x, x16, p, tm_expert):
    x1 = _mixer(x, x16, p)
    return _moe(x1, p, tm_expert)


_STACKED = ("w_gate", "w_up", "w_down")


def _layer_params(params, l):
    p = {k: v[l] for k, v in params.items() if k not in _STACKED}
    p.update({k + "_all": params[k] for k in _STACKED})
    p["layer"] = l
    return p


def kernel(x, w_in, gm_norm_g, gm_norm_b, gm_w_s, gm_b_s, gdn_conv_w, gdn_a_log, gdn_dt_bias, gdn_norm_g, ml_i_bias, ml_f_bias, ml_norm_g, w_branch, w_out, ln1_g, ln1_b, w_router, router_bias, w_gate, w_up, w_down, ws_gate, ws_up, ws_down, ln2_g, ln2_b):
    params = dict(w_in=w_in, gm_norm_g=gm_norm_g, gm_norm_b=gm_norm_b, gm_w_s=gm_w_s, gm_b_s=gm_b_s,
                  gdn_conv_w=gdn_conv_w, gdn_a_log=gdn_a_log, gdn_dt_bias=gdn_dt_bias,
                  gdn_norm_g=gdn_norm_g, ml_i_bias=ml_i_bias, ml_f_bias=ml_f_bias, ml_norm_g=ml_norm_g,
                  w_branch=w_branch, w_out=w_out, ln1_g=ln1_g, ln1_b=ln1_b, w_router=w_router,
                  router_bias=router_bias, w_gate=w_gate, w_up=w_up, w_down=w_down, ws_gate=ws_gate,
                  ws_up=ws_up, ws_down=ws_down, ln2_g=ln2_g, ln2_b=ln2_b)
    b, t, d = x.shape
    h = x.reshape(b * t, d)
    h16 = h.astype(BF16)
    for l in range(DEPTH):
        h, h16 = _layer(h, h16, _layer_params(params, l), 256)
    return h.reshape(b, t, d)
```

```python
import functools

import jax
import jax.numpy as jnp
from jax import lax
from jax.experimental import pallas as pl
from jax.experimental.pallas import tpu as pltpu

F32 = jnp.float32
BF16 = jnp.bfloat16

D_MODEL = 2048
N_BRANCHES = 4
BRANCH_WIDTH = 512
HEADS = 4
HEAD_DIM = 128
GM_CHUNK = 128
GDN_CHUNK = 64
GDN_CONV = 4
ML_CHUNK = 64
ML_QK = 64
N_EXPERTS = 64
N_GROUPS = 8
GROUP_SIZE = 8
TOPK_GROUPS = 4
TOP_K = 6
EXPERT_DIM = 512
SHARED_DIM = 512
ROUTED_SCALE = 2.5
DEPTH = 2
ALPHA = (2 * DEPTH) ** 0.25
NORM_EPS = 1e-5

A_Q, A_K, A_V = 0, 512, 1024
B_U, B_V = 1536, 2048
C_Q, C_K, C_V, C_G = 2560, 3072, 3584, 4096
D_Q, D_K, D_V, D_O = 4608, 4864, 5120, 5632
GATES = 6144
MAIN_COLS = GATES + N_BRANCHES * D_MODEL
S_BETA, S_A, S_I, S_F = 0, 4, 8, 12
SMALL_COLS = 128

VMEM_LIMIT = 56 * 1024 * 1024


def _cparams(n_axes):
    return pltpu.CompilerParams(
        dimension_semantics=("arbitrary",) * n_axes, vmem_limit_bytes=VMEM_LIMIT)


def _dot(a, b):
    return jnp.dot(a, b, preferred_element_type=F32)


def _dot_nt(a, b):
    return lax.dot_general(a, b, (((1,), (1,)), ((), ())), preferred_element_type=F32)


def _dot_tn(a, b):
    return lax.dot_general(a, b, (((0,), (0,)), ((), ())), preferred_element_type=F32)


def _split2(a):
    hi = a.astype(BF16)
    lo = (a - hi.astype(F32)).astype(BF16)
    return hi, lo


def _split3(a):
    hi = a.astype(BF16)
    r = a - hi.astype(F32)
    mid = r.astype(BF16)
    lo = (r - mid.astype(F32)).astype(BF16)
    return hi, mid, lo


def _dot_sel_lhs(sel, a):
    sel3 = jnp.concatenate([sel, sel, sel], axis=1)
    return _dot(sel3, jnp.concatenate(_split3(a), axis=0))


def _dot_sel_rhs(a, sel):
    hi, lo = _split2(a)
    return _dot(hi, sel) + _dot(lo, sel)


def _dot_f32(a, b):
    ah, al = _split2(a)
    bh, bl = _split2(b)
    return _dot(jnp.concatenate([ah, ah, al], axis=1), jnp.concatenate([bh, bl, bh], axis=0))


def _softplus(x):
    return jnp.maximum(x, 0.0) + jnp.log1p(jnp.exp(-jnp.abs(x)))


def _sigmoid(x):
    return 1.0 / (1.0 + jnp.exp(-x))


def _silu(x):
    return x * _sigmoid(x)


def _layer_norm(x, g, b):
    mu = jnp.mean(x, axis=-1, keepdims=True)
    xc = x - mu
    var = jnp.mean(xc * xc, axis=-1, keepdims=True)
    return xc * lax.rsqrt(var + NORM_EPS) * g + b


def _mm_kernel(x_ref, w_ref, o_ref):
    o_ref[...] = _dot(x_ref[...], w_ref[...]).astype(o_ref.dtype)


def _matmul(x, w, out_dtype, tm, tn, name):
    m, k = x.shape
    n = w.shape[1]
    return pl.pallas_call(
        _mm_kernel,
        out_shape=jax.ShapeDtypeStruct((m, n), out_dtype),
        grid=(n // tn, m // tm),
        in_specs=[pl.BlockSpec((tm, k), lambda j, i: (i, 0)),
                  pl.BlockSpec((k, tn), lambda j, i: (0, j))],
        out_specs=pl.BlockSpec((tm, tn), lambda j, i: (i, j)),
        compiler_params=_cparams(2),
        name=name,
    )(x, w)


def _sb_kernel(q_ref, k_ref, v_ref, o_ref, *, tq, tk, per_trip):
    i = pl.program_id(1)
    q = q_ref[...]
    scale = HEAD_DIM ** -0.5
    half = tk // 2
    row = i * tq + lax.broadcasted_iota(jnp.int32, (tq, tk), 0)
    col0 = lax.broadcasted_iota(jnp.int32, (tq, tk), 1)
    jj = lax.broadcasted_iota(jnp.int32, (tk, tk), 0) % half
    ss = lax.broadcasted_iota(jnp.int32, (tk, tk), 1)
    sel = jnp.where((jj > ss) | (ss >= half), 1.0, 0.0).astype(BF16)

    def blocks(starts, carry, on_diagonal):
        zs = [_dot_nt(q, k_ref[pl.ds(ks, tk), :]) * scale for ks in starts]
        sps = [jnp.maximum(z, 0.0) + jnp.log(1.0 + jnp.exp(-jnp.abs(z))) for z in zs]
        if on_diagonal:
            masks = [(col0 + ks) < row for ks in starts]
            sps = [jnp.where(m, sp, 0.0) for m, sp in zip(masks, sps)]
        pieces = [_split2(sp) for sp in sps]
        lates = [_dot(jnp.concatenate([hi[:, half:], lo[:, half:]], axis=1), sel) for hi, lo in pieces]
        earlies = [_dot(jnp.concatenate([hi[:, :half], lo[:, :half]], axis=1), sel) for hi, lo in pieces]
        base_late = [z[:, half:] - sp[:, half:] - late[:, :half] for z, sp, late in zip(zs, sps, lates)]
        base_early = [z[:, :half] - sp[:, :half] - early[:, :half] for z, sp, early in zip(zs, sps, earlies)]
        acc, run = carry
        for n, ks in enumerate(starts):
            run_mid = run + lates[n][:, half:]
            a = jnp.exp(jnp.concatenate([base_early[n] - run_mid, base_late[n] - run], axis=1))
            if on_diagonal:
                a = jnp.where(masks[n], a, 0.0)
            acc = acc + _dot(a.astype(BF16), v_ref[pl.ds(ks, tk), :])
            run = run_mid + earlies[n][:, half:]
        return acc, run

    carry = blocks([pl.multiple_of(i * tq, tq)],
                   (jnp.zeros((tq, HEAD_DIM), F32), jnp.zeros((tq, half), F32)), True)

    def run_groups(size, top, trips, carry):
        def group(step, carry):
            starts = [pl.multiple_of((top - 1 - size * step - s) * tk, tk) for s in range(size)]
            return blocks(starts, carry, False)
        return lax.fori_loop(0, trips, group, carry)

    carry = run_groups(per_trip, i, i // per_trip, carry)
    left = i % per_trip
    size = per_trip // 2
    while size >= 1:
        carry = run_groups(size, left, left // size, carry)
        left = left % size
        size //= 2
    acc, _ = carry
    o_ref[...] = acc.astype(o_ref.dtype)


def _sb_attention(zm, tq, tk, per_trip):
    t = zm.shape[0]
    assert tq == tk and tk % 256 == 0
    cq, ck, cv = A_Q // HEAD_DIM, A_K // HEAD_DIM, A_V // HEAD_DIM
    return pl.pallas_call(
        functools.partial(_sb_kernel, tq=tq, tk=tk, per_trip=per_trip),
        out_shape=jax.ShapeDtypeStruct((t, BRANCH_WIDTH), BF16),
        grid=(HEADS, t // tq),
        in_specs=[pl.BlockSpec((tq, HEAD_DIM), lambda h, i: (i, cq + h)),
                  pl.BlockSpec((t, HEAD_DIM), lambda h, i: (0, ck + h)),
                  pl.BlockSpec((t, HEAD_DIM), lambda h, i: (0, cv + h))],
        out_specs=pl.BlockSpec((tq, HEAD_DIM), lambda h, i: (i, h)),
        compiler_params=_cparams(2),
        name="sb_attention",
    )(zm, zm, zm)


def _gm_kernel(u_ref, v_ref, ng_ref, nb_ref, ws_ref, bs_ref, o_ref, *, n_chunk):
    u = jax.nn.gelu(u_ref[...].astype(F32))
    v = jax.nn.gelu(v_ref[...].astype(F32))
    v = _layer_norm(v, ng_ref[...], nb_ref[...])
    ii = lax.broadcasted_iota(jnp.int32, (GM_CHUNK, GM_CHUNK), 0)
    jj = lax.broadcasted_iota(jnp.int32, (GM_CHUNK, GM_CHUNK), 1)
    for g in range(HEADS):
        w = jnp.where(ii >= jj, ws_ref[g], 0.0).astype(BF16)
        bias = bs_ref[g]
        for c in range(n_chunk):
            rows = slice(c * GM_CHUNK, (c + 1) * GM_CHUNK)
            cols = slice(g * HEAD_DIM, (g + 1) * HEAD_DIM)
            mixed = _dot(w, v[rows, cols].astype(BF16)) + bias
            o_ref[rows, cols] = (u[rows, cols] * mixed).astype(o_ref.dtype)


def _gmlp(zm, norm_g, norm_b, w_s, b_s, tb):
    t = zm.shape[0]
    bias = jnp.broadcast_to(b_s[:, :, None], (HEADS, GM_CHUNK, HEAD_DIM))
    cu, cv = B_U // BRANCH_WIDTH, B_V // BRANCH_WIDTH
    return pl.pallas_call(
        functools.partial(_gm_kernel, n_chunk=tb // GM_CHUNK),
        out_shape=jax.ShapeDtypeStruct((t, BRANCH_WIDTH), BF16),
        grid=(t // tb,),
        in_specs=[pl.BlockSpec((tb, BRANCH_WIDTH), lambda i: (i, cu)),
                  pl.BlockSpec((tb, BRANCH_WIDTH), lambda i: (i, cv)),
                  pl.BlockSpec((1, BRANCH_WIDTH), lambda i: (0, 0)),
                  pl.BlockSpec((1, BRANCH_WIDTH), lambda i: (0, 0)),
                  pl.BlockSpec((HEADS, GM_CHUNK, GM_CHUNK), lambda i: (0, 0, 0)),
                  pl.BlockSpec((HEADS, GM_CHUNK, HEAD_DIM), lambda i: (0, 0, 0))],
        out_specs=pl.BlockSpec((tb, BRANCH_WIDTH), lambda i: (i, 0)),
        compiler_params=_cparams(1),
        name="gmlp",
    )(zm, zm, norm_g.reshape(1, -1), norm_b.reshape(1, -1), w_s, bias)


def _gdn_prep_kernel(q_ref, k_ref, v_ref, zs_ref, cw_ref, par_ref,
                     u_ref, w_ref, qe_ref, kd_ref, attn_ref, eg_ref, xs_ref, *, n_ch):
    c = GDN_CHUNK
    w3 = 3 * BRANCH_WIDTH
    rows_all = n_ch * c

    @pl.when(pl.program_id(0) == 0)
    def _():
        xs_ref[0:8, :] = jnp.zeros((8, w3), F32)

    xs_ref[8:8 + rows_all, 0:BRANCH_WIDTH] = q_ref[...].astype(F32)
    xs_ref[8:8 + rows_all, BRANCH_WIDTH:2 * BRANCH_WIDTH] = k_ref[...].astype(F32)
    xs_ref[8:8 + rows_all, 2 * BRANCH_WIDTH:w3] = v_ref[...].astype(F32)
    cw = cw_ref[...]
    y_all = xs_ref[5:5 + rows_all, :] * cw[0:1, :]
    for tap in range(1, GDN_CONV):
        y_all = y_all + xs_ref[5 + tap:5 + tap + rows_all, :] * cw[tap:tap + 1, :]
    xs_ref[0:8, :] = xs_ref[rows_all:rows_all + 8, :]
    y_all = _silu(y_all)

    zs_all = zs_ref[...]
    beta_full = _sigmoid(zs_all)
    g_full = -jnp.exp(par_ref[0:1, :]) * _softplus(zs_all + par_ref[1:2, :])

    ii = lax.broadcasted_iota(jnp.int32, (c, c), 0)
    jj = lax.broadcasted_iota(jnp.int32, (c, c), 1)
    tri_incl = jnp.where(ii >= jj, 1.0, 0.0).astype(BF16)
    ones_cc = jnp.ones((c, c), BF16)

    chains = []
    for ch, h in [(ch, h) for ch in range(n_ch) for h in range(HEADS)]:
        rows = slice(ch * c, (ch + 1) * c)
        y = y_all[rows]
        beta_all = beta_full[rows]
        g_all = g_full[rows]
        lanes = slice(h * HEAD_DIM, (h + 1) * HEAD_DIM)
        qh = y[:, h * HEAD_DIM:(h + 1) * HEAD_DIM]
        kh = y[:, BRANCH_WIDTH + h * HEAD_DIM:BRANCH_WIDTH + (h + 1) * HEAD_DIM]
        vh = y[:, 2 * BRANCH_WIDTH + h * HEAD_DIM:2 * BRANCH_WIDTH + (h + 1) * HEAD_DIM]
        qh = qh * lax.rsqrt(jnp.sum(qh * qh, axis=-1, keepdims=True) + 1e-6) * HEAD_DIM ** -0.5
        kh = kh * lax.rsqrt(jnp.sum(kh * kh, axis=-1, keepdims=True) + 1e-6)
        beta = beta_all[:, S_BETA + h:S_BETA + h + 1]
        g_b = jnp.broadcast_to(g_all[:, S_A + h:S_A + h + 1], (c, HEAD_DIM))
        chains.append(dict(ch=ch, h=h, rows=rows, lanes=lanes, qh=qh, kh=kh, vh=vh, beta=beta, g_b=g_b))

    for s in chains:
        s["gcol"] = _dot_sel_lhs(tri_incl, s["g_b"])
        s["grow"] = _dot_sel_lhs(ones_cc, jnp.where(ii <= jj, s["g_b"][:, :c], 0.0))
        s["kh16"] = s["kh"].astype(BF16)
        s["kb"] = s["kh"] * s["beta"]
        s["kk"] = _dot_nt(s["kb"].astype(BF16), s["kh16"])
        s["qk"] = _dot_nt(s["qh"].astype(BF16), s["kh16"])
    for s in chains:
        decay = jnp.where(ii >= jj, jnp.exp(s["gcol"][:, :c] - s["grow"]), 0.0)
        expg = jnp.exp(s["gcol"])
        g_last = s["gcol"][c - 1:c, :]
        rows, lanes = s["rows"], s["lanes"]
        qe_ref[rows, lanes] = (s["qh"] * expg).astype(BF16)
        kd_ref[rows, lanes] = (s["kh"] * jnp.exp(g_last - s["gcol"])).astype(BF16)
        attn_ref[s["h"], rows, :] = (s["qk"] * decay).astype(BF16)
        eg_ref[s["ch"], :, lanes] = jnp.broadcast_to(jnp.exp(g_last), (8, HEAD_DIM))
        s["p"] = -jnp.where(ii > jj, s["kk"] * decay, 0.0)
        s["x"] = jnp.concatenate([s["vh"] * s["beta"], s["kb"] * expg], axis=1)
    for r in range(6):
        for s in chains:
            s["x"] = s["x"] + _dot_f32(s["p"], s["x"])
            if r < 5:
                s["p"] = _dot_f32(s["p"], s["p"])
    for s in chains:
        u_ref[s["rows"], s["lanes"]] = s["x"][:, :HEAD_DIM]
        w_ref[s["rows"], s["lanes"]] = s["x"][:, HEAD_DIM:].astype(BF16)


def _gdn_scan_kernel(u_ref, w_ref, qe_ref, kd_ref, attn_ref, eg_ref, gate_ref, ng_ref, o_ref,
                     s_ref, *, n_ch):
    c = GDN_CHUNK

    @pl.when(pl.program_id(0) == 0)
    def _():
        s_ref[...] = jnp.zeros(s_ref.shape, F32)

    heads = range(HEADS)
    lanes = [slice(h * HEAD_DIM, (h + 1) * HEAD_DIM) for h in heads]
    state = [s_ref[h] for h in heads]
    for ch in range(n_ch):
        rows = slice(ch * c, (ch + 1) * c)
        s16 = [state[h].astype(BF16) for h in heads]
        v_new = [(u_ref[rows, lanes[h]] - _dot(w_ref[rows, lanes[h]], s16[h])).astype(BF16) for h in heads]
        state = [state[h] * eg_ref[ch, 0:1, lanes[h]] + _dot_tn(kd_ref[rows, lanes[h]], v_new[h])
                 for h in heads]
        out = [_dot(qe_ref[rows, lanes[h]], s16[h]) + _dot(attn_ref[h, rows, :], v_new[h]) for h in heads]
        for h in heads:
            o = out[h]
            o = o * lax.rsqrt(jnp.mean(o * o, axis=-1, keepdims=True) + NORM_EPS) * ng_ref[...]
            o_ref[rows, lanes[h]] = (o * _silu(gate_ref[rows, lanes[h]].astype(F32))).astype(o_ref.dtype)
    for h in heads:
        s_ref[h] = state[h]


def _gdn(zm, zs, conv_w, a_log, dt_bias, norm_g, prep_chunks, scan_chunks):
    t = zm.shape[0]
    c = GDN_CHUNK
    par = jnp.zeros((8, SMALL_COLS), F32)
    par = par.at[0, S_A:S_A + HEADS].set(a_log).at[1, S_A:S_A + HEADS].set(dt_bias)

    rp = prep_chunks * c
    blk = lambda col: pl.BlockSpec((rp, BRANCH_WIDTH), lambda n: (n, col // BRANCH_WIDTH))
    wide = pl.BlockSpec((rp, BRANCH_WIDTH), lambda n: (n, 0))
    u, w, qe, kd, attn, eg = pl.pallas_call(
        functools.partial(_gdn_prep_kernel, n_ch=prep_chunks),
        out_shape=(jax.ShapeDtypeStruct((t, BRANCH_WIDTH), F32),
                   jax.ShapeDtypeStruct((t, BRANCH_WIDTH), BF16),
                   jax.ShapeDtypeStruct((t, BRANCH_WIDTH), BF16),
                   jax.ShapeDtypeStruct((t, BRANCH_WIDTH), BF16),
                   jax.ShapeDtypeStruct((HEADS, t, c), BF16),
                   jax.ShapeDtypeStruct((t // c, 8, BRANCH_WIDTH), F32)),
        grid=(t // rp,),
        in_specs=[blk(C_Q), blk(C_K), blk(C_V),
                  pl.BlockSpec((rp, SMALL_COLS), lambda n: (n, 0)),
                  pl.BlockSpec((GDN_CONV, 3 * BRANCH_WIDTH), lambda n: (0, 0)),
                  pl.BlockSpec((8, SMALL_COLS), lambda n: (0, 0))],
        out_specs=(wide, wide, wide, wide,
                   pl.BlockSpec((HEADS, rp, c), lambda n: (0, n, 0)),
                   pl.BlockSpec((prep_chunks, 8, BRANCH_WIDTH), lambda n: (n, 0, 0))),
        scratch_shapes=[pltpu.VMEM((rp + 8, 3 * BRANCH_WIDTH), F32)],
        compiler_params=_cparams(1),
        name="gdn_chunk_prep",
    )(zm, zm, zm, zs, conv_w, par)

    rs = scan_chunks * c
    wide = pl.BlockSpec((rs, BRANCH_WIDTH), lambda n: (n, 0))
    return pl.pallas_call(
        functools.partial(_gdn_scan_kernel, n_ch=scan_chunks),
        out_shape=jax.ShapeDtypeStruct((t, BRANCH_WIDTH), BF16),
        grid=(t // rs,),
        in_specs=[wide, wide, wide, wide,
                  pl.BlockSpec((HEADS, rs, c), lambda n: (0, n, 0)),
                  pl.BlockSpec((scan_chunks, 8, BRANCH_WIDTH), lambda n: (n, 0, 0)),
                  pl.BlockSpec((rs, BRANCH_WIDTH), lambda n: (n, C_G // BRANCH_WIDTH)),
                  pl.BlockSpec((1, HEAD_DIM), lambda n: (0, 0))],
        out_specs=wide,
        scratch_shapes=[pltpu.VMEM((HEADS, HEAD_DIM, HEAD_DIM), F32)],
        compiler_params=_cparams(1),
        name="gdn_state_scan",
    )(u, w, qe, kd, attn, eg, zm, norm_g.reshape(1, -1))


def _ml_kernel(q_ref, k_ref, v_ref, og_ref, zs_ref, par_ref, ng_ref, o_ref,
               c_ref, n_ref, m_ref, *, n_ch):
    c = ML_CHUNK

    @pl.when(pl.program_id(0) == 0)
    def _():
        c_ref[...] = jnp.zeros(c_ref.shape, F32)
        n_ref[...] = jnp.zeros(n_ref.shape, F32)
        m_ref[...] = jnp.zeros(m_ref.shape, F32)

    zz_all = zs_ref[...] + par_ref[2:3, :]
    logf_all = -_softplus(-zz_all)

    ii = lax.broadcasted_iota(jnp.int32, (c, c), 0)
    jj = lax.broadcasted_iota(jnp.int32, (c, c), 1)
    tri_incl = jnp.where(ii >= jj, 1.0, 0.0).astype(BF16)
    ones_cc = jnp.ones((c, c), BF16)
    ones_cl = jnp.ones((c, HEAD_DIM), BF16)

    pairs = []
    for ch in range(n_ch):
        rows = slice(ch * c, (ch + 1) * c)
        for h in range(HEADS):
            lanes = slice(h * HEAD_DIM, (h + 1) * HEAD_DIM)
            qk_lanes = slice(h * ML_QK, (h + 1) * ML_QK)
            pairs.append(dict(
                ch=ch, h=h, rows=rows, lanes=lanes,
                qs=(q_ref[rows, qk_lanes].astype(F32) * ML_QK ** -0.5).astype(BF16),
                kh=k_ref[rows, qk_lanes], vh=v_ref[rows, lanes],
                i_b=jnp.broadcast_to(zz_all[rows, S_I + h:S_I + h + 1], (c, HEAD_DIM)),
                lf_b=jnp.broadcast_to(logf_all[rows, S_F + h:S_F + h + 1], (c, HEAD_DIM))))
    for s in pairs:
        s["bcol"] = _dot_sel_lhs(tri_incl, s["lf_b"])
        s["qk"] = _dot_nt(s["qs"], s["kh"])
    for s in pairs:
        rv = s["i_b"] - s["bcol"]
        s["rrow"] = _dot_sel_lhs(ones_cc, jnp.where(ii == jj, rv[:, :c], 0.0))
        s["b_last"] = s["bcol"][c - 1:c, :]
        s["log_kv"] = s["b_last"] - s["bcol"] + s["i_b"]
        s["kv_max"] = jnp.max(s["log_kv"], axis=0, keepdims=True)
    for s in pairs:
        s["log_intra"] = jnp.where(ii >= jj, s["bcol"][:, :c] + s["rrow"], -jnp.inf)
        s["intra_max"] = jnp.max(s["log_intra"], axis=1, keepdims=True)

    heads = range(HEADS)
    c_mat = [c_ref[h] for h in heads]
    n_b = [n_ref[h] for h in heads]
    m_prev = [m_ref[h][0:1, :] for h in heads]
    for ch in range(n_ch):
        cur = pairs[ch * HEADS:(ch + 1) * HEADS]
        m_out = [jnp.maximum(s["bcol"] + m_prev[s["h"]], s["intra_max"]) for s in cur]
        w_inter = [jnp.exp(s["bcol"] + m_prev[s["h"]] - m_out[s["h"]]) for s in cur]
        sc = [jnp.exp(s["log_intra"] - m_out[s["h"]][:, :c]) * s["qk"] for s in cur]
        num = [w_inter[h] * _dot(cur[h]["qs"], c_mat[h].astype(BF16)) + _dot(sc[h].astype(BF16), cur[h]["vh"])
               for h in heads]
        den = [w_inter[h] * _dot(cur[h]["qs"], n_b[h].astype(BF16)) + jnp.sum(sc[h], axis=1, keepdims=True)
               for h in heads]
        m_new = [jnp.maximum(s["b_last"] + m_prev[s["h"]], s["kv_max"]) for s in cur]
        kw = [(s["kh"].astype(F32) * jnp.exp(s["log_kv"] - m_new[s["h"]])[:, :ML_QK]).astype(BF16) for s in cur]
        decay = [jnp.exp(s["b_last"] + m_prev[s["h"]] - m_new[s["h"]]) for s in cur]
        c_mat = [decay[h] * c_mat[h] + _dot_tn(kw[h], cur[h]["vh"]) for h in heads]
        n_b = [decay[h] * n_b[h] + _dot_tn(kw[h], ones_cl) for h in heads]
        m_prev = m_new
        for h in heads:
            s = cur[h]
            hid = num[h] / jnp.maximum(jnp.abs(den[h]), jnp.exp(-m_out[h]))
            hid = hid * lax.rsqrt(jnp.mean(hid * hid, axis=-1, keepdims=True) + NORM_EPS)
            hid = hid * ng_ref[:, s["lanes"]]
            gate = _sigmoid(og_ref[s["rows"], s["lanes"]].astype(F32))
            o_ref[s["rows"], s["lanes"]] = (hid * gate).astype(o_ref.dtype)
    for h in heads:
        c_ref[h] = c_mat[h]
        n_ref[h] = n_b[h]
        m_ref[h] = jnp.broadcast_to(m_prev[h], (8, HEAD_DIM))


def _mlstm(zm, zs, i_bias, f_bias, norm_g, n_ch):
    t = zm.shape[0]
    c = ML_CHUNK * n_ch
    qk_w = HEADS * ML_QK
    par = jnp.zeros((8, SMALL_COLS), F32)
    par = par.at[2, S_I:S_I + HEADS].set(i_bias).at[2, S_F:S_F + HEADS].set(f_bias)
    return pl.pallas_call(
        functools.partial(_ml_kernel, n_ch=n_ch),
        out_shape=jax.ShapeDtypeStruct((t, BRANCH_WIDTH), BF16),
        grid=(t // c,),
        in_specs=[pl.BlockSpec((c, qk_w), lambda n: (n, D_Q // qk_w)),
                  pl.BlockSpec((c, qk_w), lambda n: (n, D_K // qk_w)),
                  pl.BlockSpec((c, BRANCH_WIDTH), lambda n: (n, D_V // BRANCH_WIDTH)),
                  pl.BlockSpec((c, BRANCH_WIDTH), lambda n: (n, D_O // BRANCH_WIDTH)),
                  pl.BlockSpec((c, SMALL_COLS), lambda n: (n, 0)),
                  pl.BlockSpec((8, SMALL_COLS), lambda n: (0, 0)),
                  pl.BlockSpec((1, BRANCH_WIDTH), lambda n: (0, 0))],
        out_specs=pl.BlockSpec((c, BRANCH_WIDTH), lambda n: (n, 0)),
        scratch_shapes=[pltpu.VMEM((HEADS, ML_QK, HEAD_DIM), F32),
                        pltpu.VMEM((HEADS, ML_QK, HEAD_DIM), F32),
                        pltpu.VMEM((HEADS, 8, HEAD_DIM), F32)],
        compiler_params=_cparams(1),
        name="mlstm",
    )(zm, zm, zm, zm, zs, par, norm_g.reshape(1, -1))


def _merge_kernel(a_ref, b_ref, c_ref, d_ref, g0_ref, g1_ref, g2_ref, g3_ref, wb_ref, o_ref):
    acc = None
    for g, (br, gr) in enumerate(((a_ref, g0_ref), (b_ref, g1_ref), (c_ref, g2_ref), (d_ref, g3_ref))):
        term = _sigmoid(gr[...].astype(F32)) * _dot(br[...], wb_ref[g])
        acc = term if acc is None else acc + term
    o_ref[...] = acc.astype(o_ref.dtype)


def _merge(outs, zm, wb16, tm):
    t = zm.shape[0]
    branch = pl.BlockSpec((tm, BRANCH_WIDTH), lambda i: (i, 0))
    gate = lambda g: pl.BlockSpec((tm, D_MODEL), lambda i: (i, GATES // D_MODEL + g))
    return pl.pallas_call(
        _merge_kernel,
        out_shape=jax.ShapeDtypeStruct((t, D_MODEL), BF16),
        grid=(t // tm,),
        in_specs=[branch] * 4 + [gate(0), gate(1), gate(2), gate(3),
                                 pl.BlockSpec((N_BRANCHES, BRANCH_WIDTH, D_MODEL), lambda i: (0, 0, 0))],
        out_specs=pl.BlockSpec((tm, D_MODEL), lambda i: (i, 0)),
        compiler_params=_cparams(1),
        name="merge_branches",
    )(*outs, zm, zm, zm, zm, wb16)


def _proj_ln_kernel(a_ref, w_ref, x_ref, g_ref, b_ref, o_ref):
    y = ALPHA * x_ref[...] + _dot(a_ref[...], w_ref[...])
    o_ref[...] = _layer_norm(y, g_ref[...], b_ref[...])


def _proj_ln(a, w16, x, g, b, tm):
    t, k = a.shape
    return pl.pallas_call(
        _proj_ln_kernel,
        out_shape=jax.ShapeDtypeStruct((t, D_MODEL), F32),
        grid=(t // tm,),
        in_specs=[pl.BlockSpec((tm, k), lambda i: (i, 0)),
                  pl.BlockSpec((k, D_MODEL), lambda i: (0, 0)),
                  pl.BlockSpec((tm, D_MODEL), lambda i: (i, 0)),
                  pl.BlockSpec((1, D_MODEL), lambda i: (0, 0)),
                  pl.BlockSpec((1, D_MODEL), lambda i: (0, 0))],
        out_specs=pl.BlockSpec((tm, D_MODEL), lambda i: (i, 0)),
        compiler_params=_cparams(1),
        name="out_proj_layernorm",
    )(a, w16, x, g.reshape(1, -1), b.reshape(1, -1))


def _router_kernel(x_ref, wr_ref, bias_ref, idx_ref, w_ref, *, tm):
    logits = _dot_f32_nt(wr_ref[...], x_ref[...])
    scores = _sigmoid(logits)
    biased = scores + bias_ref[...]
    neg = -jnp.inf

    b3 = biased.reshape(N_GROUPS, GROUP_SIZE, tm)
    pos = lax.broadcasted_iota(jnp.int32, (N_GROUPS, GROUP_SIZE, tm), 1)
    m1 = jnp.max(b3, axis=1, keepdims=True)
    first = jnp.min(jnp.where(b3 == m1, pos, GROUP_SIZE), axis=1, keepdims=True)
    m2 = jnp.max(jnp.where(pos == first, neg, b3), axis=1, keepdims=True)
    gscore = m1 + m2

    gidx = lax.broadcasted_iota(jnp.int32, (N_GROUPS, 1, tm), 0)
    gsel = jnp.zeros((N_GROUPS, 1, tm), F32)
    for _ in range(TOPK_GROUPS):
        gm = jnp.max(gscore, axis=0, keepdims=True)
        gfirst = jnp.min(jnp.where(gscore == gm, gidx, N_GROUPS), axis=0, keepdims=True)
        hit = gidx == gfirst
        gsel = jnp.where(hit, 1.0, gsel)
        gscore = jnp.where(hit, neg, gscore)

    allowed = jnp.broadcast_to(gsel, (N_GROUPS, GROUP_SIZE, tm)) > 0.0
    masked = jnp.where(allowed, b3, neg).reshape(N_EXPERTS, tm)
    eidx = lax.broadcasted_iota(jnp.int32, (N_EXPERTS, tm), 0)
    idx_rows, sel_rows = [], []
    for _ in range(TOP_K):
        em = jnp.max(masked, axis=0, keepdims=True)
        efirst = jnp.min(jnp.where(masked == em, eidx, N_EXPERTS), axis=0, keepdims=True)
        hit = eidx == efirst
        idx_rows.append(efirst)
        sel_rows.append(jnp.sum(jnp.where(hit, scores, 0.0), axis=0, keepdims=True))
        masked = jnp.where(hit, neg, masked)
    total = sel_rows[0]
    for r in sel_rows[1:]:
        total = total + r
    zero_i = jnp.zeros((1, tm), jnp.int32)
    zero_f = jnp.zeros((1, tm), F32)
    idx_ref[...] = jnp.concatenate(idx_rows + [zero_i, zero_i], axis=0)
    w_ref[...] = jnp.concatenate([r / total * ROUTED_SCALE for r in sel_rows] + [zero_f, zero_f], axis=0)


def _dot_f32_nt(a, b):
    ah, al = _split2(a)
    bh, bl = _split2(b)
    return _dot_nt(ah, bh) + _dot_nt(ah, bl) + _dot_nt(al, bh)


def _router(x, w_router, router_bias, tm):
    t = x.shape[0]
    return pl.pallas_call(
        functools.partial(_router_kernel, tm=tm),
        out_shape=(jax.ShapeDtypeStruct((8, t), jnp.int32), jax.ShapeDtypeStruct((8, t), F32)),
        grid=(t // tm,),
        in_specs=[pl.BlockSpec((tm, D_MODEL), lambda i: (i, 0)),
                  pl.BlockSpec((N_EXPERTS, D_MODEL), lambda i: (0, 0)),
                  pl.BlockSpec((N_EXPERTS, 1), lambda i: (0, 0))],
        out_specs=(pl.BlockSpec((8, tm), lambda i: (0, i)), pl.BlockSpec((8, tm), lambda i: (0, i))),
        compiler_params=_cparams(1),
        name="router_topk",
    )(x, w_router.T, router_bias.reshape(N_EXPERTS, 1))


def _dispatch_kernel(pos_ref, x_ref, xs_hbm, sem, *, td):
    def issue(r, carry):
        for k in range(TOP_K):
            pltpu.make_async_copy(x_ref.at[pl.ds(r, 1), :],
                                  xs_hbm.at[pl.ds(pos_ref[0, r * TOP_K + k], 1), :], sem).start()
        return carry
    lax.fori_loop(0, td, issue, 0, unroll=8)
    for _ in range(TOP_K):
        pltpu.make_async_copy(x_ref, xs_hbm.at[pl.ds(0, td), :], sem).wait()


def _dispatch(x, pos, td):
    t = x.shape[0]
    return pl.pallas_call(
        functools.partial(_dispatch_kernel, td=td),
        out_shape=jax.ShapeDtypeStruct((t * TOP_K, D_MODEL), F32),
        grid=(t // td,),
        in_specs=[pl.BlockSpec((None, 1, td * TOP_K), lambda i: (i, 0, 0), memory_space=pltpu.SMEM),
                  pl.BlockSpec((td, D_MODEL), lambda i: (i, 0))],
        out_specs=pl.BlockSpec(memory_space=pl.ANY),
        scratch_shapes=[pltpu.SemaphoreType.DMA],
        compiler_params=_cparams(1),
        name="moe_dispatch",
    )(pos.reshape(t // td, 1, td * TOP_K), x)


def _expert_kernel(n_ref, tile_ref, exp_ref, nxt_ref, par_ref, lo_ref, hi_ref,
                   x_ref, wg_hbm, wu_hbm, wd_hbm, y_ref,
                   wgbuf, wubuf, wdbuf, wg16, wu16, wd16, wsem, *, tm, layer):
    j = pl.program_id(0)
    prev = jnp.maximum(j - 1, 0)
    first = (j == 0) | (tile_ref[prev] != tile_ref[j])
    expert = exp_ref[j]
    slot = par_ref[j]

    def copies(ex, s):
        return [pltpu.make_async_copy(hbm.at[layer, ex], buf.at[s], wsem.at[s])
                for hbm, buf in ((wg_hbm, wgbuf), (wu_hbm, wubuf), (wd_hbm, wdbuf))]

    @pl.when(j < n_ref[0])
    def _():
        @pl.when((j == 0) | (exp_ref[prev] != expert))
        def _():
            @pl.when(j == 0)
            def _():
                for cp in copies(expert, slot):
                    cp.start()

            @pl.when(nxt_ref[j] != expert)
            def _():
                for cp in copies(nxt_ref[j], 1 - slot):
                    cp.start()

            for cp in copies(expert, slot):
                cp.wait()
            wg16[...] = wgbuf[slot].astype(BF16)
            wu16[...] = wubuf[slot].astype(BF16)
            wd16[...] = wdbuf[slot].astype(BF16)

        x = x_ref[...].astype(BF16)
        hidden = _silu(_dot(x, wg16[...])) * _dot(x, wu16[...])
        y = _dot(hidden.astype(BF16), wd16[...])

        @pl.when(first)
        def _():
            y_ref[...] = y

        @pl.when(jnp.logical_not(first))
        def _():
            row = lax.broadcasted_iota(jnp.int32, (tm, 1), 0)
            mine = (row >= lo_ref[j]) & (row < hi_ref[j])
            y_ref[...] = jnp.where(mine, y, y_ref[...])


def _experts(xs, sched, w_gate, w_up, w_down, layer, tm):
    n_max = sched[1].shape[0]
    tile_of = lambda j, n, tl, *rest: tl[j]
    hbm = pl.BlockSpec(memory_space=pl.ANY)
    up_shape, down_shape = (D_MODEL, EXPERT_DIM), (EXPERT_DIM, D_MODEL)
    return pl.pallas_call(
        functools.partial(_expert_kernel, tm=tm, layer=layer),
        out_shape=jax.ShapeDtypeStruct(xs.shape, F32),
        grid_spec=pltpu.PrefetchScalarGridSpec(
            num_scalar_prefetch=len(sched),
            grid=(n_max,),
            in_specs=[pl.BlockSpec((tm, D_MODEL), lambda *a: (tile_of(*a), 0)), hbm, hbm, hbm],
            out_specs=pl.BlockSpec((tm, D_MODEL), lambda *a: (tile_of(*a), 0)),
            scratch_shapes=[pltpu.VMEM((2,) + up_shape, F32), pltpu.VMEM((2,) + up_shape, F32),
                            pltpu.VMEM((2,) + down_shape, F32),
                            pltpu.VMEM(up_shape, BF16), pltpu.VMEM(up_shape, BF16),
                            pltpu.VMEM(down_shape, BF16),
                            pltpu.SemaphoreType.DMA((2,))]),
        compiler_params=_cparams(1),
        name="routed_experts",
    )(*sched, xs, w_gate, w_up, w_down)


def _expert_schedule(idx, t, tm):
    i32 = jnp.int32
    n_rows = t * TOP_K
    n_tiles = n_rows // tm
    n_max = n_tiles + N_EXPERTS - 1
    flat_e = idx.reshape(-1)
    onehot = (flat_e[:, None] == jnp.arange(N_EXPERTS, dtype=i32)[None, :]).astype(i32)
    seen = jnp.cumsum(onehot, axis=0)
    counts = seen[-1]
    end = jnp.cumsum(counts)
    start = end - counts
    pos = jnp.sum(onehot * (seen - 1 + start[None, :]), axis=1).astype(i32)
    first_tile = start // tm
    items_per_e = jnp.where(counts > 0, (end - 1) // tm - first_tile + 1, 0)
    item_end = jnp.cumsum(items_per_e)
    n_items = item_end[-1]
    j = jnp.arange(n_max, dtype=i32)
    e = jnp.minimum(jnp.sum((item_end[None, :] <= j[:, None]).astype(i32), axis=1), N_EXPERTS - 1)
    tile = first_tile[e] + j - (item_end[e] - items_per_e[e])
    lo = jnp.maximum(start[e], tile * tm) - tile * tm
    hi = jnp.minimum(end[e], (tile + 1) * tm) - tile * tm
    valid = j < n_items
    last = jnp.maximum(n_items - 1, 0)
    pick = lambda a: jnp.where(valid, a, a[last]).astype(i32)
    ids = jnp.arange(N_EXPERTS, dtype=i32)
    later = (ids[None, :] > ids[:, None]) & (counts[None, :] > 0)
    nxt = jnp.min(jnp.where(later, ids[None, :], N_EXPERTS), axis=1)
    nxt = jnp.where(nxt == N_EXPERTS, ids, nxt)
    parity = (jnp.cumsum((counts > 0).astype(i32)) - 1) % 2
    return pos, (n_items.reshape(1).astype(i32), pick(tile), pick(e), pick(nxt[e]), pick(parity[e]),
                 pick(lo), pick(hi))


def _combine_kernel(pos_ref, pos_next_ref, x_ref, w_ref, ys_hbm, sg_ref, su_ref, sd_ref, g_ref, b_ref,
                    o_ref, o16_ref, ybuf, sem, *, tt, n_steps):
    i = pl.program_id(0)
    slot = lax.rem(i, 2)

    def gather(idx_ref, s):
        def issue(r, carry):
            for k in range(TOP_K):
                pltpu.make_async_copy(ys_hbm.at[pl.ds(idx_ref[0, r * TOP_K + k], 1), :],
                                      ybuf.at[s, k, pl.ds(r, 1), :], sem.at[s]).start()
            return carry
        lax.fori_loop(0, tt, issue, 0, unroll=8)

    @pl.when(i == 0)
    def _():
        gather(pos_ref, 0)

    @pl.when(i + 1 < n_steps)
    def _():
        gather(pos_next_ref, 1 - slot)

    x = x_ref[...]
    x16 = x.astype(BF16)
    hidden = _silu(_dot(x16, sg_ref[...])) * _dot(x16, su_ref[...])
    acc = _dot(hidden.astype(BF16), sd_ref[...])
    for k in range(TOP_K):
        pltpu.make_async_copy(ys_hbm.at[pl.ds(0, tt), :], ybuf.at[slot, k], sem.at[slot]).wait()
    w = w_ref[...]
    for k in range(TOP_K):
        acc = acc + ybuf[slot, k] * w[:, k:k + 1]
    out = _layer_norm(ALPHA * x + acc, g_ref[...], b_ref[...])
    o_ref[...] = out
    o16_ref[...] = out.astype(BF16)


def _combine(x, w_tok, pos, ys, sg16, su16, sd16, g, b, tt):
    t = x.shape[0]
    nb = t // tt
    pos3 = pos.reshape(nb, 1, tt * TOP_K)
    tile = pl.BlockSpec((tt, D_MODEL), lambda i: (i, 0))
    vec = pl.BlockSpec((1, D_MODEL), lambda i: (0, 0))
    idx_spec = lambda f: pl.BlockSpec((None, 1, tt * TOP_K), f, memory_space=pltpu.SMEM)
    return pl.pallas_call(
        functools.partial(_combine_kernel, tt=tt, n_steps=nb),
        out_shape=(jax.ShapeDtypeStruct((t, D_MODEL), F32), jax.ShapeDtypeStruct((t, D_MODEL), BF16)),
        grid=(nb,),
        in_specs=[idx_spec(lambda i: (i, 0, 0)), idx_spec(lambda i: (jnp.minimum(i + 1, nb - 1), 0, 0)),
                  tile, pl.BlockSpec((tt, 8), lambda i: (i, 0)),
                  pl.BlockSpec(memory_space=pl.ANY),
                  pl.BlockSpec((D_MODEL, SHARED_DIM), lambda i: (0, 0)),
                  pl.BlockSpec((D_MODEL, SHARED_DIM), lambda i: (0, 0)),
                  pl.BlockSpec((SHARED_DIM, D_MODEL), lambda i: (0, 0)), vec, vec],
        out_specs=(tile, tile),
        scratch_shapes=[pltpu.VMEM((2, TOP_K, tt, D_MODEL), F32), pltpu.SemaphoreType.DMA((2,))],
        compiler_params=_cparams(1),
        name="moe_combine_layernorm",
    )(pos3, pos3, x, w_tok, ys, sg16, su16, sd16, g.reshape(1, -1), b.reshape(1, -1))


def _tile(t, pref):
    return min(t, pref)


def _repack_w_in(w_in):
    main = jnp.concatenate([w_in[:, :4608], w_in[:, 4616:6152], w_in[:, 6160:]], axis=1).astype(BF16)
    small = jnp.concatenate([w_in[:, 4608:4616], w_in[:, 6152:6160],
                             jnp.zeros((D_MODEL, SMALL_COLS - 16), F32)], axis=1).astype(BF16)
    return main, small


def _mixer(x, x16, p):
    t = x.shape[0]
    w_main, w_small = _repack_w_in(p["w_in"])
    zm = _matmul(x16, w_main, BF16, _tile(t, 512), 1024, "in_proj_main")
    zs = _matmul(x16, w_small, F32, _tile(t, 1024), SMALL_COLS, "in_proj_gates")
    out_a = _sb_attention(zm, 256, 256, 8)
    out_b = _gmlp(zm, p["gm_norm_g"], p["gm_norm_b"], p["gm_w_s"], p["gm_b_s"], _tile(t, 512))
    out_c = _gdn(zm, zs, p["gdn_conv_w"], p["gdn_a_log"], p["gdn_dt_bias"], p["gdn_norm_g"], 4, 4)
    out_d = _mlstm(zm, zs, p["ml_i_bias"], p["ml_f_bias"], p["ml_norm_g"], 2)
    merged = _merge((out_a, out_b, out_c, out_d), zm, p["w_branch"].astype(BF16), _tile(t, 256))
    return _proj_ln(merged, p["w_out"].astype(BF16), x, p["ln1_g"], p["ln1_b"], _tile(t, 256))


def _moe(x, p, tm):
    t = x.shape[0]
    idx_t, w_t = _router(x, p["w_router"], p["router_bias"], _tile(t, 512))
    idx = idx_t[:TOP_K].T
    w_tok = w_t.T
    pos, sched = _expert_schedule(idx, t, tm)
    xs = _dispatch(x, pos, _tile(t, 256))
    ys = _experts(xs, sched, p["w_gate_all"], p["w_up_all"], p["w_down_all"], p["layer"], tm)
    return _combine(x, w_tok, pos, ys, p["ws_gate"].astype(BF16), p["ws_up"].astype(BF16),
                    p["ws_down"].astype(BF16), p["ln2_g"], p["ln2_b"], _tile(t, 128))


def _layer(x, x16, p, tm_expert):
    x1 = _mixer(x, x16, p)
    return _moe(x1, p, tm_expert)


_STACKED = ("w_gate", "w_up", "w_down")


def _layer_params(params, l):
    p = {k: v[l] for k, v in params.items() if k not in _STACKED}
    p.update({k + "_all": params[k] for k in _STACKED})
    p["layer"] = l
    return p


def kernel(x, w_in, gm_norm_g, gm_norm_b, gm_w_s, gm_b_s, gdn_conv_w, gdn_a_log, gdn_dt_bias, gdn_norm_g, ml_i_bias, ml_f_bias, ml_norm_g, w_branch, w_out, ln1_g, ln1_b, w_router, router_bias, w_gate, w_up, w_down, ws_gate, ws_up, ws_down, ln2_g, ln2_b):
    params = dict(w_in=w_in, gm_norm_g=gm_norm_g, gm_norm_b=gm_norm_b, gm_w_s=gm_w_s, gm_b_s=gm_b_s,
                  gdn_conv_w=gdn_conv_w, gdn_a_log=gdn_a_log, gdn_dt_bias=gdn_dt_bias,
                  gdn_norm_g=gdn_norm_g, ml_i_bias=ml_i_bias, ml_f_bias=ml_f_bias, ml_norm_g=ml_norm_g,
                  w_branch=w_branch, w_out=w_out, ln1_g=ln1_g, ln1_b=ln1_b, w_router=w_router,
                  router_bias=router_bias, w_gate=w_gate, w_up=w_up, w_down=w_down, ws_gate=ws_gate,
                  ws_up=ws_up, ws_down=ws_down, ln2_g=ln2_g, ln2_b=ln2_b)
    b, t, d = x.shape
    h = x.reshape(b * t, d)
    h16 = h.astype(BF16)
    for l in range(DEPTH):
        h, h16 = _layer(h, h16, _layer_params(params, l), 256)
    return h.reshape(b, t, d)
```

```python
import functools

import jax
import jax.numpy as jnp
from jax import lax
from jax.experimental import pallas as pl
from jax.experimental.pallas import tpu as pltpu

F32 = jnp.float32
BF16 = jnp.bfloat16

D_MODEL = 2048
N_BRANCHES = 4
BRANCH_WIDTH = 512
HEADS = 4
HEAD_DIM = 128
GM_CHUNK = 128
GDN_CHUNK = 64
GDN_CONV = 4
ML_CHUNK = 64
ML_QK = 64
N_EXPERTS = 64
N_GROUPS = 8
GROUP_SIZE = 8
TOPK_GROUPS = 4
TOP_K = 6
EXPERT_DIM = 512
SHARED_DIM = 512
ROUTED_SCALE = 2.5
DEPTH = 2
ALPHA = (2 * DEPTH) ** 0.25
NORM_EPS = 1e-5
LOG2E = 1.4426950408889634

ABC_COLS = (0, 4608)
A_Q, A_K, A_V = 0, 512, 1024
B_U, B_V = 1536, 2048
C_Q, C_K, C_V, C_G = 2560, 3072, 3584, 4096
D_COLS = (4616, 6152)
D_Q, D_K, D_V, D_O = 0, 256, 512, 1024
GATE_COLS = (6160, 6160 + N_BRANCHES * D_MODEL)
SMALL_SRC = ((4608, 4616), (6152, 6160))
S_BETA, S_A, S_I, S_F = 0, 4, 8, 12
SMALL_COLS = 128

VMEM_LIMIT = 56 * 1024 * 1024


def _cparams(n_axes):
    return pltpu.CompilerParams(
        dimension_semantics=("arbitrary",) * n_axes, vmem_limit_bytes=VMEM_LIMIT)


def _dot(a, b):
    return jnp.dot(a, b, preferred_element_type=F32)


def _dot_nt(a, b):
    return lax.dot_general(a, b, (((1,), (1,)), ((), ())), preferred_element_type=F32)


def _dot_tn(a, b):
    return lax.dot_general(a, b, (((0,), (0,)), ((), ())), preferred_element_type=F32)


def _split2(a):
    hi = a.astype(BF16)
    lo = (a - hi.astype(F32)).astype(BF16)
    return hi, lo


def _split3(a):
    hi = a.astype(BF16)
    r = a - hi.astype(F32)
    mid = r.astype(BF16)
    lo = (r - mid.astype(F32)).astype(BF16)
    return hi, mid, lo


def _dot_sel_lhs(sel, a):
    sel3 = jnp.concatenate([sel, sel, sel], axis=1)
    return _dot(sel3, jnp.concatenate(_split3(a), axis=0))


def _dot_sel_rhs(a, sel):
    hi, lo = _split2(a)
    return _dot(hi, sel) + _dot(lo, sel)


def _dot_f32(a, b):
    ah, al = _split2(a)
    bh, bl = _split2(b)
    return _dot(jnp.concatenate([ah, ah, al], axis=1), jnp.concatenate([bh, bl, bh], axis=0))


def _softplus(x):
    return jnp.maximum(x, 0.0) + jnp.log1p(jnp.exp(-jnp.abs(x)))


def _sigmoid(x):
    return 1.0 / (1.0 + jnp.exp(-x))


def _silu(x):
    return x * _sigmoid(x)


def _layer_norm(x, g, b):
    mu = jnp.mean(x, axis=-1, keepdims=True)
    xc = x - mu
    var = jnp.mean(xc * xc, axis=-1, keepdims=True)
    return xc * lax.rsqrt(var + NORM_EPS) * g + b


def _mm_kernel(x_ref, w_ref, o_ref):
    o_ref[...] = _dot(x_ref[...], w_ref[...]).astype(o_ref.dtype)


def _matmul(x, w, out_dtype, tm, tn, name):
    m, k = x.shape
    n = w.shape[1]
    return pl.pallas_call(
        _mm_kernel,
        out_shape=jax.ShapeDtypeStruct((m, n), out_dtype),
        grid=(n // tn, m // tm),
        in_specs=[pl.BlockSpec((tm, k), lambda j, i: (i, 0)),
                  pl.BlockSpec((k, tn), lambda j, i: (0, j))],
        out_specs=pl.BlockSpec((tm, tn), lambda j, i: (i, j)),
        compiler_params=_cparams(2),
        name=name,
    )(x, w)


def _sb_kernel(q_ref, k_ref, v_ref, o_ref, *, tq, tk, per_trip):
    i = pl.program_id(1)
    q = (q_ref[...].astype(F32) * (HEAD_DIM ** -0.5 * LOG2E)).astype(BF16)
    half = tk // 2
    row = i * tq + lax.broadcasted_iota(jnp.int32, (tq, tk), 0)
    col0 = lax.broadcasted_iota(jnp.int32, (tq, tk), 1)
    jj = lax.broadcasted_iota(jnp.int32, (tk, tk), 0) % half
    ss = lax.broadcasted_iota(jnp.int32, (tk, tk), 1)
    sel = jnp.where((jj >= ss) | (ss >= half), 1.0, 0.0).astype(BF16)

    def blocks(starts, carry, on_diagonal):
        zs = [_dot_nt(q, k_ref[pl.ds(ks, tk), :]) for ks in starts]
        sps = [jnp.maximum(z, 0.0) + jnp.log(1.0 + jnp.exp2(-jnp.abs(z))) * LOG2E for z in zs]
        if on_diagonal:
            masks = [(col0 + ks) < row for ks in starts]
            sps = [jnp.where(m, sp, 0.0) for m, sp in zip(masks, sps)]
        pieces = [_split2(sp) for sp in sps]
        lates = [_dot(jnp.concatenate([hi[:, half:], lo[:, half:]], axis=1), sel) for hi, lo in pieces]
        earlies = [_dot(jnp.concatenate([hi[:, :half], lo[:, :half]], axis=1), sel) for hi, lo in pieces]
        base_late = [z[:, half:] - late[:, :half] for z, late in zip(zs, lates)]
        base_early = [z[:, :half] - early[:, :half] for z, early in zip(zs, earlies)]
        acc, run = carry
        for n, ks in enumerate(starts):
            run_mid = run + lates[n][:, half:]
            a = jnp.exp2(jnp.concatenate([base_early[n] - run_mid, base_late[n] - run], axis=1))
            if on_diagonal:
                a = jnp.where(masks[n], a, 0.0)
            acc = acc + _dot(a.astype(BF16), v_ref[pl.ds(ks, tk), :])
            run = run_mid + earlies[n][:, half:]
        return acc, run

    carry = blocks([pl.multiple_of(i * tq, tq)],
                   (jnp.zeros((tq, HEAD_DIM), F32), jnp.zeros((tq, half), F32)), True)

    def run_groups(size, top, trips, carry):
        def group(step, carry):
            starts = [pl.multiple_of((top - 1 - size * step - s) * tk, tk) for s in range(size)]
            return blocks(starts, carry, False)
        return lax.fori_loop(0, trips, group, carry)

    carry = run_groups(per_trip, i, i // per_trip, carry)
    left = i % per_trip
    size = per_trip // 2
    while size >= 1:
        carry = run_groups(size, left, left // size, carry)
        left = left % size
        size //= 2
    acc, _ = carry
    o_ref[...] = acc.astype(o_ref.dtype)


def _sb_attention(zm, tq, tk, per_trip):
    t = zm.shape[0]
    assert tq == tk and tk % 256 == 0
    cq, ck, cv = A_Q // HEAD_DIM, A_K // HEAD_DIM, A_V // HEAD_DIM
    return pl.pallas_call(
        functools.partial(_sb_kernel, tq=tq, tk=tk, per_trip=per_trip),
        out_shape=jax.ShapeDtypeStruct((t, BRANCH_WIDTH), BF16),
        grid=(HEADS, t // tq),
        in_specs=[pl.BlockSpec((tq, HEAD_DIM), lambda h, i: (i, cq + h)),
                  pl.BlockSpec((t, HEAD_DIM), lambda h, i: (0, ck + h)),
                  pl.BlockSpec((t, HEAD_DIM), lambda h, i: (0, cv + h))],
        out_specs=pl.BlockSpec((tq, HEAD_DIM), lambda h, i: (i, h)),
        compiler_params=_cparams(2),
        name="sb_attention",
    )(zm, zm, zm)


def _gm_kernel(u_ref, v_ref, ng_ref, nb_ref, ws_ref, bs_ref, o_ref, *, n_chunk):
    u = jax.nn.gelu(u_ref[...].astype(F32))
    v = jax.nn.gelu(v_ref[...].astype(F32))
    v = _layer_norm(v, ng_ref[...], nb_ref[...])
    ii = lax.broadcasted_iota(jnp.int32, (GM_CHUNK, GM_CHUNK), 0)
    jj = lax.broadcasted_iota(jnp.int32, (GM_CHUNK, GM_CHUNK), 1)
    for g in range(HEADS):
        w = jnp.where(ii >= jj, ws_ref[g], 0.0).astype(BF16)
        bias = bs_ref[g]
        for c in range(n_chunk):
            rows = slice(c * GM_CHUNK, (c + 1) * GM_CHUNK)
            cols = slice(g * HEAD_DIM, (g + 1) * HEAD_DIM)
            mixed = _dot(w, v[rows, cols].astype(BF16)) + bias
            o_ref[rows, cols] = (u[rows, cols] * mixed).astype(o_ref.dtype)


def _gmlp(zm, norm_g, norm_b, w_s, b_s, tb):
    t = zm.shape[0]
    bias = jnp.broadcast_to(b_s[:, :, None], (HEADS, GM_CHUNK, HEAD_DIM))
    cu, cv = B_U // BRANCH_WIDTH, B_V // BRANCH_WIDTH
    return pl.pallas_call(
        functools.partial(_gm_kernel, n_chunk=tb // GM_CHUNK),
        out_shape=jax.ShapeDtypeStruct((t, BRANCH_WIDTH), BF16),
        grid=(t // tb,),
        in_specs=[pl.BlockSpec((tb, BRANCH_WIDTH), lambda i: (i, cu)),
                  pl.BlockSpec((tb, BRANCH_WIDTH), lambda i: (i, cv)),
                  pl.BlockSpec((1, BRANCH_WIDTH), lambda i: (0, 0)),
                  pl.BlockSpec((1, BRANCH_WIDTH), lambda i: (0, 0)),
                  pl.BlockSpec((HEADS, GM_CHUNK, GM_CHUNK), lambda i: (0, 0, 0)),
                  pl.BlockSpec((HEADS, GM_CHUNK, HEAD_DIM), lambda i: (0, 0, 0))],
        out_specs=pl.BlockSpec((tb, BRANCH_WIDTH), lambda i: (i, 0)),
        compiler_params=_cparams(1),
        name="gmlp",
    )(zm, zm, norm_g.reshape(1, -1), norm_b.reshape(1, -1), w_s, bias)


def _gdn_prep_kernel(q_ref, k_ref, v_ref, zs_ref, cw_ref, par_ref,
                     u_ref, w_ref, qe_ref, kd_ref, attn_ref, eg_ref, xs_ref, *, n_ch):
    c = GDN_CHUNK
    w3 = 3 * BRANCH_WIDTH
    rows_all = n_ch * c

    @pl.when(pl.program_id(0) == 0)
    def _():
        xs_ref[0:8, :] = jnp.zeros((8, w3), F32)

    xs_ref[8:8 + rows_all, 0:BRANCH_WIDTH] = q_ref[...].astype(F32)
    xs_ref[8:8 + rows_all, BRANCH_WIDTH:2 * BRANCH_WIDTH] = k_ref[...].astype(F32)
    xs_ref[8:8 + rows_all, 2 * BRANCH_WIDTH:w3] = v_ref[...].astype(F32)
    cw = cw_ref[...]
    y_all = xs_ref[5:5 + rows_all, :] * cw[0:1, :]
    for tap in range(1, GDN_CONV):
        y_all = y_all + xs_ref[5 + tap:5 + tap + rows_all, :] * cw[tap:tap + 1, :]
    xs_ref[0:8, :] = xs_ref[rows_all:rows_all + 8, :]
    y_all = _silu(y_all)

    zs_all = zs_ref[...]
    beta_full = _sigmoid(zs_all)
    g_full = -jnp.exp(par_ref[0:1, :]) * _softplus(zs_all + par_ref[1:2, :])

    ii = lax.broadcasted_iota(jnp.int32, (c, c), 0)
    jj = lax.broadcasted_iota(jnp.int32, (c, c), 1)
    tri_incl = jnp.where(ii >= jj, 1.0, 0.0).astype(BF16)
    ones_cc = jnp.ones((c, c), BF16)

    chains = []
    for ch, h in [(ch, h) for ch in range(n_ch) for h in range(HEADS)]:
        rows = slice(ch * c, (ch + 1) * c)
        y = y_all[rows]
        beta_all = beta_full[rows]
        g_all = g_full[rows]
        lanes = slice(h * HEAD_DIM, (h + 1) * HEAD_DIM)
        qh = y[:, h * HEAD_DIM:(h + 1) * HEAD_DIM]
        kh = y[:, BRANCH_WIDTH + h * HEAD_DIM:BRANCH_WIDTH + (h + 1) * HEAD_DIM]
        vh = y[:, 2 * BRANCH_WIDTH + h * HEAD_DIM:2 * BRANCH_WIDTH + (h + 1) * HEAD_DIM]
        qh = qh * lax.rsqrt(jnp.sum(qh * qh, axis=-1, keepdims=True) + 1e-6) * HEAD_DIM ** -0.5
        kh = kh * lax.rsqrt(jnp.sum(kh * kh, axis=-1, keepdims=True) + 1e-6)
        beta = beta_all[:, S_BETA + h:S_BETA + h + 1]
        g_b = jnp.broadcast_to(g_all[:, S_A + h:S_A + h + 1], (c, HEAD_DIM))
        chains.append(dict(ch=ch, h=h, rows=rows, lanes=lanes, qh=qh, kh=kh, vh=vh, beta=beta, g_b=g_b))

    for s in chains:
        s["gcol"] = _dot_sel_lhs(tri_incl, s["g_b"])
        s["grow"] = _dot_sel_lhs(ones_cc, jnp.where(ii <= jj, s["g_b"][:, :c], 0.0))
        s["kh16"] = s["kh"].astype(BF16)
        s["kb"] = s["kh"] * s["beta"]
        s["kk"] = _dot_nt(s["kb"].astype(BF16), s["kh16"])
        s["qk"] = _dot_nt(s["qh"].astype(BF16), s["kh16"])
    for s in chains:
        decay = jnp.where(ii >= jj, jnp.exp(s["gcol"][:, :c] - s["grow"]), 0.0)
        expg = jnp.exp(s["gcol"])
        g_last = s["gcol"][c - 1:c, :]
        rows, lanes = s["rows"], s["lanes"]
        qe_ref[rows, lanes] = (s["qh"] * expg).astype(BF16)
        kd_ref[rows, lanes] = (s["kh"] * jnp.exp(g_last - s["gcol"])).astype(BF16)
        attn_ref[s["h"], rows, :] = (s["qk"] * decay).astype(BF16)
        eg_ref[s["ch"], :, lanes] = jnp.broadcast_to(jnp.exp(g_last), (8, HEAD_DIM))
        s["p"] = -jnp.where(ii > jj, s["kk"] * decay, 0.0)
        s["x"] = jnp.concatenate([s["vh"] * s["beta"], s["kb"] * expg], axis=1)
    for r in range(6):
        for s in chains:
            s["x"] = s["x"] + _dot_f32(s["p"], s["x"])
            if r < 5:
                s["p"] = _dot_f32(s["p"], s["p"])
    for s in chains:
        u_ref[s["rows"], s["lanes"]] = s["x"][:, :HEAD_DIM]
        w_ref[s["rows"], s["lanes"]] = s["x"][:, HEAD_DIM:].astype(BF16)


def _gdn_scan_kernel(u_ref, w_ref, qe_ref, kd_ref, attn_ref, eg_ref, gate_ref, ng_ref, o_ref,
                     s_ref, *, n_ch):
    c = GDN_CHUNK

    @pl.when(pl.program_id(0) == 0)
    def _():
        s_ref[...] = jnp.zeros(s_ref.shape, F32)

    heads = range(HEADS)
    lanes = [slice(h * HEAD_DIM, (h + 1) * HEAD_DIM) for h in heads]
    state = [s_ref[h] for h in heads]
    for ch in range(n_ch):
        rows = slice(ch * c, (ch + 1) * c)
        s16 = [state[h].astype(BF16) for h in heads]
        v_new = [(u_ref[rows, lanes[h]] - _dot(w_ref[rows, lanes[h]], s16[h])).astype(BF16) for h in heads]
        state = [state[h] * eg_ref[ch, 0:1, lanes[h]] + _dot_tn(kd_ref[rows, lanes[h]], v_new[h])
                 for h in heads]
        out = [_dot(qe_ref[rows, lanes[h]], s16[h]) + _dot(attn_ref[h, rows, :], v_new[h]) for h in heads]
        for h in heads:
            o = out[h]
            o = o * lax.rsqrt(jnp.mean(o * o, axis=-1, keepdims=True) + NORM_EPS) * ng_ref[...]
            o_ref[rows, lanes[h]] = (o * _silu(gate_ref[rows, lanes[h]].astype(F32))).astype(o_ref.dtype)
    for h in heads:
        s_ref[h] = state[h]


def _gdn(zm, zs, conv_w, a_log, dt_bias, norm_g, prep_chunks, scan_chunks):
    t = zm.shape[0]
    c = GDN_CHUNK
    par = jnp.zeros((8, SMALL_COLS), F32)
    par = par.at[0, S_A:S_A + HEADS].set(a_log).at[1, S_A:S_A + HEADS].set(dt_bias)

    rp = prep_chunks * c
    blk = lambda col: pl.BlockSpec((rp, BRANCH_WIDTH), lambda n: (n, col // BRANCH_WIDTH))
    wide = pl.BlockSpec((rp, BRANCH_WIDTH), lambda n: (n, 0))
    u, w, qe, kd, attn, eg = pl.pallas_call(
        functools.partial(_gdn_prep_kernel, n_ch=prep_chunks),
        out_shape=(jax.ShapeDtypeStruct((t, BRANCH_WIDTH), F32),
                   jax.ShapeDtypeStruct((t, BRANCH_WIDTH), BF16),
                   jax.ShapeDtypeStruct((t, BRANCH_WIDTH), BF16),
                   jax.ShapeDtypeStruct((t, BRANCH_WIDTH), BF16),
                   jax.ShapeDtypeStruct((HEADS, t, c), BF16),
                   jax.ShapeDtypeStruct((t // c, 8, BRANCH_WIDTH), F32)),
        grid=(t // rp,),
        in_specs=[blk(C_Q), blk(C_K), blk(C_V),
                  pl.BlockSpec((rp, SMALL_COLS), lambda n: (n, 0)),
                  pl.BlockSpec((GDN_CONV, 3 * BRANCH_WIDTH), lambda n: (0, 0)),
                  pl.BlockSpec((8, SMALL_COLS), lambda n: (0, 0))],
        out_specs=(wide, wide, wide, wide,
                   pl.BlockSpec((HEADS, rp, c), lambda n: (0, n, 0)),
                   pl.BlockSpec((prep_chunks, 8, BRANCH_WIDTH), lambda n: (n, 0, 0))),
        scratch_shapes=[pltpu.VMEM((rp + 8, 3 * BRANCH_WIDTH), F32)],
        compiler_params=_cparams(1),
        name="gdn_chunk_prep",
    )(zm, zm, zm, zs, conv_w, par)

    rs = scan_chunks * c
    wide = pl.BlockSpec((rs, BRANCH_WIDTH), lambda n: (n, 0))
    return pl.pallas_call(
        functools.partial(_gdn_scan_kernel, n_ch=scan_chunks),
        out_shape=jax.ShapeDtypeStruct((t, BRANCH_WIDTH), BF16),
        grid=(t // rs,),
        in_specs=[wide, wide, wide, wide,
                  pl.BlockSpec((HEADS, rs, c), lambda n: (0, n, 0)),
                  pl.BlockSpec((scan_chunks, 8, BRANCH_WIDTH), lambda n: (n, 0, 0)),
                  pl.BlockSpec((rs, BRANCH_WIDTH), lambda n: (n, C_G // BRANCH_WIDTH)),
                  pl.BlockSpec((1, HEAD_DIM), lambda n: (0, 0))],
        out_specs=wide,
        scratch_shapes=[pltpu.VMEM((HEADS, HEAD_DIM, HEAD_DIM), F32)],
        compiler_params=_cparams(1),
        name="gdn_state_scan",
    )(u, w, qe, kd, attn, eg, zm, norm_g.reshape(1, -1))


def _ml_kernel(q_ref, k_ref, v_ref, og_ref, zs_ref, par_ref, ng_ref, o_ref,
               c_ref, n_ref, m_ref, *, n_ch):
    c = ML_CHUNK

    @pl.when(pl.program_id(0) == 0)
    def _():
        c_ref[...] = jnp.zeros(c_ref.shape, F32)
        n_ref[...] = jnp.zeros(n_ref.shape, F32)
        m_ref[...] = jnp.zeros(m_ref.shape, F32)

    zz_all = zs_ref[...] + par_ref[2:3, :]
    logf_all = -_softplus(-zz_all)

    ii = lax.broadcasted_iota(jnp.int32, (c, c), 0)
    jj = lax.broadcasted_iota(jnp.int32, (c, c), 1)
    tri_incl = jnp.where(ii >= jj, 1.0, 0.0).astype(BF16)
    ones_cc = jnp.ones((c, c), BF16)
    ones_cl = jnp.ones((c, HEAD_DIM), BF16)

    pairs = []
    for ch in range(n_ch):
        rows = slice(ch * c, (ch + 1) * c)
        for h in range(HEADS):
            lanes = slice(h * HEAD_DIM, (h + 1) * HEAD_DIM)
            qk_lanes = slice(h * ML_QK, (h + 1) * ML_QK)
            pairs.append(dict(
                ch=ch, h=h, rows=rows, lanes=lanes,
                qs=(q_ref[rows, qk_lanes].astype(F32) * ML_QK ** -0.5).astype(BF16),
                kh=k_ref[rows, qk_lanes], vh=v_ref[rows, lanes],
                i_b=jnp.broadcast_to(zz_all[rows, S_I + h:S_I + h + 1], (c, HEAD_DIM)),
                lf_b=jnp.broadcast_to(logf_all[rows, S_F + h:S_F + h + 1], (c, HEAD_DIM))))
    for s in pairs:
        s["bcol"] = _dot_sel_lhs(tri_incl, s["lf_b"])
        s["qk"] = _dot_nt(s["qs"], s["kh"])
    for s in pairs:
        rv = s["i_b"] - s["bcol"]
        s["rrow"] = _dot_sel_lhs(ones_cc, jnp.where(ii == jj, rv[:, :c], 0.0))
        s["b_last"] = s["bcol"][c - 1:c, :]
        s["log_kv"] = s["b_last"] - s["bcol"] + s["i_b"]
        s["kv_max"] = jnp.max(s["log_kv"], axis=0, keepdims=True)
    for s in pairs:
        s["log_intra"] = jnp.where(ii >= jj, s["bcol"][:, :c] + s["rrow"], -jnp.inf)
        s["intra_max"] = jnp.max(s["log_intra"], axis=1, keepdims=True)

    heads = range(HEADS)
    c_mat = [c_ref[h] for h in heads]
    n_b = [n_ref[h] for h in heads]
    m_prev = [m_ref[h][0:1, :] for h in heads]
    for ch in range(n_ch):
        cur = pairs[ch * HEADS:(ch + 1) * HEADS]
        m_out = [jnp.maximum(s["bcol"] + m_prev[s["h"]], s["intra_max"]) for s in cur]
        w_inter = [jnp.exp(s["bcol"] + m_prev[s["h"]] - m_out[s["h"]]) for s in cur]
        sc = [jnp.exp(s["log_intra"] - m_out[s["h"]][:, :c]) * s["qk"] for s in cur]
        num = [w_inter[h] * _dot(cur[h]["qs"], c_mat[h].astype(BF16)) + _dot(sc[h].astype(BF16), cur[h]["vh"])
               for h in heads]
        den = [w_inter[h] * _dot(cur[h]["qs"], n_b[h].astype(BF16)) + jnp.sum(sc[h], axis=1, keepdims=True)
               for h in heads]
        m_new = [jnp.maximum(s["b_last"] + m_prev[s["h"]], s["kv_max"]) for s in cur]
        kw = [(s["kh"].astype(F32) * jnp.exp(s["log_kv"] - m_new[s["h"]])[:, :ML_QK]).astype(BF16) for s in cur]
        decay = [jnp.exp(s["b_last"] + m_prev[s["h"]] - m_new[s["h"]]) for s in cur]
        c_mat = [decay[h] * c_mat[h] + _dot_tn(kw[h], cur[h]["vh"]) for h in heads]
        n_b = [decay[h] * n_b[h] + _dot_tn(kw[h], ones_cl) for h in heads]
        m_prev = m_new
        for h in heads:
            s = cur[h]
            hid = num[h] / jnp.maximum(jnp.abs(den[h]), jnp.exp(-m_out[h]))
            hid = hid * lax.rsqrt(jnp.mean(hid * hid, axis=-1, keepdims=True) + NORM_EPS)
            hid = hid * ng_ref[:, s["lanes"]]
            gate = _sigmoid(og_ref[s["rows"], s["lanes"]].astype(F32))
            o_ref[s["rows"], s["lanes"]] = (hid * gate).astype(o_ref.dtype)
    for h in heads:
        c_ref[h] = c_mat[h]
        n_ref[h] = n_b[h]
        m_ref[h] = jnp.broadcast_to(m_prev[h], (8, HEAD_DIM))


def _mlstm(zm, zs, i_bias, f_bias, norm_g, n_ch):
    t = zm.shape[0]
    c = ML_CHUNK * n_ch
    qk_w = HEADS * ML_QK
    par = jnp.zeros((8, SMALL_COLS), F32)
    par = par.at[2, S_I:S_I + HEADS].set(i_bias).at[2, S_F:S_F + HEADS].set(f_bias)
    return pl.pallas_call(
        functools.partial(_ml_kernel, n_ch=n_ch),
        out_shape=jax.ShapeDtypeStruct((t, BRANCH_WIDTH), BF16),
        grid=(t // c,),
        in_specs=[pl.BlockSpec((c, qk_w), lambda n: (n, D_Q // qk_w)),
                  pl.BlockSpec((c, qk_w), lambda n: (n, D_K // qk_w)),
                  pl.BlockSpec((c, BRANCH_WIDTH), lambda n: (n, D_V // BRANCH_WIDTH)),
                  pl.BlockSpec((c, BRANCH_WIDTH), lambda n: (n, D_O // BRANCH_WIDTH)),
                  pl.BlockSpec((c, SMALL_COLS), lambda n: (n, 0)),
                  pl.BlockSpec((8, SMALL_COLS), lambda n: (0, 0)),
                  pl.BlockSpec((1, BRANCH_WIDTH), lambda n: (0, 0))],
        out_specs=pl.BlockSpec((c, BRANCH_WIDTH), lambda n: (n, 0)),
        scratch_shapes=[pltpu.VMEM((HEADS, ML_QK, HEAD_DIM), F32),
                        pltpu.VMEM((HEADS, ML_QK, HEAD_DIM), F32),
                        pltpu.VMEM((HEADS, 8, HEAD_DIM), F32)],
        compiler_params=_cparams(1),
        name="mlstm",
    )(zm, zm, zm, zm, zs, par, norm_g.reshape(1, -1))


def _merge_kernel(a_ref, b_ref, c_ref, d_ref, g0_ref, g1_ref, g2_ref, g3_ref, wb_ref, o_ref):
    acc = None
    for g, (br, gr) in enumerate(((a_ref, g0_ref), (b_ref, g1_ref), (c_ref, g2_ref), (d_ref, g3_ref))):
        term = _sigmoid(gr[...].astype(F32)) * _dot(br[...], wb_ref[g])
        acc = term if acc is None else acc + term
    o_ref[...] = acc.astype(o_ref.dtype)


def _merge(outs, zm, wb16, tm):
    t = zm.shape[0]
    branch = pl.BlockSpec((tm, BRANCH_WIDTH), lambda i: (i, 0))
    gate = lambda g: pl.BlockSpec((tm, D_MODEL), lambda i: (i, g))
    return pl.pallas_call(
        _merge_kernel,
        out_shape=jax.ShapeDtypeStruct((t, D_MODEL), BF16),
        grid=(t // tm,),
        in_specs=[branch] * 4 + [gate(0), gate(1), gate(2), gate(3),
                                 pl.BlockSpec((N_BRANCHES, BRANCH_WIDTH, D_MODEL), lambda i: (0, 0, 0))],
        out_specs=pl.BlockSpec((tm, D_MODEL), lambda i: (i, 0)),
        compiler_params=_cparams(1),
        name="merge_branches",
    )(*outs, zm, zm, zm, zm, wb16)


def _proj_ln_kernel(a_ref, w_ref, x_ref, g_ref, b_ref, o_ref):
    y = ALPHA * x_ref[...] + _dot(a_ref[...], w_ref[...])
    o_ref[...] = _layer_norm(y, g_ref[...], b_ref[...])


def _proj_ln(a, w16, x, g, b, tm):
    t, k = a.shape
    return pl.pallas_call(
        _proj_ln_kernel,
        out_shape=jax.ShapeDtypeStruct((t, D_MODEL), F32),
        grid=(t // tm,),
        in_specs=[pl.BlockSpec((tm, k), lambda i: (i, 0)),
                  pl.BlockSpec((k, D_MODEL), lambda i: (0, 0)),
                  pl.BlockSpec((tm, D_MODEL), lambda i: (i, 0)),
                  pl.BlockSpec((1, D_MODEL), lambda i: (0, 0)),
                  pl.BlockSpec((1, D_MODEL), lambda i: (0, 0))],
        out_specs=pl.BlockSpec((tm, D_MODEL), lambda i: (i, 0)),
        compiler_params=_cparams(1),
        name="out_proj_layernorm",
    )(a, w16, x, g.reshape(1, -1), b.reshape(1, -1))


def _router_kernel(x_ref, wr_ref, bias_ref, idx_ref, w_ref, *, tm):
    logits = _dot_f32_nt(wr_ref[...], x_ref[...])
    scores = _sigmoid(logits)
    biased = scores + bias_ref[...]
    neg = -jnp.inf

    b3 = biased.reshape(N_GROUPS, GROUP_SIZE, tm)
    pos = lax.broadcasted_iota(jnp.int32, (N_GROUPS, GROUP_SIZE, tm), 1)
    m1 = jnp.max(b3, axis=1, keepdims=True)
    first = jnp.min(jnp.where(b3 == m1, pos, GROUP_SIZE), axis=1, keepdims=True)
    m2 = jnp.max(jnp.where(pos == first, neg, b3), axis=1, keepdims=True)
    gscore = m1 + m2

    gidx = lax.broadcasted_iota(jnp.int32, (N_GROUPS, 1, tm), 0)
    gsel = jnp.zeros((N_GROUPS, 1, tm), F32)
    for _ in range(TOPK_GROUPS):
        gm = jnp.max(gscore, axis=0, keepdims=True)
        gfirst = jnp.min(jnp.where(gscore == gm, gidx, N_GROUPS), axis=0, keepdims=True)
        hit = gidx == gfirst
        gsel = jnp.where(hit, 1.0, gsel)
        gscore = jnp.where(hit, neg, gscore)

    allowed = jnp.broadcast_to(gsel, (N_GROUPS, GROUP_SIZE, tm)) > 0.0
    masked = jnp.where(allowed, b3, neg).reshape(N_EXPERTS, tm)
    eidx = lax.broadcasted_iota(jnp.int32, (N_EXPERTS, tm), 0)
    idx_rows, sel_rows = [], []
    for _ in range(TOP_K):
        em = jnp.max(masked, axis=0, keepdims=True)
        efirst = jnp.min(jnp.where(masked == em, eidx, N_EXPERTS), axis=0, keepdims=True)
        hit = eidx == efirst
        idx_rows.append(efirst)
        sel_rows.append(jnp.sum(jnp.where(hit, scores, 0.0), axis=0, keepdims=True))
        masked = jnp.where(hit, neg, masked)
    total = sel_rows[0]
    for r in sel_rows[1:]:
        total = total + r
    zero_i = jnp.zeros((1, tm), jnp.int32)
    zero_f = jnp.zeros((1, tm), F32)
    idx_ref[...] = jnp.concatenate(idx_rows + [zero_i, zero_i], axis=0)
    w_ref[...] = jnp.concatenate([r / total * ROUTED_SCALE for r in sel_rows] + [zero_f, zero_f], axis=0)


def _dot_f32_nt(a, b):
    ah, al = _split2(a)
    bh, bl = _split2(b)
    return _dot_nt(ah, bh) + _dot_nt(ah, bl) + _dot_nt(al, bh)


def _router(x, w_router, router_bias, tm):
    t = x.shape[0]
    return pl.pallas_call(
        functools.partial(_router_kernel, tm=tm),
        out_shape=(jax.ShapeDtypeStruct((8, t), jnp.int32), jax.ShapeDtypeStruct((8, t), F32)),
        grid=(t // tm,),
        in_specs=[pl.BlockSpec((tm, D_MODEL), lambda i: (i, 0)),
                  pl.BlockSpec((N_EXPERTS, D_MODEL), lambda i: (0, 0)),
                  pl.BlockSpec((N_EXPERTS, 1), lambda i: (0, 0))],
        out_specs=(pl.BlockSpec((8, tm), lambda i: (0, i)), pl.BlockSpec((8, tm), lambda i: (0, i))),
        compiler_params=_cparams(1),
        name="router_topk",
    )(x, w_router.T, router_bias.reshape(N_EXPERTS, 1))


def _dispatch_kernel(pos_ref, x_ref, xs_hbm, sem, *, td):
    for r in range(td):
        for k in range(TOP_K):
            pltpu.make_async_copy(x_ref.at[pl.ds(r, 1), :],
                                  xs_hbm.at[pl.ds(pos_ref[0, r * TOP_K + k], 1), :], sem).start()
    for _ in range(TOP_K):
        pltpu.make_async_copy(x_ref, xs_hbm.at[pl.ds(0, td), :], sem).wait()


def _dispatch(x, pos, td):
    t = x.shape[0]
    return pl.pallas_call(
        functools.partial(_dispatch_kernel, td=td),
        out_shape=jax.ShapeDtypeStruct((t * TOP_K, D_MODEL), F32),
        grid=(t // td,),
        in_specs=[pl.BlockSpec((None, 1, td * TOP_K), lambda i: (i, 0, 0), memory_space=pltpu.SMEM),
                  pl.BlockSpec((td, D_MODEL), lambda i: (i, 0))],
        out_specs=pl.BlockSpec(memory_space=pl.ANY),
        scratch_shapes=[pltpu.SemaphoreType.DMA],
        compiler_params=_cparams(1),
        name="moe_dispatch",
    )(pos.reshape(t // td, 1, td * TOP_K), x)


def _expert_kernel(n_ref, tile_ref, exp_ref, nxt_ref, par_ref, lo_ref, hi_ref,
                   x_ref, wg_hbm, wu_hbm, wd_hbm, y_ref,
                   wgbuf, wubuf, wdbuf, wg16, wu16, wd16, wsem, *, tm, layer):
    j = pl.program_id(0)
    prev = jnp.maximum(j - 1, 0)
    first = (j == 0) | (tile_ref[prev] != tile_ref[j])
    expert = exp_ref[j]
    slot = par_ref[j]

    def copies(ex, s):
        return [pltpu.make_async_copy(hbm.at[layer, ex], buf.at[s], wsem.at[s])
                for hbm, buf in ((wg_hbm, wgbuf), (wu_hbm, wubuf), (wd_hbm, wdbuf))]

    @pl.when(j < n_ref[0])
    def _():
        @pl.when((j == 0) | (exp_ref[prev] != expert))
        def _():
            @pl.when(j == 0)
            def _():
                for cp in copies(expert, slot):
                    cp.start()

            @pl.when(nxt_ref[j] != expert)
            def _():
                for cp in copies(nxt_ref[j], 1 - slot):
                    cp.start()

            for cp in copies(expert, slot):
                cp.wait()
            wg16[...] = wgbuf[slot].astype(BF16)
            wu16[...] = wubuf[slot].astype(BF16)
            wd16[...] = wdbuf[slot].astype(BF16)

        x = x_ref[...].astype(BF16)
        hidden = _silu(_dot(x, wg16[...])) * _dot(x, wu16[...])
        y = _dot(hidden.astype(BF16), wd16[...])

        @pl.when(first)
        def _():
            y_ref[...] = y

        @pl.when(jnp.logical_not(first))
        def _():
            row = lax.broadcasted_iota(jnp.int32, (tm, 1), 0)
            mine = (row >= lo_ref[j]) & (row < hi_ref[j])
            y_ref[...] = jnp.where(mine, y, y_ref[...])


def _experts(xs, sched, w_gate, w_up, w_down, layer, tm):
    n_max = sched[1].shape[0]
    tile_of = lambda j, n, tl, *rest: tl[j]
    hbm = pl.BlockSpec(memory_space=pl.ANY)
    up_shape, down_shape = (D_MODEL, EXPERT_DIM), (EXPERT_DIM, D_MODEL)
    return pl.pallas_call(
        functools.partial(_expert_kernel, tm=tm, layer=layer),
        out_shape=jax.ShapeDtypeStruct(xs.shape, F32),
        grid_spec=pltpu.PrefetchScalarGridSpec(
            num_scalar_prefetch=len(sched),
            grid=(n_max,),
            in_specs=[pl.BlockSpec((tm, D_MODEL), lambda *a: (tile_of(*a), 0)), hbm, hbm, hbm],
            out_specs=pl.BlockSpec((tm, D_MODEL), lambda *a: (tile_of(*a), 0)),
            scratch_shapes=[pltpu.VMEM((2,) + up_shape, F32), pltpu.VMEM((2,) + up_shape, F32),
                            pltpu.VMEM((2,) + down_shape, F32),
                            pltpu.VMEM(up_shape, BF16), pltpu.VMEM(up_shape, BF16),
                            pltpu.VMEM(down_shape, BF16),
                            pltpu.SemaphoreType.DMA((2,))]),
        compiler_params=_cparams(1),
        name="routed_experts",
    )(*sched, xs, w_gate, w_up, w_down)


def _expert_schedule(idx, t, tm):
    i32 = jnp.int32
    n_rows = t * TOP_K
    n_tiles = n_rows // tm
    n_max = n_tiles + N_EXPERTS - 1
    flat_e = idx.reshape(-1)
    onehot = (flat_e[:, None] == jnp.arange(N_EXPERTS, dtype=i32)[None, :]).astype(i32)
    seen = jnp.cumsum(onehot, axis=0)
    counts = seen[-1]
    end = jnp.cumsum(counts)
    start = end - counts
    pos = jnp.sum(onehot * (seen - 1 + start[None, :]), axis=1).astype(i32)
    first_tile = start // tm
    items_per_e = jnp.where(counts > 0, (end - 1) // tm - first_tile + 1, 0)
    item_end = jnp.cumsum(items_per_e)
    n_items = item_end[-1]
    j = jnp.arange(n_max, dtype=i32)
    e = jnp.minimum(jnp.sum((item_end[None, :] <= j[:, None]).astype(i32), axis=1), N_EXPERTS - 1)
    tile = first_tile[e] + j - (item_end[e] - items_per_e[e])
    lo = jnp.maximum(start[e], tile * tm) - tile * tm
    hi = jnp.minimum(end[e], (tile + 1) * tm) - tile * tm
    valid = j < n_items
    last = jnp.maximum(n_items - 1, 0)
    pick = lambda a: jnp.where(valid, a, a[last]).astype(i32)
    ids = jnp.arange(N_EXPERTS, dtype=i32)
    later = (ids[None, :] > ids[:, None]) & (counts[None, :] > 0)
    nxt = jnp.min(jnp.where(later, ids[None, :], N_EXPERTS), axis=1)
    nxt = jnp.where(nxt == N_EXPERTS, ids, nxt)
    parity = (jnp.cumsum((counts > 0).astype(i32)) - 1) % 2
    return pos, (n_items.reshape(1).astype(i32), pick(tile), pick(e), pick(nxt[e]), pick(parity[e]),
                 pick(lo), pick(hi))


def _combine_kernel(pos_ref, pos_next_ref, x_ref, w_ref, ys_hbm, sg_ref, su_ref, sd_ref, g_ref, b_ref,
                    o_ref, o16_ref, ybuf, sem, *, tt, n_steps):
    i = pl.program_id(0)
    slot = lax.rem(i, 2)

    def gather(idx_ref, s):
        for r in range(tt):
            for k in range(TOP_K):
                pltpu.make_async_copy(ys_hbm.at[pl.ds(idx_ref[0, r * TOP_K + k], 1), :],
                                      ybuf.at[s, k, pl.ds(r, 1), :], sem.at[s]).start()

    @pl.when(i == 0)
    def _():
        gather(pos_ref, 0)

    @pl.when(i + 1 < n_steps)
    def _():
        gather(pos_next_ref, 1 - slot)

    x = x_ref[...]
    x16 = x.astype(BF16)
    hidden = _silu(_dot(x16, sg_ref[...])) * _dot(x16, su_ref[...])
    acc = _dot(hidden.astype(BF16), sd_ref[...])
    for k in range(TOP_K):
        pltpu.make_async_copy(ys_hbm.at[pl.ds(0, tt), :], ybuf.at[slot, k], sem.at[slot]).wait()
    w = w_ref[...]
    for k in range(TOP_K):
        acc = acc + ybuf[slot, k] * w[:, k:k + 1]
    out = _layer_norm(ALPHA * x + acc, g_ref[...], b_ref[...])
    o_ref[...] = out
    o16_ref[...] = out.astype(BF16)


def _combine(x, w_tok, pos, ys, sg16, su16, sd16, g, b, tt):
    t = x.shape[0]
    nb = t // tt
    pos3 = pos.reshape(nb, 1, tt * TOP_K)
    tile = pl.BlockSpec((tt, D_MODEL), lambda i: (i, 0))
    vec = pl.BlockSpec((1, D_MODEL), lambda i: (0, 0))
    idx_spec = lambda f: pl.BlockSpec((None, 1, tt * TOP_K), f, memory_space=pltpu.SMEM)
    return pl.pallas_call(
        functools.partial(_combine_kernel, tt=tt, n_steps=nb),
        out_shape=(jax.ShapeDtypeStruct((t, D_MODEL), F32), jax.ShapeDtypeStruct((t, D_MODEL), BF16)),
        grid=(nb,),
        in_specs=[idx_spec(lambda i: (i, 0, 0)), idx_spec(lambda i: (jnp.minimum(i + 1, nb - 1), 0, 0)),
                  tile, pl.BlockSpec((tt, 8), lambda i: (i, 0)),
                  pl.BlockSpec(memory_space=pl.ANY),
                  pl.BlockSpec((D_MODEL, SHARED_DIM), lambda i: (0, 0)),
                  pl.BlockSpec((D_MODEL, SHARED_DIM), lambda i: (0, 0)),
                  pl.BlockSpec((SHARED_DIM, D_MODEL), lambda i: (0, 0)), vec, vec],
        out_specs=(tile, tile),
        scratch_shapes=[pltpu.VMEM((2, TOP_K, tt, D_MODEL), F32), pltpu.SemaphoreType.DMA((2,))],
        compiler_params=_cparams(1),
        name="moe_combine_layernorm",
    )(pos3, pos3, x, w_tok, ys, sg16, su16, sd16, g.reshape(1, -1), b.reshape(1, -1))


def _tile(t, pref):
    return min(t, pref)


def _in_proj(x16, w_in):
    t = x16.shape[0]
    cols = lambda r: w_in[:, r[0]:r[1]].astype(BF16)
    z_abc = _matmul(x16, cols(ABC_COLS), BF16, _tile(t, 1024), 1536, "in_proj_abc")
    z_d = _matmul(x16, cols(D_COLS), BF16, _tile(t, 1024), 1536, "in_proj_d")
    z_gate = _matmul(x16, cols(GATE_COLS), BF16, _tile(t, 1024), 1024, "in_proj_gate")
    pad = jnp.zeros((D_MODEL, SMALL_COLS - sum(b - a for a, b in SMALL_SRC)), F32)
    w_small = jnp.concatenate([w_in[:, a:b] for a, b in SMALL_SRC] + [pad], axis=1).astype(BF16)
    zs = _matmul(x16, w_small, F32, _tile(t, 1024), SMALL_COLS, "in_proj_small")
    return z_abc, z_d, z_gate, zs


def _mixer(x, x16, p):
    t = x.shape[0]
    zm, z_d, z_gate, zs = _in_proj(x16, p["w_in"])
    out_a = _sb_attention(zm, 256, 256, 8)
    out_b = _gmlp(zm, p["gm_norm_g"], p["gm_norm_b"], p["gm_w_s"], p["gm_b_s"], _tile(t, 512))
    out_c = _gdn(zm, zs, p["gdn_conv_w"], p["gdn_a_log"], p["gdn_dt_bias"], p["gdn_norm_g"], 4, 4)
    out_d = _mlstm(z_d, zs, p["ml_i_bias"], p["ml_f_bias"], p["ml_norm_g"], 2)
    merged = _merge((out_a, out_b, out_c, out_d), z_gate, p["w_branch"].astype(BF16), _tile(t, 256))
    return _proj_ln(merged, p["w_out"].astype(BF16), x, p["ln1_g"], p["ln1_b"], _tile(t, 256))


def _moe(x, p, tm):
    t = x.shape[0]
    idx_t, w_t = _router(x, p["w_router"], p["router_bias"], _tile(t, 512))
    idx = idx_t[:TOP_K].T
    w_tok = w_t.T
    pos, sched = _expert_schedule(idx, t, tm)
    xs = _dispatch(x, pos, _tile(t, 256))
    ys = _experts(xs, sched, p["w_gate_all"], p["w_up_all"], p["w_down_all"], p["layer"], tm)
    return _combine(x, w_tok, pos, ys, p["ws_gate"].astype(BF16), p["ws_up"].astype(BF16),
                    p["ws_down"].astype(BF16), p["ln2_g"], p["ln2_b"], _tile(t, 128))


def _layer(x, x16, p, tm_expert):
    x1 = _mixer(x, x16, p)
    return _moe(x1, p, tm_expert)


_STACKED = ("w_gate", "w_up", "w_down")


def _layer_params(params, l):
    p = {k: v[l] for k, v in params.items() if k not in _STACKED}
    p.update({k + "_all": params[k] for k in _STACKED})
    p["layer"] = l
    return p


def kernel(x, w_in, gm_norm_g, gm_norm_b, gm_w_s, gm_b_s, gdn_conv_w, gdn_a_log, gdn_dt_bias, gdn_norm_g, ml_i_bias, ml_f_bias, ml_norm_g, w_branch, w_out, ln1_g, ln1_b, w_router, router_bias, w_gate, w_up, w_down, ws_gate, ws_up, ws_down, ln2_g, ln2_b):
    params = dict(w_in=w_in, gm_norm_g=gm_norm_g, gm_norm_b=gm_norm_b, gm_w_s=gm_w_s, gm_b_s=gm_b_s,
                  gdn_conv_w=gdn_conv_w, gdn_a_log=gdn_a_log, gdn_dt_bias=gdn_dt_bias,
                  gdn_norm_g=gdn_norm_g, ml_i_bias=ml_i_bias, ml_f_bias=ml_f_bias, ml_norm_g=ml_norm_g,
                  w_branch=w_branch, w_out=w_out, ln1_g=ln1_g, ln1_b=ln1_b, w_router=w_router,
                  router_bias=router_bias, w_gate=w_gate, w_up=w_up, w_down=w_down, ws_gate=ws_gate,
                  ws_up=ws_up, ws_down=ws_down, ln2_g=ln2_g, ln2_b=ln2_b)
    b, t, d = x.shape
    h = x.reshape(b * t, d)
    h16 = h.astype(BF16)
    for l in range(DEPTH):
        h, h16 = _layer(h, h16, _layer_params(params, l), 256)
    return h.reshape(b, t, d)
```

```python
import functools

import jax
import jax.numpy as jnp
from jax import lax
from jax.experimental import pallas as pl
from jax.experimental.pallas import tpu as pltpu

F32 = jnp.float32
BF16 = jnp.bfloat16

D_MODEL = 2048
N_BRANCHES = 4
BRANCH_WIDTH = 512
HEADS = 4
HEAD_DIM = 128
GM_CHUNK = 128
GDN_CHUNK = 64
GDN_CONV = 4
ML_CHUNK = 64
ML_QK = 64
N_EXPERTS = 64
N_GROUPS = 8
GROUP_SIZE = 8
TOPK_GROUPS = 4
TOP_K = 6
EXPERT_DIM = 512
SHARED_DIM = 512
ROUTED_SCALE = 2.5
DEPTH = 2
ALPHA = (2 * DEPTH) ** 0.25
NORM_EPS = 1e-5
LOG2E = 1.4426950408889634

ABC_COLS = (0, 4608)
A_Q, A_K, A_V = 0, 512, 1024
B_U, B_V = 1536, 2048
C_Q, C_K, C_V, C_G = 2560, 3072, 3584, 4096
D_COLS = (4616, 6152)
D_Q, D_K, D_V, D_O = 0, 256, 512, 1024
GATE_COLS = (6160, 6160 + N_BRANCHES * D_MODEL)
SMALL_SRC = ((4608, 4616), (6152, 6160))
S_BETA, S_A, S_I, S_F = 0, 4, 8, 12
SMALL_COLS = 128

VMEM_LIMIT = 56 * 1024 * 1024


def _cparams(n_axes):
    return pltpu.CompilerParams(
        dimension_semantics=("arbitrary",) * n_axes, vmem_limit_bytes=VMEM_LIMIT)


def _dot(a, b):
    return jnp.dot(a, b, preferred_element_type=F32)


def _dot_nt(a, b):
    return lax.dot_general(a, b, (((1,), (1,)), ((), ())), preferred_element_type=F32)


def _dot_tn(a, b):
    return lax.dot_general(a, b, (((0,), (0,)), ((), ())), preferred_element_type=F32)


def _split2(a):
    hi = a.astype(BF16)
    lo = (a - hi.astype(F32)).astype(BF16)
    return hi, lo


def _split3(a):
    hi = a.astype(BF16)
    r = a - hi.astype(F32)
    mid = r.astype(BF16)
    lo = (r - mid.astype(F32)).astype(BF16)
    return hi, mid, lo


def _dot_sel_lhs(sel, a):
    sel3 = jnp.concatenate([sel, sel, sel], axis=1)
    return _dot(sel3, jnp.concatenate(_split3(a), axis=0))


def _dot_sel_rhs(a, sel):
    hi, lo = _split2(a)
    return _dot(hi, sel) + _dot(lo, sel)


def _dot_f32(a, b):
    ah, al = _split2(a)
    bh, bl = _split2(b)
    return _dot(jnp.concatenate([ah, ah, al], axis=1), jnp.concatenate([bh, bl, bh], axis=0))


def _softplus(x):
    return jnp.maximum(x, 0.0) + jnp.log1p(jnp.exp(-jnp.abs(x)))


def _sigmoid(x):
    return 1.0 / (1.0 + jnp.exp(-x))


def _silu(x):
    return x * _sigmoid(x)


def _layer_norm(x, g, b):
    mu = jnp.mean(x, axis=-1, keepdims=True)
    xc = x - mu
    var = jnp.mean(xc * xc, axis=-1, keepdims=True)
    return xc * lax.rsqrt(var + NORM_EPS) * g + b


def _mm_kernel(x_ref, w_ref, o_ref):
    o_ref[...] = _dot(x_ref[...], w_ref[...]).astype(o_ref.dtype)


def _matmul(x, w, out_dtype, tm, tn, name):
    m, k = x.shape
    n = w.shape[1]
    return pl.pallas_call(
        _mm_kernel,
        out_shape=jax.ShapeDtypeStruct((m, n), out_dtype),
        grid=(n // tn, m // tm),
        in_specs=[pl.BlockSpec((tm, k), lambda j, i: (i, 0)),
                  pl.BlockSpec((k, tn), lambda j, i: (0, j))],
        out_specs=pl.BlockSpec((tm, tn), lambda j, i: (i, j)),
        compiler_params=_cparams(2),
        name=name,
    )(x, w)


def _mm_cols_kernel(x_ref, wa_ref, wb_ref, o_ref, w16, *, shift, tn, row_chunk):
    @pl.when(pl.program_id(1) == 0)
    def _():
        for r in range(0, w16.shape[0], row_chunk):
            rows = slice(r, r + row_chunk)
            if shift == 0:
                w16[rows, :] = wa_ref[rows, :].astype(BF16)
            else:
                wide = jnp.concatenate([wa_ref[rows, :], wb_ref[rows, :]], axis=1)
                w16[rows, :] = wide[:, shift:shift + tn].astype(BF16)

    o_ref[...] = _dot(x_ref[...], w16[...]).astype(o_ref.dtype)


def _matmul_cols(x, w_all, layer, start, n_cols, out_dtype, tm, tn, name):
    m, k = x.shape
    shift = start % 128
    base = start - shift
    assert base % tn == 0 and tn % 128 == 0 and n_cols % tn == 0
    return pl.pallas_call(
        functools.partial(_mm_cols_kernel, shift=shift, tn=tn, row_chunk=256),
        out_shape=jax.ShapeDtypeStruct((m, n_cols), out_dtype),
        grid=(n_cols // tn, m // tm),
        in_specs=[pl.BlockSpec((tm, k), lambda j, i: (i, 0)),
                  pl.BlockSpec((None, k, tn), lambda j, i: (layer, 0, base // tn + j)),
                  pl.BlockSpec((None, k, 128), lambda j, i: (layer, 0, (base + (j + 1) * tn) // 128))],
        out_specs=pl.BlockSpec((tm, tn), lambda j, i: (i, j)),
        scratch_shapes=[pltpu.VMEM((k, tn), BF16)],
        compiler_params=_cparams(2),
        name=name,
    )(x, w_all, w_all)


def _sb_kernel(q_ref, k_ref, v_ref, o_ref, *, tq, tk, per_trip):
    i = pl.program_id(1)
    q = (q_ref[...].astype(F32) * (HEAD_DIM ** -0.5 * LOG2E)).astype(BF16)
    half = tk // 2
    row = i * tq + lax.broadcasted_iota(jnp.int32, (tq, tk), 0)
    col0 = lax.broadcasted_iota(jnp.int32, (tq, tk), 1)
    jj = lax.broadcasted_iota(jnp.int32, (tk, tk), 0)
    ss = lax.broadcasted_iota(jnp.int32, (tk, tk), 1)
    sel = jnp.where((jj >= ss) & ((jj < half) == (ss < half)), 1.0, 0.0).astype(BF16)

    def blocks(starts, carry, on_diagonal):
        zs = [_dot_nt(q, k_ref[pl.ds(ks, tk), :]) for ks in starts]
        sps = [jnp.maximum(z, 0.0) + jnp.log(1.0 + jnp.exp2(-jnp.abs(z))) * LOG2E for z in zs]
        if on_diagonal:
            masks = [(col0 + ks) < row for ks in starts]
            sps = [jnp.where(m, sp, 0.0) for m, sp in zip(masks, sps)]
        tails = [_dot(sp.astype(BF16), sel) for sp in sps]
        bases = [z - tail for z, tail in zip(zs, tails)]
        late_total = [jnp.broadcast_to(tail[:, half:half + 1], (tq, half)) for tail in tails]
        early_total = [jnp.broadcast_to(tail[:, 0:1], (tq, half)) for tail in tails]
        acc, run = carry
        for n, ks in enumerate(starts):
            run_mid = run + late_total[n]
            a = jnp.exp2(jnp.concatenate([bases[n][:, :half] - run_mid, bases[n][:, half:] - run], axis=1))
            if on_diagonal:
                a = jnp.where(masks[n], a, 0.0)
            acc = acc + _dot(a.astype(BF16), v_ref[pl.ds(ks, tk), :])
            run = run_mid + early_total[n]
        return acc, run

    carry = blocks([pl.multiple_of(i * tq, tq)],
                   (jnp.zeros((tq, HEAD_DIM), F32), jnp.zeros((tq, half), F32)), True)

    def run_groups(size, top, trips, carry):
        def group(step, carry):
            starts = [pl.multiple_of((top - 1 - size * step - s) * tk, tk) for s in range(size)]
            return blocks(starts, carry, False)
        return lax.fori_loop(0, trips, group, carry)

    carry = run_groups(per_trip, i, i // per_trip, carry)
    left = i % per_trip
    size = per_trip // 2
    while size >= 1:
        carry = run_groups(size, left, left // size, carry)
        left = left % size
        size //= 2
    acc, _ = carry
    o_ref[...] = acc.astype(o_ref.dtype)


def _sb_attention(zm, tq, tk, per_trip):
    t = zm.shape[0]
    assert tq == tk and tk % 256 == 0
    cq, ck, cv = A_Q // HEAD_DIM, A_K // HEAD_DIM, A_V // HEAD_DIM
    return pl.pallas_call(
        functools.partial(_sb_kernel, tq=tq, tk=tk, per_trip=per_trip),
        out_shape=jax.ShapeDtypeStruct((t, BRANCH_WIDTH), BF16),
        grid=(HEADS, t // tq),
        in_specs=[pl.BlockSpec((tq, HEAD_DIM), lambda h, i: (i, cq + h)),
                  pl.BlockSpec((t, HEAD_DIM), lambda h, i: (0, ck + h)),
                  pl.BlockSpec((t, HEAD_DIM), lambda h, i: (0, cv + h))],
        out_specs=pl.BlockSpec((tq, HEAD_DIM), lambda h, i: (i, h)),
        compiler_params=_cparams(2),
        name="sb_attention",
    )(zm, zm, zm)


def _gm_kernel(u_ref, v_ref, ng_ref, nb_ref, ws_ref, bs_ref, o_ref, *, n_chunk):
    u = jax.nn.gelu(u_ref[...].astype(F32))
    v = jax.nn.gelu(v_ref[...].astype(F32))
    v = _layer_norm(v, ng_ref[...], nb_ref[...])
    ii = lax.broadcasted_iota(jnp.int32, (GM_CHUNK, GM_CHUNK), 0)
    jj = lax.broadcasted_iota(jnp.int32, (GM_CHUNK, GM_CHUNK), 1)
    for g in range(HEADS):
        w = jnp.where(ii >= jj, ws_ref[g], 0.0).astype(BF16)
        bias = bs_ref[g]
        for c in range(n_chunk):
            rows = slice(c * GM_CHUNK, (c + 1) * GM_CHUNK)
            cols = slice(g * HEAD_DIM, (g + 1) * HEAD_DIM)
            mixed = _dot(w, v[rows, cols].astype(BF16)) + bias
            o_ref[rows, cols] = (u[rows, cols] * mixed).astype(o_ref.dtype)


def _gmlp(zm, norm_g, norm_b, w_s, b_s, tb):
    t = zm.shape[0]
    bias = jnp.broadcast_to(b_s[:, :, None], (HEADS, GM_CHUNK, HEAD_DIM))
    cu, cv = B_U // BRANCH_WIDTH, B_V // BRANCH_WIDTH
    return pl.pallas_call(
        functools.partial(_gm_kernel, n_chunk=tb // GM_CHUNK),
        out_shape=jax.ShapeDtypeStruct((t, BRANCH_WIDTH), BF16),
        grid=(t // tb,),
        in_specs=[pl.BlockSpec((tb, BRANCH_WIDTH), lambda i: (i, cu)),
                  pl.BlockSpec((tb, BRANCH_WIDTH), lambda i: (i, cv)),
                  pl.BlockSpec((1, BRANCH_WIDTH), lambda i: (0, 0)),
                  pl.BlockSpec((1, BRANCH_WIDTH), lambda i: (0, 0)),
                  pl.BlockSpec((HEADS, GM_CHUNK, GM_CHUNK), lambda i: (0, 0, 0)),
                  pl.BlockSpec((HEADS, GM_CHUNK, HEAD_DIM), lambda i: (0, 0, 0))],
        out_specs=pl.BlockSpec((tb, BRANCH_WIDTH), lambda i: (i, 0)),
        compiler_params=_cparams(1),
        name="gmlp",
    )(zm, zm, norm_g.reshape(1, -1), norm_b.reshape(1, -1), w_s, bias)


def _gdn_prep_kernel(q_ref, k_ref, v_ref, zs_ref, cw_ref, par_ref,
                     u_ref, w_ref, qe_ref, kd_ref, attn_ref, eg_ref, xs_ref, *, n_ch):
    c = GDN_CHUNK
    w3 = 3 * BRANCH_WIDTH
    rows_all = n_ch * c

    @pl.when(pl.program_id(0) == 0)
    def _():
        xs_ref[0:8, :] = jnp.zeros((8, w3), F32)

    xs_ref[8:8 + rows_all, 0:BRANCH_WIDTH] = q_ref[...].astype(F32)
    xs_ref[8:8 + rows_all, BRANCH_WIDTH:2 * BRANCH_WIDTH] = k_ref[...].astype(F32)
    xs_ref[8:8 + rows_all, 2 * BRANCH_WIDTH:w3] = v_ref[...].astype(F32)
    cw = cw_ref[...]
    y_all = xs_ref[5:5 + rows_all, :] * cw[0:1, :]
    for tap in range(1, GDN_CONV):
        y_all = y_all + xs_ref[5 + tap:5 + tap + rows_all, :] * cw[tap:tap + 1, :]
    xs_ref[0:8, :] = xs_ref[rows_all:rows_all + 8, :]
    y_all = _silu(y_all)

    zs_all = zs_ref[...]
    beta_full = _sigmoid(zs_all)
    g_full = -jnp.exp(par_ref[0:1, :]) * _softplus(zs_all + par_ref[1:2, :])

    ii = lax.broadcasted_iota(jnp.int32, (c, c), 0)
    jj = lax.broadcasted_iota(jnp.int32, (c, c), 1)
    tri_incl = jnp.where(ii >= jj, 1.0, 0.0).astype(BF16)
    ones_cc = jnp.ones((c, c), BF16)

    chains = []
    for ch, h in [(ch, h) for ch in range(n_ch) for h in range(HEADS)]:
        rows = slice(ch * c, (ch + 1) * c)
        y = y_all[rows]
        beta_all = beta_full[rows]
        g_all = g_full[rows]
        lanes = slice(h * HEAD_DIM, (h + 1) * HEAD_DIM)
        qh = y[:, h * HEAD_DIM:(h + 1) * HEAD_DIM]
        kh = y[:, BRANCH_WIDTH + h * HEAD_DIM:BRANCH_WIDTH + (h + 1) * HEAD_DIM]
        vh = y[:, 2 * BRANCH_WIDTH + h * HEAD_DIM:2 * BRANCH_WIDTH + (h + 1) * HEAD_DIM]
        qh = qh * lax.rsqrt(jnp.sum(qh * qh, axis=-1, keepdims=True) + 1e-6) * HEAD_DIM ** -0.5
        kh = kh * lax.rsqrt(jnp.sum(kh * kh, axis=-1, keepdims=True) + 1e-6)
        beta = beta_all[:, S_BETA + h:S_BETA + h + 1]
        g_b = jnp.broadcast_to(g_all[:, S_A + h:S_A + h + 1], (c, HEAD_DIM))
        chains.append(dict(ch=ch, h=h, rows=rows, lanes=lanes, qh=qh, kh=kh, vh=vh, beta=beta, g_b=g_b))

    for s in chains:
        s["gcol"] = _dot_sel_lhs(tri_incl, s["g_b"])
        s["grow"] = _dot_sel_lhs(ones_cc, jnp.where(ii <= jj, s["g_b"][:, :c], 0.0))
        s["kh16"] = s["kh"].astype(BF16)
        s["kb"] = s["kh"] * s["beta"]
        s["kk"] = _dot_nt(s["kb"].astype(BF16), s["kh16"])
        s["qk"] = _dot_nt(s["qh"].astype(BF16), s["kh16"])
    for s in chains:
        decay = jnp.where(ii >= jj, jnp.exp(s["gcol"][:, :c] - s["grow"]), 0.0)
        expg = jnp.exp(s["gcol"])
        g_last = s["gcol"][c - 1:c, :]
        rows, lanes = s["rows"], s["lanes"]
        qe_ref[rows, lanes] = (s["qh"] * expg).astype(BF16)
        kd_ref[rows, lanes] = (s["kh"] * jnp.exp(g_last - s["gcol"])).astype(BF16)
        attn_ref[s["h"], rows, :] = (s["qk"] * decay).astype(BF16)
        eg_ref[s["ch"], :, lanes] = jnp.broadcast_to(jnp.exp(g_last), (8, HEAD_DIM))
        s["p"] = -jnp.where(ii > jj, s["kk"] * decay, 0.0)
        s["x"] = jnp.concatenate([s["vh"] * s["beta"], s["kb"] * expg], axis=1)
    for r in range(6):
        for s in chains:
            s["x"] = s["x"] + _dot_f32(s["p"], s["x"])
            if r < 5:
                s["p"] = _dot_f32(s["p"], s["p"])
    for s in chains:
        u_ref[s["rows"], s["lanes"]] = s["x"][:, :HEAD_DIM]
        w_ref[s["rows"], s["lanes"]] = s["x"][:, HEAD_DIM:].astype(BF16)


def _gdn_scan_kernel(u_ref, w_ref, qe_ref, kd_ref, attn_ref, eg_ref, gate_ref, ng_ref, o_ref,
                     s_ref, *, n_ch):
    c = GDN_CHUNK

    @pl.when(pl.program_id(0) == 0)
    def _():
        s_ref[...] = jnp.zeros(s_ref.shape, F32)

    heads = range(HEADS)
    lanes = [slice(h * HEAD_DIM, (h + 1) * HEAD_DIM) for h in heads]
    state = [s_ref[h] for h in heads]
    for ch in range(n_ch):
        rows = slice(ch * c, (ch + 1) * c)
        s16 = [state[h].astype(BF16) for h in heads]
        v_new = [(u_ref[rows, lanes[h]] - _dot(w_ref[rows, lanes[h]], s16[h])).astype(BF16) for h in heads]
        state = [state[h] * eg_ref[ch, 0:1, lanes[h]] + _dot_tn(kd_ref[rows, lanes[h]], v_new[h])
                 for h in heads]
        out = [_dot(qe_ref[rows, lanes[h]], s16[h]) + _dot(attn_ref[h, rows, :], v_new[h]) for h in heads]
        for h in heads:
            o = out[h]
            o = o * lax.rsqrt(jnp.mean(o * o, axis=-1, keepdims=True) + NORM_EPS) * ng_ref[...]
            o_ref[rows, lanes[h]] = (o * _silu(gate_ref[rows, lanes[h]].astype(F32))).astype(o_ref.dtype)
    for h in heads:
        s_ref[h] = state[h]


def _gdn(zm, zs, conv_w, a_log, dt_bias, norm_g, prep_chunks, scan_chunks):
    t = zm.shape[0]
    c = GDN_CHUNK
    par = jnp.zeros((8, SMALL_COLS), F32)
    par = par.at[0, S_A:S_A + HEADS].set(a_log).at[1, S_A:S_A + HEADS].set(dt_bias)

    rp = prep_chunks * c
    blk = lambda col: pl.BlockSpec((rp, BRANCH_WIDTH), lambda n: (n, col // BRANCH_WIDTH))
    wide = pl.BlockSpec((rp, BRANCH_WIDTH), lambda n: (n, 0))
    u, w, qe, kd, attn, eg = pl.pallas_call(
        functools.partial(_gdn_prep_kernel, n_ch=prep_chunks),
        out_shape=(jax.ShapeDtypeStruct((t, BRANCH_WIDTH), F32),
                   jax.ShapeDtypeStruct((t, BRANCH_WIDTH), BF16),
                   jax.ShapeDtypeStruct((t, BRANCH_WIDTH), BF16),
                   jax.ShapeDtypeStruct((t, BRANCH_WIDTH), BF16),
                   jax.ShapeDtypeStruct((HEADS, t, c), BF16),
                   jax.ShapeDtypeStruct((t // c, 8, BRANCH_WIDTH), F32)),
        grid=(t // rp,),
        in_specs=[blk(C_Q), blk(C_K), blk(C_V),
                  pl.BlockSpec((rp, SMALL_COLS), lambda n: (n, 0)),
                  pl.BlockSpec((GDN_CONV, 3 * BRANCH_WIDTH), lambda n: (0, 0)),
                  pl.BlockSpec((8, SMALL_COLS), lambda n: (0, 0))],
        out_specs=(wide, wide, wide, wide,
                   pl.BlockSpec((HEADS, rp, c), lambda n: (0, n, 0)),
                   pl.BlockSpec((prep_chunks, 8, BRANCH_WIDTH), lambda n: (n, 0, 0))),
        scratch_shapes=[pltpu.VMEM((rp + 8, 3 * BRANCH_WIDTH), F32)],
        compiler_params=_cparams(1),
        name="gdn_chunk_prep",
    )(zm, zm, zm, zs, conv_w, par)

    rs = scan_chunks * c
    wide = pl.BlockSpec((rs, BRANCH_WIDTH), lambda n: (n, 0))
    return pl.pallas_call(
        functools.partial(_gdn_scan_kernel, n_ch=scan_chunks),
        out_shape=jax.ShapeDtypeStruct((t, BRANCH_WIDTH), BF16),
        grid=(t // rs,),
        in_specs=[wide, wide, wide, wide,
                  pl.BlockSpec((HEADS, rs, c), lambda n: (0, n, 0)),
                  pl.BlockSpec((scan_chunks, 8, BRANCH_WIDTH), lambda n: (n, 0, 0)),
                  pl.BlockSpec((rs, BRANCH_WIDTH), lambda n: (n, C_G // BRANCH_WIDTH)),
                  pl.BlockSpec((1, HEAD_DIM), lambda n: (0, 0))],
        out_specs=wide,
        scratch_shapes=[pltpu.VMEM((HEADS, HEAD_DIM, HEAD_DIM), F32)],
        compiler_params=_cparams(1),
        name="gdn_state_scan",
    )(u, w, qe, kd, attn, eg, zm, norm_g.reshape(1, -1))


def _ml_kernel(q_ref, k_ref, v_ref, og_ref, zs_ref, par_ref, ng_ref, o_ref,
               c_ref, n_ref, m_ref, *, n_ch):
    c = ML_CHUNK

    @pl.when(pl.program_id(0) == 0)
    def _():
        c_ref[...] = jnp.zeros(c_ref.shape, F32)
        n_ref[...] = jnp.zeros(n_ref.shape, F32)
        m_ref[...] = jnp.zeros(m_ref.shape, F32)

    zz_all = zs_ref[...] + par_ref[2:3, :]
    logf_all = -_softplus(-zz_all)

    ii = lax.broadcasted_iota(jnp.int32, (c, c), 0)
    jj = lax.broadcasted_iota(jnp.int32, (c, c), 1)
    tri_incl = jnp.where(ii >= jj, 1.0, 0.0).astype(BF16)
    ones_cc = jnp.ones((c, c), BF16)
    ones_cl = jnp.ones((c, HEAD_DIM), BF16)

    pairs = []
    for ch in range(n_ch):
        rows = slice(ch * c, (ch + 1) * c)
        for h in range(HEADS):
            lanes = slice(h * HEAD_DIM, (h + 1) * HEAD_DIM)
            qk_lanes = slice(h * ML_QK, (h + 1) * ML_QK)
            pairs.append(dict(
                ch=ch, h=h, rows=rows, lanes=lanes,
                qs=(q_ref[rows, qk_lanes].astype(F32) * ML_QK ** -0.5).astype(BF16),
                kh=k_ref[rows, qk_lanes], vh=v_ref[rows, lanes],
                i_b=jnp.broadcast_to(zz_all[rows, S_I + h:S_I + h + 1], (c, HEAD_DIM)),
                lf_b=jnp.broadcast_to(logf_all[rows, S_F + h:S_F + h + 1], (c, HEAD_DIM))))
    for s in pairs:
        s["bcol"] = _dot_sel_lhs(tri_incl, s["lf_b"])
        s["qk"] = _dot_nt(s["qs"], s["kh"])
    for s in pairs:
        rv = s["i_b"] - s["bcol"]
        s["rrow"] = _dot_sel_lhs(ones_cc, jnp.where(ii == jj, rv[:, :c], 0.0))
        s["b_last"] = s["bcol"][c - 1:c, :]
        s["log_kv"] = s["b_last"] - s["bcol"] + s["i_b"]
        s["kv_max"] = jnp.max(s["log_kv"], axis=0, keepdims=True)
    for s in pairs:
        s["log_intra"] = jnp.where(ii >= jj, s["bcol"][:, :c] + s["rrow"], -jnp.inf)
        s["intra_max"] = jnp.max(s["log_intra"], axis=1, keepdims=True)

    heads = range(HEADS)
    c_mat = [c_ref[h] for h in heads]
    n_b = [n_ref[h] for h in heads]
    m_prev = [m_ref[h][0:1, :] for h in heads]
    for ch in range(n_ch):
        cur = pairs[ch * HEADS:(ch + 1) * HEADS]
        m_out = [jnp.maximum(s["bcol"] + m_prev[s["h"]], s["intra_max"]) for s in cur]
        w_inter = [jnp.exp(s["bcol"] + m_prev[s["h"]] - m_out[s["h"]]) for s in cur]
        sc = [jnp.exp(s["log_intra"] - m_out[s["h"]][:, :c]) * s["qk"] for s in cur]
        num = [w_inter[h] * _dot(cur[h]["qs"], c_mat[h].astype(BF16)) + _dot(sc[h].astype(BF16), cur[h]["vh"])
               for h in heads]
        den = [w_inter[h] * _dot(cur[h]["qs"], n_b[h].astype(BF16)) + jnp.sum(sc[h], axis=1, keepdims=True)
               for h in heads]
        m_new = [jnp.maximum(s["b_last"] + m_prev[s["h"]], s["kv_max"]) for s in cur]
        kw = [(s["kh"].astype(F32) * jnp.exp(s["log_kv"] - m_new[s["h"]])[:, :ML_QK]).astype(BF16) for s in cur]
        decay = [jnp.exp(s["b_last"] + m_prev[s["h"]] - m_new[s["h"]]) for s in cur]
        c_mat = [decay[h] * c_mat[h] + _dot_tn(kw[h], cur[h]["vh"]) for h in heads]
        n_b = [decay[h] * n_b[h] + _dot_tn(kw[h], ones_cl) for h in heads]
        m_prev = m_new
        for h in heads:
            s = cur[h]
            hid = num[h] / jnp.maximum(jnp.abs(den[h]), jnp.exp(-m_out[h]))
            hid = hid * lax.rsqrt(jnp.mean(hid * hid, axis=-1, keepdims=True) + NORM_EPS)
            hid = hid * ng_ref[:, s["lanes"]]
            gate = _sigmoid(og_ref[s["rows"], s["lanes"]].astype(F32))
            o_ref[s["rows"], s["lanes"]] = (hid * gate).astype(o_ref.dtype)
    for h in heads:
        c_ref[h] = c_mat[h]
        n_ref[h] = n_b[h]
        m_ref[h] = jnp.broadcast_to(m_prev[h], (8, HEAD_DIM))


def _mlstm(zm, zs, i_bias, f_bias, norm_g, n_ch):
    t = zm.shape[0]
    c = ML_CHUNK * n_ch
    qk_w = HEADS * ML_QK
    par = jnp.zeros((8, SMALL_COLS), F32)
    par = par.at[2, S_I:S_I + HEADS].set(i_bias).at[2, S_F:S_F + HEADS].set(f_bias)
    return pl.pallas_call(
        functools.partial(_ml_kernel, n_ch=n_ch),
        out_shape=jax.ShapeDtypeStruct((t, BRANCH_WIDTH), BF16),
        grid=(t // c,),
        in_specs=[pl.BlockSpec((c, qk_w), lambda n: (n, D_Q // qk_w)),
                  pl.BlockSpec((c, qk_w), lambda n: (n, D_K // qk_w)),
                  pl.BlockSpec((c, BRANCH_WIDTH), lambda n: (n, D_V // BRANCH_WIDTH)),
                  pl.BlockSpec((c, BRANCH_WIDTH), lambda n: (n, D_O // BRANCH_WIDTH)),
                  pl.BlockSpec((c, SMALL_COLS), lambda n: (n, 0)),
                  pl.BlockSpec((8, SMALL_COLS), lambda n: (0, 0)),
                  pl.BlockSpec((1, BRANCH_WIDTH), lambda n: (0, 0))],
        out_specs=pl.BlockSpec((c, BRANCH_WIDTH), lambda n: (n, 0)),
        scratch_shapes=[pltpu.VMEM((HEADS, ML_QK, HEAD_DIM), F32),
                        pltpu.VMEM((HEADS, ML_QK, HEAD_DIM), F32),
                        pltpu.VMEM((HEADS, 8, HEAD_DIM), F32)],
        compiler_params=_cparams(1),
        name="mlstm",
    )(zm, zm, zm, zm, zs, par, norm_g.reshape(1, -1))


def _merge_kernel(a_ref, b_ref, c_ref, d_ref, g0_ref, g1_ref, g2_ref, g3_ref, wb_ref, o_ref):
    acc = None
    for g, (br, gr) in enumerate(((a_ref, g0_ref), (b_ref, g1_ref), (c_ref, g2_ref), (d_ref, g3_ref))):
        term = _sigmoid(gr[...].astype(F32)) * _dot(br[...], wb_ref[g])
        acc = term if acc is None else acc + term
    o_ref[...] = acc.astype(o_ref.dtype)


def _merge(outs, zm, wb16, tm):
    t = zm.shape[0]
    branch = pl.BlockSpec((tm, BRANCH_WIDTH), lambda i: (i, 0))
    gate = lambda g: pl.BlockSpec((tm, D_MODEL), lambda i: (i, g))
    return pl.pallas_call(
        _merge_kernel,
        out_shape=jax.ShapeDtypeStruct((t, D_MODEL), BF16),
        grid=(t // tm,),
        in_specs=[branch] * 4 + [gate(0), gate(1), gate(2), gate(3),
                                 pl.BlockSpec((N_BRANCHES, BRANCH_WIDTH, D_MODEL), lambda i: (0, 0, 0))],
        out_specs=pl.BlockSpec((tm, D_MODEL), lambda i: (i, 0)),
        compiler_params=_cparams(1),
        name="merge_branches",
    )(*outs, zm, zm, zm, zm, wb16)


def _proj_ln_kernel(a_ref, w_ref, x_ref, g_ref, b_ref, o_ref):
    y = ALPHA * x_ref[...] + _dot(a_ref[...], w_ref[...])
    o_ref[...] = _layer_norm(y, g_ref[...], b_ref[...])


def _proj_ln(a, w16, x, g, b, tm):
    t, k = a.shape
    return pl.pallas_call(
        _proj_ln_kernel,
        out_shape=jax.ShapeDtypeStruct((t, D_MODEL), F32),
        grid=(t // tm,),
        in_specs=[pl.BlockSpec((tm, k), lambda i: (i, 0)),
                  pl.BlockSpec((k, D_MODEL), lambda i: (0, 0)),
                  pl.BlockSpec((tm, D_MODEL), lambda i: (i, 0)),
                  pl.BlockSpec((1, D_MODEL), lambda i: (0, 0)),
                  pl.BlockSpec((1, D_MODEL), lambda i: (0, 0))],
        out_specs=pl.BlockSpec((tm, D_MODEL), lambda i: (i, 0)),
        compiler_params=_cparams(1),
        name="out_proj_layernorm",
    )(a, w16, x, g.reshape(1, -1), b.reshape(1, -1))


def _router_kernel(x_ref, wr_ref, bias_ref, idx_ref, w_ref, *, tm):
    logits = _dot_f32_nt(wr_ref[...], x_ref[...])
    scores = _sigmoid(logits)
    biased = scores + bias_ref[...]
    neg = -jnp.inf

    b3 = biased.reshape(N_GROUPS, GROUP_SIZE, tm)
    pos = lax.broadcasted_iota(jnp.int32, (N_GROUPS, GROUP_SIZE, tm), 1)
    m1 = jnp.max(b3, axis=1, keepdims=True)
    first = jnp.min(jnp.where(b3 == m1, pos, GROUP_SIZE), axis=1, keepdims=True)
    m2 = jnp.max(jnp.where(pos == first, neg, b3), axis=1, keepdims=True)
    gscore = m1 + m2

    gidx = lax.broadcasted_iota(jnp.int32, (N_GROUPS, 1, tm), 0)
    gsel = jnp.zeros((N_GROUPS, 1, tm), F32)
    for _ in range(TOPK_GROUPS):
        gm = jnp.max(gscore, axis=0, keepdims=True)
        gfirst = jnp.min(jnp.where(gscore == gm, gidx, N_GROUPS), axis=0, keepdims=True)
        hit = gidx == gfirst
        gsel = jnp.where(hit, 1.0, gsel)
        gscore = jnp.where(hit, neg, gscore)

    allowed = jnp.broadcast_to(gsel, (N_GROUPS, GROUP_SIZE, tm)) > 0.0
    masked = jnp.where(allowed, b3, neg).reshape(N_EXPERTS, tm)
    eidx = lax.broadcasted_iota(jnp.int32, (N_EXPERTS, tm), 0)
    idx_rows, sel_rows = [], []
    for _ in range(TOP_K):
        em = jnp.max(masked, axis=0, keepdims=True)
        efirst = jnp.min(jnp.where(masked == em, eidx, N_EXPERTS), axis=0, keepdims=True)
        hit = eidx == efirst
        idx_rows.append(efirst)
        sel_rows.append(jnp.sum(jnp.where(hit, scores, 0.0), axis=0, keepdims=True))
        masked = jnp.where(hit, neg, masked)
    total = sel_rows[0]
    for r in sel_rows[1:]:
        total = total + r
    zero_i = jnp.zeros((1, tm), jnp.int32)
    zero_f = jnp.zeros((1, tm), F32)
    idx_ref[...] = jnp.concatenate(idx_rows + [zero_i, zero_i], axis=0)
    w_ref[...] = jnp.concatenate([r / total * ROUTED_SCALE for r in sel_rows] + [zero_f, zero_f], axis=0)


def _dot_f32_nt(a, b):
    ah, al = _split2(a)
    bh, bl = _split2(b)
    return _dot_nt(ah, bh) + _dot_nt(ah, bl) + _dot_nt(al, bh)


def _router(x, w_router, router_bias, tm):
    t = x.shape[0]
    return pl.pallas_call(
        functools.partial(_router_kernel, tm=tm),
        out_shape=(jax.ShapeDtypeStruct((8, t), jnp.int32), jax.ShapeDtypeStruct((8, t), F32)),
        grid=(t // tm,),
        in_specs=[pl.BlockSpec((tm, D_MODEL), lambda i: (i, 0)),
                  pl.BlockSpec((N_EXPERTS, D_MODEL), lambda i: (0, 0)),
                  pl.BlockSpec((N_EXPERTS, 1), lambda i: (0, 0))],
        out_specs=(pl.BlockSpec((8, tm), lambda i: (0, i)), pl.BlockSpec((8, tm), lambda i: (0, i))),
        compiler_params=_cparams(1),
        name="router_topk",
    )(x, w_router.T, router_bias.reshape(N_EXPERTS, 1))


def _dispatch_kernel(pos_ref, x_ref, xs_hbm, sem, *, td):
    for r in range(td):
        for k in range(TOP_K):
            pltpu.make_async_copy(x_ref.at[pl.ds(r, 1), :],
                                  xs_hbm.at[pl.ds(pos_ref[0, r * TOP_K + k], 1), :], sem).start()
    for _ in range(TOP_K):
        pltpu.make_async_copy(x_ref, xs_hbm.at[pl.ds(0, td), :], sem).wait()


def _dispatch(x, pos, td):
    t = x.shape[0]
    return pl.pallas_call(
        functools.partial(_dispatch_kernel, td=td),
        out_shape=jax.ShapeDtypeStruct((t * TOP_K, D_MODEL), F32),
        grid=(t // td,),
        in_specs=[pl.BlockSpec((None, 1, td * TOP_K), lambda i: (i, 0, 0), memory_space=pltpu.SMEM),
                  pl.BlockSpec((td, D_MODEL), lambda i: (i, 0))],
        out_specs=pl.BlockSpec(memory_space=pl.ANY),
        scratch_shapes=[pltpu.SemaphoreType.DMA],
        compiler_params=_cparams(1),
        name="moe_dispatch",
    )(pos.reshape(t // td, 1, td * TOP_K), x)


def _expert_kernel(n_ref, tile_ref, exp_ref, nxt_ref, par_ref, lo_ref, hi_ref,
                   x_ref, wg_hbm, wu_hbm, wd_hbm, y_ref,
                   wgbuf, wubuf, wdbuf, wg16, wu16, wd16, wsem, *, tm, layer):
    j = pl.program_id(0)
    prev = jnp.maximum(j - 1, 0)
    first = (j == 0) | (tile_ref[prev] != tile_ref[j])
    expert = exp_ref[j]
    slot = par_ref[j]

    def copies(ex, s):
        return [pltpu.make_async_copy(hbm.at[layer, ex], buf.at[s], wsem.at[s])
                for hbm, buf in ((wg_hbm, wgbuf), (wu_hbm, wubuf), (wd_hbm, wdbuf))]

    @pl.when(j < n_ref[0])
    def _():
        @pl.when((j == 0) | (exp_ref[prev] != expert))
        def _():
            @pl.when(j == 0)
            def _():
                for cp in copies(expert, slot):
                    cp.start()

            @pl.when(nxt_ref[j] != expert)
            def _():
                for cp in copies(nxt_ref[j], 1 - slot):
                    cp.start()

            for cp in copies(expert, slot):
                cp.wait()
            wg16[...] = wgbuf[slot].astype(BF16)
            wu16[...] = wubuf[slot].astype(BF16)
            wd16[...] = wdbuf[slot].astype(BF16)

        x = x_ref[...].astype(BF16)
        hidden = _silu(_dot(x, wg16[...])) * _dot(x, wu16[...])
        y = _dot(hidden.astype(BF16), wd16[...])

        @pl.when(first)
        def _():
            y_ref[...] = y

        @pl.when(jnp.logical_not(first))
        def _():
            row = lax.broadcasted_iota(jnp.int32, (tm, 1), 0)
            mine = (row >= lo_ref[j]) & (row < hi_ref[j])
            y_ref[...] = jnp.where(mine, y, y_ref[...])


def _experts(xs, sched, w_gate, w_up, w_down, layer, tm):
    n_max = sched[1].shape[0]
    tile_of = lambda j, n, tl, *rest: tl[j]
    hbm = pl.BlockSpec(memory_space=pl.ANY)
    up_shape, down_shape = (D_MODEL, EXPERT_DIM), (EXPERT_DIM, D_MODEL)
    return pl.pallas_call(
        functools.partial(_expert_kernel, tm=tm, layer=layer),
        out_shape=jax.ShapeDtypeStruct(xs.shape, F32),
        grid_spec=pltpu.PrefetchScalarGridSpec(
            num_scalar_prefetch=len(sched),
            grid=(n_max,),
            in_specs=[pl.BlockSpec((tm, D_MODEL), lambda *a: (tile_of(*a), 0)), hbm, hbm, hbm],
            out_specs=pl.BlockSpec((tm, D_MODEL), lambda *a: (tile_of(*a), 0)),
            scratch_shapes=[pltpu.VMEM((2,) + up_shape, F32), pltpu.VMEM((2,) + up_shape, F32),
                            pltpu.VMEM((2,) + down_shape, F32),
                            pltpu.VMEM(up_shape, BF16), pltpu.VMEM(up_shape, BF16),
                            pltpu.VMEM(down_shape, BF16),
                            pltpu.SemaphoreType.DMA((2,))]),
        compiler_params=_cparams(1),
        name="routed_experts",
    )(*sched, xs, w_gate, w_up, w_down)


def _expert_schedule(idx, t, tm):
    i32 = jnp.int32
    n_rows = t * TOP_K
    n_tiles = n_rows // tm
    n_max = n_tiles + N_EXPERTS - 1
    flat_e = idx.reshape(-1)
    onehot = (flat_e[:, None] == jnp.arange(N_EXPERTS, dtype=i32)[None, :]).astype(i32)
    seen = jnp.cumsum(onehot, axis=0)
    counts = seen[-1]
    end = jnp.cumsum(counts)
    start = end - counts
    pos = jnp.sum(onehot * (seen - 1 + start[None, :]), axis=1).astype(i32)
    first_tile = start // tm
    items_per_e = jnp.where(counts > 0, (end - 1) // tm - first_tile + 1, 0)
    item_end = jnp.cumsum(items_per_e)
    n_items = item_end[-1]
    j = jnp.arange(n_max, dtype=i32)
    e = jnp.minimum(jnp.sum((item_end[None, :] <= j[:, None]).astype(i32), axis=1), N_EXPERTS - 1)
    tile = first_tile[e] + j - (item_end[e] - items_per_e[e])
    lo = jnp.maximum(start[e], tile * tm) - tile * tm
    hi = jnp.minimum(end[e], (tile + 1) * tm) - tile * tm
    valid = j < n_items
    last = jnp.maximum(n_items - 1, 0)
    pick = lambda a: jnp.where(valid, a, a[last]).astype(i32)
    ids = jnp.arange(N_EXPERTS, dtype=i32)
    later = (ids[None, :] > ids[:, None]) & (counts[None, :] > 0)
    nxt = jnp.min(jnp.where(later, ids[None, :], N_EXPERTS), axis=1)
    nxt = jnp.where(nxt == N_EXPERTS, ids, nxt)
    parity = (jnp.cumsum((counts > 0).astype(i32)) - 1) % 2
    return pos, (n_items.reshape(1).astype(i32), pick(tile), pick(e), pick(nxt[e]), pick(parity[e]),
                 pick(lo), pick(hi))


def _combine_kernel(pos_ref, pos_next_ref, x_ref, w_ref, ys_hbm, sg_ref, su_ref, sd_ref, g_ref, b_ref,
                    o_ref, o16_ref, ybuf, sem, *, tt, n_steps):
    i = pl.program_id(0)
    slot = lax.rem(i, 2)

    def gather(idx_ref, s):
        for r in range(tt):
            for k in range(TOP_K):
                pltpu.make_async_copy(ys_hbm.at[pl.ds(idx_ref[0, r * TOP_K + k], 1), :],
                                      ybuf.at[s, k, pl.ds(r, 1), :], sem.at[s]).start()

    @pl.when(i == 0)
    def _():
        gather(pos_ref, 0)

    @pl.when(i + 1 < n_steps)
    def _():
        gather(pos_next_ref, 1 - slot)

    x = x_ref[...]
    x16 = x.astype(BF16)
    hidden = _silu(_dot(x16, sg_ref[...])) * _dot(x16, su_ref[...])
    acc = _dot(hidden.astype(BF16), sd_ref[...])
    for k in range(TOP_K):
        pltpu.make_async_copy(ys_hbm.at[pl.ds(0, tt), :], ybuf.at[slot, k], sem.at[slot]).wait()
    w = w_ref[...]
    for k in range(TOP_K):
        acc = acc + ybuf[slot, k] * w[:, k:k + 1]
    out = _layer_norm(ALPHA * x + acc, g_ref[...], b_ref[...])
    o_ref[...] = out
    o16_ref[...] = out.astype(BF16)


def _combine(x, w_tok, pos, ys, sg16, su16, sd16, g, b, tt):
    t = x.shape[0]
    nb = t // tt
    pos3 = pos.reshape(nb, 1, tt * TOP_K)
    tile = pl.BlockSpec((tt, D_MODEL), lambda i: (i, 0))
    vec = pl.BlockSpec((1, D_MODEL), lambda i: (0, 0))
    idx_spec = lambda f: pl.BlockSpec((None, 1, tt * TOP_K), f, memory_space=pltpu.SMEM)
    return pl.pallas_call(
        functools.partial(_combine_kernel, tt=tt, n_steps=nb),
        out_shape=(jax.ShapeDtypeStruct((t, D_MODEL), F32), jax.ShapeDtypeStruct((t, D_MODEL), BF16)),
        grid=(nb,),
        in_specs=[idx_spec(lambda i: (i, 0, 0)), idx_spec(lambda i: (jnp.minimum(i + 1, nb - 1), 0, 0)),
                  tile, pl.BlockSpec((tt, 8), lambda i: (i, 0)),
                  pl.BlockSpec(memory_space=pl.ANY),
                  pl.BlockSpec((D_MODEL, SHARED_DIM), lambda i: (0, 0)),
                  pl.BlockSpec((D_MODEL, SHARED_DIM), lambda i: (0, 0)),
                  pl.BlockSpec((SHARED_DIM, D_MODEL), lambda i: (0, 0)), vec, vec],
        out_specs=(tile, tile),
        scratch_shapes=[pltpu.VMEM((2, TOP_K, tt, D_MODEL), F32), pltpu.SemaphoreType.DMA((2,))],
        compiler_params=_cparams(1),
        name="moe_combine_layernorm",
    )(pos3, pos3, x, w_tok, ys, sg16, su16, sd16, g.reshape(1, -1), b.reshape(1, -1))


def _tile(t, pref):
    return min(t, pref)


def _in_proj(x16, w_in_all, layer):
    t = x16.shape[0]
    tm = _tile(t, 1024)
    wide = lambda r, tn, name: _matmul_cols(x16, w_in_all, layer, r[0], r[1] - r[0], BF16, tm, tn, name)
    z_abc = wide(ABC_COLS, 768, "in_proj_abc")
    z_d = wide(D_COLS, 768, "in_proj_d")
    z_gate = wide(GATE_COLS, 1024, "in_proj_gate")
    w_in = w_in_all[layer]
    pad = jnp.zeros((D_MODEL, SMALL_COLS - sum(b - a for a, b in SMALL_SRC)), F32)
    w_small = jnp.concatenate([w_in[:, a:b] for a, b in SMALL_SRC] + [pad], axis=1).astype(BF16)
    zs = _matmul(x16, w_small, F32, tm, SMALL_COLS, "in_proj_small")
    return z_abc, z_d, z_gate, zs


def _mixer(x, x16, p):
    t = x.shape[0]
    zm, z_d, z_gate, zs = _in_proj(x16, p["w_in_all"], p["layer"])
    out_a = _sb_attention(zm, 256, 256, 8)
    out_b = _gmlp(zm, p["gm_norm_g"], p["gm_norm_b"], p["gm_w_s"], p["gm_b_s"], _tile(t, 512))
    out_c = _gdn(zm, zs, p["gdn_conv_w"], p["gdn_a_log"], p["gdn_dt_bias"], p["gdn_norm_g"], 4, 4)
    out_d = _mlstm(z_d, zs, p["ml_i_bias"], p["ml_f_bias"], p["ml_norm_g"], 2)
    merged = _merge((out_a, out_b, out_c, out_d), z_gate, p["w_branch"].astype(BF16), _tile(t, 256))
    return _proj_ln(merged, p["w_out"].astype(BF16), x, p["ln1_g"], p["ln1_b"], _tile(t, 256))


def _moe(x, p, tm):
    t = x.shape[0]
    idx_t, w_t = _router(x, p["w_router"], p["router_bias"], _tile(t, 512))
    idx = idx_t[:TOP_K].T
    w_tok = w_t.T
    pos, sched = _expert_schedule(idx, t, tm)
    xs = _dispatch(x, pos, _tile(t, 256))
    ys = _experts(xs, sched, p["w_gate_all"], p["w_up_all"], p["w_down_all"], p["layer"], tm)
    return _combine(x, w_tok, pos, ys, p["ws_gate"].astype(BF16), p["ws_up"].astype(BF16),
                    p["ws_down"].astype(BF16), p["ln2_g"], p["ln2_b"], _tile(t, 128))


def _layer(x, x16, p, tm_expert):
    x1 = _mixer(x, x16, p)
    return _moe(x1, p, tm_expert)


_STACKED = ("w_in", "w_gate", "w_up", "w_down")


def _layer_params(params, l):
    p = {k: v[l] for k, v in params.items() if k not in _STACKED}
    p.update({k + "_all": params[k] for k in _STACKED})
    p["layer"] = l
    return p


def kernel(x, w_in, gm_norm_g, gm_norm_b, gm_w_s, gm_b_s, gdn_conv_w, gdn_a_log, gdn_dt_bias, gdn_norm_g, ml_i_bias, ml_f_bias, ml_norm_g, w_branch, w_out, ln1_g, ln1_b, w_router, router_bias, w_gate, w_up, w_down, ws_gate, ws_up, ws_down, ln2_g, ln2_b):
    params = dict(w_in=w_in, gm_norm_g=gm_norm_g, gm_norm_b=gm_norm_b, gm_w_s=gm_w_s, gm_b_s=gm_b_s,
                  gdn_conv_w=gdn_conv_w, gdn_a_log=gdn_a_log, gdn_dt_bias=gdn_dt_bias,
                  gdn_norm_g=gdn_norm_g, ml_i_bias=ml_i_bias, ml_f_bias=ml_f_bias, ml_norm_g=ml_norm_g,
                  w_branch=w_branch, w_out=w_out, ln1_g=ln1_g, ln1_b=ln1_b, w_router=w_router,
                  router_bias=router_bias, w_gate=w_gate, w_up=w_up, w_down=w_down, ws_gate=ws_gate,
                  ws_up=ws_up, ws_down=ws_down, ln2_g=ln2_g, ln2_b=ln2_b)
    b, t, d = x.shape
    h = x.reshape(b * t, d)
    h16 = h.astype(BF16)
    for l in range(DEPTH):
        h, h16 = _layer(h, h16, _layer_params(params, l), 256)
    return h.reshape(b, t, d)
```

```python
import functools

import jax
import jax.numpy as jnp
from jax import lax
from jax.experimental import pallas as pl
from jax.experimental.pallas import tpu as pltpu

F32 = jnp.float32
BF16 = jnp.bfloat16

D_MODEL = 2048
N_BRANCHES = 4
BRANCH_WIDTH = 512
HEADS = 4
HEAD_DIM = 128
GM_CHUNK = 128
GDN_CHUNK = 64
GDN_CONV = 4
ML_CHUNK = 64
ML_QK = 64
N_EXPERTS = 64
N_GROUPS = 8
GROUP_SIZE = 8
TOPK_GROUPS = 4
TOP_K = 6
EXPERT_DIM = 512
SHARED_DIM = 512
ROUTED_SCALE = 2.5
DEPTH = 2
ALPHA = (2 * DEPTH) ** 0.25
NORM_EPS = 1e-5
LOG2E = 1.4426950408889634

ABC_COLS = (0, 4608)
A_Q, A_K, A_V = 0, 512, 1024
B_U, B_V = 1536, 2048
C_Q, C_K, C_V, C_G = 2560, 3072, 3584, 4096
D_COLS = (4616, 6152)
D_Q, D_K, D_V, D_O = 0, 256, 512, 1024
GATE_COLS = (6160, 6160 + N_BRANCHES * D_MODEL)
SMALL_SRC = ((4608, 4616), (6152, 6160))
S_BETA, S_A, S_I, S_F = 0, 4, 8, 12
SMALL_COLS = 128

VMEM_LIMIT = 56 * 1024 * 1024


def _cparams(n_axes):
    return pltpu.CompilerParams(
        dimension_semantics=("arbitrary",) * n_axes, vmem_limit_bytes=VMEM_LIMIT)


def _dot(a, b):
    return jnp.dot(a, b, preferred_element_type=F32)


def _dot_nt(a, b):
    return lax.dot_general(a, b, (((1,), (1,)), ((), ())), preferred_element_type=F32)


def _dot_tn(a, b):
    return lax.dot_general(a, b, (((0,), (0,)), ((), ())), preferred_element_type=F32)


def _split2(a):
    hi = a.astype(BF16)
    lo = (a - hi.astype(F32)).astype(BF16)
    return hi, lo


def _split3(a):
    hi = a.astype(BF16)
    r = a - hi.astype(F32)
    mid = r.astype(BF16)
    lo = (r - mid.astype(F32)).astype(BF16)
    return hi, mid, lo


def _dot_sel_lhs(sel, a):
    sel3 = jnp.concatenate([sel, sel, sel], axis=1)
    return _dot(sel3, jnp.concatenate(_split3(a), axis=0))


def _dot_sel_rhs(a, sel):
    hi, lo = _split2(a)
    return _dot(hi, sel) + _dot(lo, sel)


def _dot_f32(a, b):
    ah, al = _split2(a)
    bh, bl = _split2(b)
    return _dot(jnp.concatenate([ah, ah, al], axis=1), jnp.concatenate([bh, bl, bh], axis=0))


def _softplus(x):
    return jnp.maximum(x, 0.0) + jnp.log1p(jnp.exp(-jnp.abs(x)))


def _sigmoid(x):
    return 1.0 / (1.0 + jnp.exp(-x))


def _silu(x):
    return x * _sigmoid(x)


def _layer_norm(x, g, b):
    mu = jnp.mean(x, axis=-1, keepdims=True)
    xc = x - mu
    var = jnp.mean(xc * xc, axis=-1, keepdims=True)
    return xc * lax.rsqrt(var + NORM_EPS) * g + b


def _mm_kernel(x_ref, w_ref, o_ref):
    o_ref[...] = _dot(x_ref[...], w_ref[...]).astype(o_ref.dtype)


def _matmul(x, w, out_dtype, tm, tn, name):
    m, k = x.shape
    n = w.shape[1]
    return pl.pallas_call(
        _mm_kernel,
        out_shape=jax.ShapeDtypeStruct((m, n), out_dtype),
        grid=(n // tn, m // tm),
        in_specs=[pl.BlockSpec((tm, k), lambda j, i: (i, 0)),
                  pl.BlockSpec((k, tn), lambda j, i: (0, j))],
        out_specs=pl.BlockSpec((tm, tn), lambda j, i: (i, j)),
        compiler_params=_cparams(2),
        name=name,
    )(x, w)


def _mm_nt_kernel(x_ref, wt_ref, o_ref):
    o_ref[...] = _dot_nt(x_ref[...], wt_ref[...]).astype(o_ref.dtype)


def _matmul_nt(x, wt, out_dtype, tm, tn, name):
    m, k = x.shape
    n = wt.shape[0]
    return pl.pallas_call(
        _mm_nt_kernel,
        out_shape=jax.ShapeDtypeStruct((m, n), out_dtype),
        grid=(n // tn, m // tm),
        in_specs=[pl.BlockSpec((tm, k), lambda j, i: (i, 0)),
                  pl.BlockSpec((tn, k), lambda j, i: (j, 0))],
        out_specs=pl.BlockSpec((tm, tn), lambda j, i: (i, j)),
        compiler_params=_cparams(2),
        name=name,
    )(x, wt)


def _sb_kernel(q_ref, k_ref, v_ref, o_ref, *, tq, tk, per_trip):
    i = pl.program_id(1)
    q = (q_ref[...].astype(F32) * (HEAD_DIM ** -0.5 * LOG2E)).astype(BF16)
    half = tk // 2
    row = i * tq + lax.broadcasted_iota(jnp.int32, (tq, tk), 0)
    col0 = lax.broadcasted_iota(jnp.int32, (tq, tk), 1)
    jj = lax.broadcasted_iota(jnp.int32, (tk, tk), 0)
    ss = lax.broadcasted_iota(jnp.int32, (tk, tk), 1)
    sel = jnp.where((jj >= ss) & ((jj < half) == (ss < half)), 1.0, 0.0).astype(BF16)

    def blocks(starts, carry, on_diagonal):
        zs = [_dot_nt(q, k_ref[pl.ds(ks, tk), :]) for ks in starts]
        sps = [jnp.maximum(z, 0.0) + jnp.log(1.0 + jnp.exp2(-jnp.abs(z))) * LOG2E for z in zs]
        if on_diagonal:
            masks = [(col0 + ks) < row for ks in starts]
            sps = [jnp.where(m, sp, 0.0) for m, sp in zip(masks, sps)]
        tails = [_dot(sp.astype(BF16), sel) for sp in sps]
        bases = [z - tail for z, tail in zip(zs, tails)]
        late_total = [jnp.broadcast_to(tail[:, half:half + 1], (tq, half)) for tail in tails]
        early_total = [jnp.broadcast_to(tail[:, 0:1], (tq, half)) for tail in tails]
        acc, run = carry
        for n, ks in enumerate(starts):
            run_mid = run + late_total[n]
            a = jnp.exp2(jnp.concatenate([bases[n][:, :half] - run_mid, bases[n][:, half:] - run], axis=1))
            if on_diagonal:
                a = jnp.where(masks[n], a, 0.0)
            acc = acc + _dot(a.astype(BF16), v_ref[pl.ds(ks, tk), :])
            run = run_mid + early_total[n]
        return acc, run

    carry = blocks([pl.multiple_of(i * tq, tq)],
                   (jnp.zeros((tq, HEAD_DIM), F32), jnp.zeros((tq, half), F32)), True)

    def run_groups(size, top, trips, carry):
        def group(step, carry):
            starts = [pl.multiple_of((top - 1 - size * step - s) * tk, tk) for s in range(size)]
            return blocks(starts, carry, False)
        return lax.fori_loop(0, trips, group, carry)

    carry = run_groups(per_trip, i, i // per_trip, carry)
    left = i % per_trip
    size = per_trip // 2
    while size >= 1:
        carry = run_groups(size, left, left // size, carry)
        left = left % size
        size //= 2
    acc, _ = carry
    o_ref[...] = acc.astype(o_ref.dtype)


def _sb_attention(zm, tq, tk, per_trip):
    t = zm.shape[0]
    assert tq == tk and tk % 256 == 0
    cq, ck, cv = A_Q // HEAD_DIM, A_K // HEAD_DIM, A_V // HEAD_DIM
    return pl.pallas_call(
        functools.partial(_sb_kernel, tq=tq, tk=tk, per_trip=per_trip),
        out_shape=jax.ShapeDtypeStruct((t, BRANCH_WIDTH), BF16),
        grid=(HEADS, t // tq),
        in_specs=[pl.BlockSpec((tq, HEAD_DIM), lambda h, i: (i, cq + h)),
                  pl.BlockSpec((t, HEAD_DIM), lambda h, i: (0, ck + h)),
                  pl.BlockSpec((t, HEAD_DIM), lambda h, i: (0, cv + h))],
        out_specs=pl.BlockSpec((tq, HEAD_DIM), lambda h, i: (i, h)),
        compiler_params=_cparams(2),
        name="sb_attention",
    )(zm, zm, zm)


def _gm_kernel(u_ref, v_ref, ng_ref, nb_ref, ws_ref, bs_ref, o_ref, *, n_chunk):
    u = jax.nn.gelu(u_ref[...].astype(F32))
    v = jax.nn.gelu(v_ref[...].astype(F32))
    v = _layer_norm(v, ng_ref[...], nb_ref[...])
    ii = lax.broadcasted_iota(jnp.int32, (GM_CHUNK, GM_CHUNK), 0)
    jj = lax.broadcasted_iota(jnp.int32, (GM_CHUNK, GM_CHUNK), 1)
    for g in range(HEADS):
        w = jnp.where(ii >= jj, ws_ref[g], 0.0).astype(BF16)
        bias = bs_ref[g]
        for c in range(n_chunk):
            rows = slice(c * GM_CHUNK, (c + 1) * GM_CHUNK)
            cols = slice(g * HEAD_DIM, (g + 1) * HEAD_DIM)
            mixed = _dot(w, v[rows, cols].astype(BF16)) + bias
            o_ref[rows, cols] = (u[rows, cols] * mixed).astype(o_ref.dtype)


def _gmlp(zm, norm_g, norm_b, w_s, b_s, tb):
    t = zm.shape[0]
    bias = jnp.broadcast_to(b_s[:, :, None], (HEADS, GM_CHUNK, HEAD_DIM))
    cu, cv = B_U // BRANCH_WIDTH, B_V // BRANCH_WIDTH
    return pl.pallas_call(
        functools.partial(_gm_kernel, n_chunk=tb // GM_CHUNK),
        out_shape=jax.ShapeDtypeStruct((t, BRANCH_WIDTH), BF16),
        grid=(t // tb,),
        in_specs=[pl.BlockSpec((tb, BRANCH_WIDTH), lambda i: (i, cu)),
                  pl.BlockSpec((tb, BRANCH_WIDTH), lambda i: (i, cv)),
                  pl.BlockSpec((1, BRANCH_WIDTH), lambda i: (0, 0)),
                  pl.BlockSpec((1, BRANCH_WIDTH), lambda i: (0, 0)),
                  pl.BlockSpec((HEADS, GM_CHUNK, GM_CHUNK), lambda i: (0, 0, 0)),
                  pl.BlockSpec((HEADS, GM_CHUNK, HEAD_DIM), lambda i: (0, 0, 0))],
        out_specs=pl.BlockSpec((tb, BRANCH_WIDTH), lambda i: (i, 0)),
        compiler_params=_cparams(1),
        name="gmlp",
    )(zm, zm, norm_g.reshape(1, -1), norm_b.reshape(1, -1), w_s, bias)


def _gdn_prep_kernel(q_ref, k_ref, v_ref, zs_ref, cw_ref, par_ref,
                     u_ref, w_ref, qe_ref, kd_ref, attn_ref, eg_ref, xs_ref, *, n_ch):
    c = GDN_CHUNK
    w3 = 3 * BRANCH_WIDTH
    rows_all = n_ch * c

    @pl.when(pl.program_id(0) == 0)
    def _():
        xs_ref[0:8, :] = jnp.zeros((8, w3), F32)

    xs_ref[8:8 + rows_all, 0:BRANCH_WIDTH] = q_ref[...].astype(F32)
    xs_ref[8:8 + rows_all, BRANCH_WIDTH:2 * BRANCH_WIDTH] = k_ref[...].astype(F32)
    xs_ref[8:8 + rows_all, 2 * BRANCH_WIDTH:w3] = v_ref[...].astype(F32)
    cw = cw_ref[...]
    y_all = xs_ref[5:5 + rows_all, :] * cw[0:1, :]
    for tap in range(1, GDN_CONV):
        y_all = y_all + xs_ref[5 + tap:5 + tap + rows_all, :] * cw[tap:tap + 1, :]
    xs_ref[0:8, :] = xs_ref[rows_all:rows_all + 8, :]
    y_all = _silu(y_all)

    zs_all = zs_ref[...]
    beta_full = _sigmoid(zs_all)
    g_full = -jnp.exp(par_ref[0:1, :]) * _softplus(zs_all + par_ref[1:2, :])

    ii = lax.broadcasted_iota(jnp.int32, (c, c), 0)
    jj = lax.broadcasted_iota(jnp.int32, (c, c), 1)
    tri_incl = jnp.where(ii >= jj, 1.0, 0.0).astype(BF16)
    ones_cc = jnp.ones((c, c), BF16)

    chains = []
    for ch, h in [(ch, h) for ch in range(n_ch) for h in range(HEADS)]:
        rows = slice(ch * c, (ch + 1) * c)
        y = y_all[rows]
        beta_all = beta_full[rows]
        g_all = g_full[rows]
        lanes = slice(h * HEAD_DIM, (h + 1) * HEAD_DIM)
        qh = y[:, h * HEAD_DIM:(h + 1) * HEAD_DIM]
        kh = y[:, BRANCH_WIDTH + h * HEAD_DIM:BRANCH_WIDTH + (h + 1) * HEAD_DIM]
        vh = y[:, 2 * BRANCH_WIDTH + h * HEAD_DIM:2 * BRANCH_WIDTH + (h + 1) * HEAD_DIM]
        qh = qh * lax.rsqrt(jnp.sum(qh * qh, axis=-1, keepdims=True) + 1e-6) * HEAD_DIM ** -0.5
        kh = kh * lax.rsqrt(jnp.sum(kh * kh, axis=-1, keepdims=True) + 1e-6)
        beta = beta_all[:, S_BETA + h:S_BETA + h + 1]
        g_b = jnp.broadcast_to(g_all[:, S_A + h:S_A + h + 1], (c, HEAD_DIM))
        chains.append(dict(ch=ch, h=h, rows=rows, lanes=lanes, qh=qh, kh=kh, vh=vh, beta=beta, g_b=g_b))

    for s in chains:
        s["gcol"] = _dot_sel_lhs(tri_incl, s["g_b"])
        s["grow"] = _dot_sel_lhs(ones_cc, jnp.where(ii <= jj, s["g_b"][:, :c], 0.0))
        s["kh16"] = s["kh"].astype(BF16)
        s["kb"] = s["kh"] * s["beta"]
        s["kk"] = _dot_nt(s["kb"].astype(BF16), s["kh16"])
        s["qk"] = _dot_nt(s["qh"].astype(BF16), s["kh16"])
    for s in chains:
        decay = jnp.where(ii >= jj, jnp.exp(s["gcol"][:, :c] - s["grow"]), 0.0)
        expg = jnp.exp(s["gcol"])
        g_last = s["gcol"][c - 1:c, :]
        rows, lanes = s["rows"], s["lanes"]
        qe_ref[rows, lanes] = (s["qh"] * expg).astype(BF16)
        kd_ref[rows, lanes] = (s["kh"] * jnp.exp(g_last - s["gcol"])).astype(BF16)
        attn_ref[s["h"], rows, :] = (s["qk"] * decay).astype(BF16)
        eg_ref[s["ch"], :, lanes] = jnp.broadcast_to(jnp.exp(g_last), (8, HEAD_DIM))
        s["p"] = -jnp.where(ii > jj, s["kk"] * decay, 0.0)
        s["x"] = jnp.concatenate([s["vh"] * s["beta"], s["kb"] * expg], axis=1)
    for r in range(6):
        for s in chains:
            s["x"] = s["x"] + _dot_f32(s["p"], s["x"])
            if r < 5:
                s["p"] = _dot_f32(s["p"], s["p"])
    for s in chains:
        u_ref[s["rows"], s["lanes"]] = s["x"][:, :HEAD_DIM]
        w_ref[s["rows"], s["lanes"]] = s["x"][:, HEAD_DIM:].astype(BF16)


def _gdn_scan_kernel(u_ref, w_ref, qe_ref, kd_ref, attn_ref, eg_ref, gate_ref, ng_ref, o_ref,
                     s_ref, *, n_ch):
    c = GDN_CHUNK

    @pl.when(pl.program_id(0) == 0)
    def _():
        s_ref[...] = jnp.zeros(s_ref.shape, F32)

    heads = range(HEADS)
    lanes = [slice(h * HEAD_DIM, (h + 1) * HEAD_DIM) for h in heads]
    state = [s_ref[h] for h in heads]
    for ch in range(n_ch):
        rows = slice(ch * c, (ch + 1) * c)
        s16 = [state[h].astype(BF16) for h in heads]
        v_new = [(u_ref[rows, lanes[h]] - _dot(w_ref[rows, lanes[h]], s16[h])).astype(BF16) for h in heads]
        state = [state[h] * eg_ref[ch, 0:1, lanes[h]] + _dot_tn(kd_ref[rows, lanes[h]], v_new[h])
                 for h in heads]
        out = [_dot(qe_ref[rows, lanes[h]], s16[h]) + _dot(attn_ref[h, rows, :], v_new[h]) for h in heads]
        for h in heads:
            o = out[h]
            o = o * lax.rsqrt(jnp.mean(o * o, axis=-1, keepdims=True) + NORM_EPS) * ng_ref[...]
            o_ref[rows, lanes[h]] = (o * _silu(gate_ref[rows, lanes[h]].astype(F32))).astype(o_ref.dtype)
    for h in heads:
        s_ref[h] = state[h]


def _gdn(zm, zs, conv_w, a_log, dt_bias, norm_g, prep_chunks, scan_chunks):
    t = zm.shape[0]
    c = GDN_CHUNK
    par = jnp.zeros((8, SMALL_COLS), F32)
    par = par.at[0, S_A:S_A + HEADS].set(a_log).at[1, S_A:S_A + HEADS].set(dt_bias)

    rp = prep_chunks * c
    blk = lambda col: pl.BlockSpec((rp, BRANCH_WIDTH), lambda n: (n, col // BRANCH_WIDTH))
    wide = pl.BlockSpec((rp, BRANCH_WIDTH), lambda n: (n, 0))
    u, w, qe, kd, attn, eg = pl.pallas_call(
        functools.partial(_gdn_prep_kernel, n_ch=prep_chunks),
        out_shape=(jax.ShapeDtypeStruct((t, BRANCH_WIDTH), F32),
                   jax.ShapeDtypeStruct((t, BRANCH_WIDTH), BF16),
                   jax.ShapeDtypeStruct((t, BRANCH_WIDTH), BF16),
                   jax.ShapeDtypeStruct((t, BRANCH_WIDTH), BF16),
                   jax.ShapeDtypeStruct((HEADS, t, c), BF16),
                   jax.ShapeDtypeStruct((t // c, 8, BRANCH_WIDTH), F32)),
        grid=(t // rp,),
        in_specs=[blk(C_Q), blk(C_K), blk(C_V),
                  pl.BlockSpec((rp, SMALL_COLS), lambda n: (n, 0)),
                  pl.BlockSpec((GDN_CONV, 3 * BRANCH_WIDTH), lambda n: (0, 0)),
                  pl.BlockSpec((8, SMALL_COLS), lambda n: (0, 0))],
        out_specs=(wide, wide, wide, wide,
                   pl.BlockSpec((HEADS, rp, c), lambda n: (0, n, 0)),
                   pl.BlockSpec((prep_chunks, 8, BRANCH_WIDTH), lambda n: (n, 0, 0))),
        scratch_shapes=[pltpu.VMEM((rp + 8, 3 * BRANCH_WIDTH), F32)],
        compiler_params=_cparams(1),
        name="gdn_chunk_prep",
    )(zm, zm, zm, zs, conv_w, par)

    rs = scan_chunks * c
    wide = pl.BlockSpec((rs, BRANCH_WIDTH), lambda n: (n, 0))
    return pl.pallas_call(
        functools.partial(_gdn_scan_kernel, n_ch=scan_chunks),
        out_shape=jax.ShapeDtypeStruct((t, BRANCH_WIDTH), BF16),
        grid=(t // rs,),
        in_specs=[wide, wide, wide, wide,
                  pl.BlockSpec((HEADS, rs, c), lambda n: (0, n, 0)),
                  pl.BlockSpec((scan_chunks, 8, BRANCH_WIDTH), lambda n: (n, 0, 0)),
                  pl.BlockSpec((rs, BRANCH_WIDTH), lambda n: (n, C_G // BRANCH_WIDTH)),
                  pl.BlockSpec((1, HEAD_DIM), lambda n: (0, 0))],
        out_specs=wide,
        scratch_shapes=[pltpu.VMEM((HEADS, HEAD_DIM, HEAD_DIM), F32)],
        compiler_params=_cparams(1),
        name="gdn_state_scan",
    )(u, w, qe, kd, attn, eg, zm, norm_g.reshape(1, -1))


def _ml_kernel(q_ref, k_ref, v_ref, og_ref, zs_ref, par_ref, ng_ref, o_ref,
               c_ref, n_ref, m_ref, *, n_ch):
    c = ML_CHUNK

    @pl.when(pl.program_id(0) == 0)
    def _():
        c_ref[...] = jnp.zeros(c_ref.shape, F32)
        n_ref[...] = jnp.zeros(n_ref.shape, F32)
        m_ref[...] = jnp.zeros(m_ref.shape, F32)

    zz_all = zs_ref[...] + par_ref[2:3, :]
    logf_all = -_softplus(-zz_all)

    ii = lax.broadcasted_iota(jnp.int32, (c, c), 0)
    jj = lax.broadcasted_iota(jnp.int32, (c, c), 1)
    tri_incl = jnp.where(ii >= jj, 1.0, 0.0).astype(BF16)
    ones_cc = jnp.ones((c, c), BF16)
    ones_cl = jnp.ones((c, HEAD_DIM), BF16)

    pairs = []
    for ch in range(n_ch):
        rows = slice(ch * c, (ch + 1) * c)
        for h in range(HEADS):
            lanes = slice(h * HEAD_DIM, (h + 1) * HEAD_DIM)
            qk_lanes = slice(h * ML_QK, (h + 1) * ML_QK)
            pairs.append(dict(
                ch=ch, h=h, rows=rows, lanes=lanes,
                qs=(q_ref[rows, qk_lanes].astype(F32) * ML_QK ** -0.5).astype(BF16),
                kh=k_ref[rows, qk_lanes], vh=v_ref[rows, lanes],
                i_b=jnp.broadcast_to(zz_all[rows, S_I + h:S_I + h + 1], (c, HEAD_DIM)),
                lf_b=jnp.broadcast_to(logf_all[rows, S_F + h:S_F + h + 1], (c, HEAD_DIM))))
    for s in pairs:
        s["bcol"] = _dot_sel_lhs(tri_incl, s["lf_b"])
        s["qk"] = _dot_nt(s["qs"], s["kh"])
    for s in pairs:
        rv = s["i_b"] - s["bcol"]
        s["rrow"] = _dot_sel_lhs(ones_cc, jnp.where(ii == jj, rv[:, :c], 0.0))
        s["b_last"] = s["bcol"][c - 1:c, :]
        s["log_kv"] = s["b_last"] - s["bcol"] + s["i_b"]
        s["kv_max"] = jnp.max(s["log_kv"], axis=0, keepdims=True)
    for s in pairs:
        s["log_intra"] = jnp.where(ii >= jj, s["bcol"][:, :c] + s["rrow"], -jnp.inf)
        s["intra_max"] = jnp.max(s["log_intra"], axis=1, keepdims=True)

    heads = range(HEADS)
    c_mat = [c_ref[h] for h in heads]
    n_b = [n_ref[h] for h in heads]
    m_prev = [m_ref[h][0:1, :] for h in heads]
    for ch in range(n_ch):
        cur = pairs[ch * HEADS:(ch + 1) * HEADS]
        m_out = [jnp.maximum(s["bcol"] + m_prev[s["h"]], s["intra_max"]) for s in cur]
        w_inter = [jnp.exp(s["bcol"] + m_prev[s["h"]] - m_out[s["h"]]) for s in cur]
        sc = [jnp.exp(s["log_intra"] - m_out[s["h"]][:, :c]) * s["qk"] for s in cur]
        num = [w_inter[h] * _dot(cur[h]["qs"], c_mat[h].astype(BF16)) + _dot(sc[h].astype(BF16), cur[h]["vh"])
               for h in heads]
        den = [w_inter[h] * _dot(cur[h]["qs"], n_b[h].astype(BF16)) + jnp.sum(sc[h], axis=1, keepdims=True)
               for h in heads]
        m_new = [jnp.maximum(s["b_last"] + m_prev[s["h"]], s["kv_max"]) for s in cur]
        kw = [(s["kh"].astype(F32) * jnp.exp(s["log_kv"] - m_new[s["h"]])[:, :ML_QK]).astype(BF16) for s in cur]
        decay = [jnp.exp(s["b_last"] + m_prev[s["h"]] - m_new[s["h"]]) for s in cur]
        c_mat = [decay[h] * c_mat[h] + _dot_tn(kw[h], cur[h]["vh"]) for h in heads]
        n_b = [decay[h] * n_b[h] + _dot_tn(kw[h], ones_cl) for h in heads]
        m_prev = m_new
        for h in heads:
            s = cur[h]
            hid = num[h] / jnp.maximum(jnp.abs(den[h]), jnp.exp(-m_out[h]))
            hid = hid * lax.rsqrt(jnp.mean(hid * hid, axis=-1, keepdims=True) + NORM_EPS)
            hid = hid * ng_ref[:, s["lanes"]]
            gate = _sigmoid(og_ref[s["rows"], s["lanes"]].astype(F32))
            o_ref[s["rows"], s["lanes"]] = (hid * gate).astype(o_ref.dtype)
    for h in heads:
        c_ref[h] = c_mat[h]
        n_ref[h] = n_b[h]
        m_ref[h] = jnp.broadcast_to(m_prev[h], (8, HEAD_DIM))


def _mlstm(zm, zs, i_bias, f_bias, norm_g, n_ch):
    t = zm.shape[0]
    c = ML_CHUNK * n_ch
    qk_w = HEADS * ML_QK
    par = jnp.zeros((8, SMALL_COLS), F32)
    par = par.at[2, S_I:S_I + HEADS].set(i_bias).at[2, S_F:S_F + HEADS].set(f_bias)
    return pl.pallas_call(
        functools.partial(_ml_kernel, n_ch=n_ch),
        out_shape=jax.ShapeDtypeStruct((t, BRANCH_WIDTH), BF16),
        grid=(t // c,),
        in_specs=[pl.BlockSpec((c, qk_w), lambda n: (n, D_Q // qk_w)),
                  pl.BlockSpec((c, qk_w), lambda n: (n, D_K // qk_w)),
                  pl.BlockSpec((c, BRANCH_WIDTH), lambda n: (n, D_V // BRANCH_WIDTH)),
                  pl.BlockSpec((c, BRANCH_WIDTH), lambda n: (n, D_O // BRANCH_WIDTH)),
                  pl.BlockSpec((c, SMALL_COLS), lambda n: (n, 0)),
                  pl.BlockSpec((8, SMALL_COLS), lambda n: (0, 0)),
                  pl.BlockSpec((1, BRANCH_WIDTH), lambda n: (0, 0))],
        out_specs=pl.BlockSpec((c, BRANCH_WIDTH), lambda n: (n, 0)),
        scratch_shapes=[pltpu.VMEM((HEADS, ML_QK, HEAD_DIM), F32),
                        pltpu.VMEM((HEADS, ML_QK, HEAD_DIM), F32),
                        pltpu.VMEM((HEADS, 8, HEAD_DIM), F32)],
        compiler_params=_cparams(1),
        name="mlstm",
    )(zm, zm, zm, zm, zs, par, norm_g.reshape(1, -1))


def _merge_kernel(a_ref, b_ref, c_ref, d_ref, g0_ref, g1_ref, g2_ref, g3_ref, wb_ref, o_ref):
    acc = None
    for g, (br, gr) in enumerate(((a_ref, g0_ref), (b_ref, g1_ref), (c_ref, g2_ref), (d_ref, g3_ref))):
        term = _sigmoid(gr[...].astype(F32)) * _dot(br[...], wb_ref[g])
        acc = term if acc is None else acc + term
    o_ref[...] = acc.astype(o_ref.dtype)


def _merge(outs, zm, wb16, tm):
    t = zm.shape[0]
    branch = pl.BlockSpec((tm, BRANCH_WIDTH), lambda i: (i, 0))
    gate = lambda g: pl.BlockSpec((tm, D_MODEL), lambda i: (i, g))
    return pl.pallas_call(
        _merge_kernel,
        out_shape=jax.ShapeDtypeStruct((t, D_MODEL), BF16),
        grid=(t // tm,),
        in_specs=[branch] * 4 + [gate(0), gate(1), gate(2), gate(3),
                                 pl.BlockSpec((N_BRANCHES, BRANCH_WIDTH, D_MODEL), lambda i: (0, 0, 0))],
        out_specs=pl.BlockSpec((tm, D_MODEL), lambda i: (i, 0)),
        compiler_params=_cparams(1),
        name="merge_branches",
    )(*outs, zm, zm, zm, zm, wb16)


def _proj_ln_kernel(a_ref, w_ref, x_ref, g_ref, b_ref, o_ref):
    y = ALPHA * x_ref[...] + _dot(a_ref[...], w_ref[...])
    o_ref[...] = _layer_norm(y, g_ref[...], b_ref[...])


def _proj_ln(a, w16, x, g, b, tm):
    t, k = a.shape
    return pl.pallas_call(
        _proj_ln_kernel,
        out_shape=jax.ShapeDtypeStruct((t, D_MODEL), F32),
        grid=(t // tm,),
        in_specs=[pl.BlockSpec((tm, k), lambda i: (i, 0)),
                  pl.BlockSpec((k, D_MODEL), lambda i: (0, 0)),
                  pl.BlockSpec((tm, D_MODEL), lambda i: (i, 0)),
                  pl.BlockSpec((1, D_MODEL), lambda i: (0, 0)),
                  pl.BlockSpec((1, D_MODEL), lambda i: (0, 0))],
        out_specs=pl.BlockSpec((tm, D_MODEL), lambda i: (i, 0)),
        compiler_params=_cparams(1),
        name="out_proj_layernorm",
    )(a, w16, x, g.reshape(1, -1), b.reshape(1, -1))


def _router_kernel(x_ref, wr_ref, bias_ref, idx_ref, w_ref, rank_ref, cnt_ref, *, tm):
    @pl.when(pl.program_id(0) == 0)
    def _():
        cnt_ref[...] = jnp.zeros(cnt_ref.shape, F32)

    logits = _dot_f32_nt(wr_ref[...], x_ref[...])
    scores = _sigmoid(logits)
    biased = scores + bias_ref[...]
    neg = -jnp.inf

    b3 = biased.reshape(N_GROUPS, GROUP_SIZE, tm)
    pos = lax.broadcasted_iota(jnp.int32, (N_GROUPS, GROUP_SIZE, tm), 1)
    m1 = jnp.max(b3, axis=1, keepdims=True)
    first = jnp.min(jnp.where(b3 == m1, pos, GROUP_SIZE), axis=1, keepdims=True)
    m2 = jnp.max(jnp.where(pos == first, neg, b3), axis=1, keepdims=True)
    gscore = m1 + m2

    gidx = lax.broadcasted_iota(jnp.int32, (N_GROUPS, 1, tm), 0)
    gsel = jnp.zeros((N_GROUPS, 1, tm), F32)
    for _ in range(TOPK_GROUPS):
        gm = jnp.max(gscore, axis=0, keepdims=True)
        gfirst = jnp.min(jnp.where(gscore == gm, gidx, N_GROUPS), axis=0, keepdims=True)
        hit = gidx == gfirst
        gsel = jnp.where(hit, 1.0, gsel)
        gscore = jnp.where(hit, neg, gscore)

    allowed = jnp.broadcast_to(gsel, (N_GROUPS, GROUP_SIZE, tm)) > 0.0
    masked = jnp.where(allowed, b3, neg).reshape(N_EXPERTS, tm)
    eidx = lax.broadcasted_iota(jnp.int32, (N_EXPERTS, tm), 0)
    idx_rows, sel_rows, hits = [], [], []
    for _ in range(TOP_K):
        em = jnp.max(masked, axis=0, keepdims=True)
        efirst = jnp.min(jnp.where(masked == em, eidx, N_EXPERTS), axis=0, keepdims=True)
        hit = eidx == efirst
        hits.append(hit)
        idx_rows.append(efirst)
        sel_rows.append(jnp.sum(jnp.where(hit, scores, 0.0), axis=0, keepdims=True))
        masked = jnp.where(hit, neg, masked)
    total = sel_rows[0]
    for r in sel_rows[1:]:
        total = total + r
    zero_i = jnp.zeros((1, tm), jnp.int32)
    zero_f = jnp.zeros((1, tm), F32)
    idx_ref[...] = jnp.concatenate(idx_rows + [zero_i, zero_i], axis=0)
    w_ref[...] = jnp.concatenate([r / total * ROUTED_SCALE for r in sel_rows] + [zero_f, zero_f], axis=0)

    chosen = jnp.zeros((N_EXPERTS, tm), F32)
    for hit in hits:
        chosen = jnp.where(hit, 1.0, chosen)
    chosen16 = chosen.astype(BF16)
    earlier = (lax.broadcasted_iota(jnp.int32, (tm, tm), 0)
               < lax.broadcasted_iota(jnp.int32, (tm, tm), 1))
    cnt = cnt_ref[...]
    before = _dot(chosen16, jnp.where(earlier, 1.0, 0.0).astype(BF16)) + jnp.tile(cnt, (1, tm // HEAD_DIM))
    ranks = [jnp.sum(jnp.where(hit, before, 0.0), axis=0, keepdims=True).astype(jnp.int32) for hit in hits]
    rank_ref[...] = jnp.concatenate(ranks + [zero_i, zero_i], axis=0)
    cnt_ref[...] = cnt + _dot(chosen16, jnp.ones((tm, HEAD_DIM), BF16))


def _dot_f32_nt(a, b):
    ah, al = _split2(a)
    bh, bl = _split2(b)
    return _dot_nt(ah, bh) + _dot_nt(ah, bl) + _dot_nt(al, bh)


def _router(x, w_router, router_bias, tm):
    t = x.shape[0]
    return pl.pallas_call(
        functools.partial(_router_kernel, tm=tm),
        out_shape=(jax.ShapeDtypeStruct((8, t), jnp.int32), jax.ShapeDtypeStruct((8, t), F32),
                   jax.ShapeDtypeStruct((8, t), jnp.int32),
                   jax.ShapeDtypeStruct((N_EXPERTS, HEAD_DIM), F32)),
        grid=(t // tm,),
        in_specs=[pl.BlockSpec((tm, D_MODEL), lambda i: (i, 0)),
                  pl.BlockSpec((N_EXPERTS, D_MODEL), lambda i: (0, 0)),
                  pl.BlockSpec((N_EXPERTS, 1), lambda i: (0, 0))],
        out_specs=(pl.BlockSpec((8, tm), lambda i: (0, i)), pl.BlockSpec((8, tm), lambda i: (0, i)),
                   pl.BlockSpec((8, tm), lambda i: (0, i)),
                   pl.BlockSpec((N_EXPERTS, HEAD_DIM), lambda i: (0, 0))),
        compiler_params=_cparams(1),
        name="router_topk",
    )(x, w_router.T, router_bias.reshape(N_EXPERTS, 1))


def _dispatch_kernel(pos_ref, x_ref, xs_hbm, sem, *, td):
    for r in range(td):
        for k in range(TOP_K):
            pltpu.make_async_copy(x_ref.at[pl.ds(r, 1), :],
                                  xs_hbm.at[pl.ds(pos_ref[0, r * TOP_K + k], 1), :], sem).start()
    for _ in range(TOP_K):
        pltpu.make_async_copy(x_ref, xs_hbm.at[pl.ds(0, td), :], sem).wait()


def _dispatch(x, pos, td):
    t = x.shape[0]
    return pl.pallas_call(
        functools.partial(_dispatch_kernel, td=td),
        out_shape=jax.ShapeDtypeStruct((t * TOP_K, D_MODEL), F32),
        grid=(t // td,),
        in_specs=[pl.BlockSpec((None, 1, td * TOP_K), lambda i: (i, 0, 0), memory_space=pltpu.SMEM),
                  pl.BlockSpec((td, D_MODEL), lambda i: (i, 0))],
        out_specs=pl.BlockSpec(memory_space=pl.ANY),
        scratch_shapes=[pltpu.SemaphoreType.DMA],
        compiler_params=_cparams(1),
        name="moe_dispatch",
    )(pos.reshape(t // td, 1, td * TOP_K), x)


def _expert_kernel(n_ref, tile_ref, exp_ref, nxt_ref, par_ref, lo_ref, hi_ref,
                   x_ref, wg_hbm, wu_hbm, wd_hbm, y_ref,
                   wgbuf, wubuf, wdbuf, wg16, wu16, wd16, wsem, *, tm, layer):
    j = pl.program_id(0)
    prev = jnp.maximum(j - 1, 0)
    first = (j == 0) | (tile_ref[prev] != tile_ref[j])
    expert = exp_ref[j]
    slot = par_ref[j]

    def copies(ex, s):
        return [pltpu.make_async_copy(hbm.at[layer, ex], buf.at[s], wsem.at[s])
                for hbm, buf in ((wg_hbm, wgbuf), (wu_hbm, wubuf), (wd_hbm, wdbuf))]

    @pl.when(j < n_ref[0])
    def _():
        @pl.when((j == 0) | (exp_ref[prev] != expert))
        def _():
            @pl.when(j == 0)
            def _():
                for cp in copies(expert, slot):
                    cp.start()

            @pl.when(nxt_ref[j] != expert)
            def _():
                for cp in copies(nxt_ref[j], 1 - slot):
                    cp.start()

            for cp in copies(expert, slot):
                cp.wait()
            wg16[...] = wgbuf[slot].astype(BF16)
            wu16[...] = wubuf[slot].astype(BF16)
            wd16[...] = wdbuf[slot].astype(BF16)

        x = x_ref[...].astype(BF16)
        hidden = _silu(_dot(x, wg16[...])) * _dot(x, wu16[...])
        y = _dot(hidden.astype(BF16), wd16[...])

        @pl.when(first)
        def _():
            y_ref[...] = y

        @pl.when(jnp.logical_not(first))
        def _():
            row = lax.broadcasted_iota(jnp.int32, (tm, 1), 0)
            mine = (row >= lo_ref[j]) & (row < hi_ref[j])
            y_ref[...] = jnp.where(mine, y, y_ref[...])


def _experts(xs, sched, w_gate, w_up, w_down, layer, tm):
    n_max = sched[1].shape[0]
    tile_of = lambda j, n, tl, *rest: tl[j]
    hbm = pl.BlockSpec(memory_space=pl.ANY)
    up_shape, down_shape = (D_MODEL, EXPERT_DIM), (EXPERT_DIM, D_MODEL)
    return pl.pallas_call(
        functools.partial(_expert_kernel, tm=tm, layer=layer),
        out_shape=jax.ShapeDtypeStruct(xs.shape, F32),
        grid_spec=pltpu.PrefetchScalarGridSpec(
            num_scalar_prefetch=len(sched),
            grid=(n_max,),
            in_specs=[pl.BlockSpec((tm, D_MODEL), lambda *a: (tile_of(*a), 0)), hbm, hbm, hbm],
            out_specs=pl.BlockSpec((tm, D_MODEL), lambda *a: (tile_of(*a), 0)),
            scratch_shapes=[pltpu.VMEM((2,) + up_shape, F32), pltpu.VMEM((2,) + up_shape, F32),
                            pltpu.VMEM((2,) + down_shape, F32),
                            pltpu.VMEM(up_shape, BF16), pltpu.VMEM(up_shape, BF16),
                            pltpu.VMEM(down_shape, BF16),
                            pltpu.SemaphoreType.DMA((2,))]),
        compiler_params=_cparams(1),
        name="routed_experts",
    )(*sched, xs, w_gate, w_up, w_down)


def _expert_schedule(idx, rank, counts, t, tm):
    i32 = jnp.int32
    n_rows = t * TOP_K
    n_tiles = n_rows // tm
    n_max = n_tiles + N_EXPERTS - 1
    end = jnp.cumsum(counts)
    start = end - counts
    pos = (jnp.take(start, idx) + rank).reshape(-1).astype(i32)
    first_tile = start // tm
    items_per_e = jnp.where(counts > 0, (end - 1) // tm - first_tile + 1, 0)
    item_end = jnp.cumsum(items_per_e)
    n_items = item_end[-1]
    j = jnp.arange(n_max, dtype=i32)
    e = jnp.minimum(jnp.sum((item_end[None, :] <= j[:, None]).astype(i32), axis=1), N_EXPERTS - 1)
    tile = first_tile[e] + j - (item_end[e] - items_per_e[e])
    lo = jnp.maximum(start[e], tile * tm) - tile * tm
    hi = jnp.minimum(end[e], (tile + 1) * tm) - tile * tm
    valid = j < n_items
    last = jnp.maximum(n_items - 1, 0)
    pick = lambda a: jnp.where(valid, a, a[last]).astype(i32)
    ids = jnp.arange(N_EXPERTS, dtype=i32)
    later = (ids[None, :] > ids[:, None]) & (counts[None, :] > 0)
    nxt = jnp.min(jnp.where(later, ids[None, :], N_EXPERTS), axis=1)
    nxt = jnp.where(nxt == N_EXPERTS, ids, nxt)
    parity = (jnp.cumsum((counts > 0).astype(i32)) - 1) % 2
    return pos, (n_items.reshape(1).astype(i32), pick(tile), pick(e), pick(nxt[e]), pick(parity[e]),
                 pick(lo), pick(hi))


def _combine_kernel(pos_ref, pos_next_ref, x_ref, w_ref, ys_hbm, sg_ref, su_ref, sd_ref, g_ref, b_ref,
                    o_ref, o16_ref, ybuf, sem, *, tt, n_steps):
    i = pl.program_id(0)
    slot = lax.rem(i, 2)

    def gather(idx_ref, s):
        for r in range(tt):
            for k in range(TOP_K):
                pltpu.make_async_copy(ys_hbm.at[pl.ds(idx_ref[0, r * TOP_K + k], 1), :],
                                      ybuf.at[s, k, pl.ds(r, 1), :], sem.at[s]).start()

    @pl.when(i == 0)
    def _():
        gather(pos_ref, 0)

    @pl.when(i + 1 < n_steps)
    def _():
        gather(pos_next_ref, 1 - slot)

    x = x_ref[...]
    x16 = x.astype(BF16)
    hidden = _silu(_dot(x16, sg_ref[...])) * _dot(x16, su_ref[...])
    acc = _dot(hidden.astype(BF16), sd_ref[...])
    for k in range(TOP_K):
        pltpu.make_async_copy(ys_hbm.at[pl.ds(0, tt), :], ybuf.at[slot, k], sem.at[slot]).wait()
    w = w_ref[...]
    for k in range(TOP_K):
        acc = acc + ybuf[slot, k] * w[:, k:k + 1]
    out = _layer_norm(ALPHA * x + acc, g_ref[...], b_ref[...])
    o_ref[...] = out
    o16_ref[...] = out.astype(BF16)


def _combine(x, w_tok, pos, ys, sg16, su16, sd16, g, b, tt):
    t = x.shape[0]
    nb = t // tt
    pos3 = pos.reshape(nb, 1, tt * TOP_K)
    tile = pl.BlockSpec((tt, D_MODEL), lambda i: (i, 0))
    vec = pl.BlockSpec((1, D_MODEL), lambda i: (0, 0))
    idx_spec = lambda f: pl.BlockSpec((None, 1, tt * TOP_K), f, memory_space=pltpu.SMEM)
    return pl.pallas_call(
        functools.partial(_combine_kernel, tt=tt, n_steps=nb),
        out_shape=(jax.ShapeDtypeStruct((t, D_MODEL), F32), jax.ShapeDtypeStruct((t, D_MODEL), BF16)),
        grid=(nb,),
        in_specs=[idx_spec(lambda i: (i, 0, 0)), idx_spec(lambda i: (jnp.minimum(i + 1, nb - 1), 0, 0)),
                  tile, pl.BlockSpec((tt, 8), lambda i: (i, 0)),
                  pl.BlockSpec(memory_space=pl.ANY),
                  pl.BlockSpec((D_MODEL, SHARED_DIM), lambda i: (0, 0)),
                  pl.BlockSpec((D_MODEL, SHARED_DIM), lambda i: (0, 0)),
                  pl.BlockSpec((SHARED_DIM, D_MODEL), lambda i: (0, 0)), vec, vec],
        out_specs=(tile, tile),
        scratch_shapes=[pltpu.VMEM((2, TOP_K, tt, D_MODEL), F32), pltpu.SemaphoreType.DMA((2,))],
        compiler_params=_cparams(1),
        name="moe_combine_layernorm",
    )(pos3, pos3, x, w_tok, ys, sg16, su16, sd16, g.reshape(1, -1), b.reshape(1, -1))


def _tile(t, pref):
    return min(t, pref)


def _in_proj(x16, w_in):
    t = x16.shape[0]
    tm = _tile(t, 1024)
    wide = lambda r, tn, name: _matmul_nt(x16, w_in[:, r[0]:r[1]].T.astype(BF16), BF16, tm, tn, name)
    z_abc = wide(ABC_COLS, 1536, "in_proj_abc")
    z_d = wide(D_COLS, 1536, "in_proj_d")
    z_gate = wide(GATE_COLS, 1024, "in_proj_gate")
    pad = jnp.zeros((D_MODEL, SMALL_COLS - sum(b - a for a, b in SMALL_SRC)), F32)
    w_small = jnp.concatenate([w_in[:, a:b] for a, b in SMALL_SRC] + [pad], axis=1).astype(BF16)
    zs = _matmul(x16, w_small, F32, tm, SMALL_COLS, "in_proj_small")
    return z_abc, z_d, z_gate, zs


def _mixer(x, x16, p):
    t = x.shape[0]
    zm, z_d, z_gate, zs = _in_proj(x16, p["w_in"])
    out_a = _sb_attention(zm, 256, 256, 8)
    out_b = _gmlp(zm, p["gm_norm_g"], p["gm_norm_b"], p["gm_w_s"], p["gm_b_s"], _tile(t, 512))
    out_c = _gdn(zm, zs, p["gdn_conv_w"], p["gdn_a_log"], p["gdn_dt_bias"], p["gdn_norm_g"], 4, 4)
    out_d = _mlstm(z_d, zs, p["ml_i_bias"], p["ml_f_bias"], p["ml_norm_g"], 2)
    merged = _merge((out_a, out_b, out_c, out_d), z_gate, p["w_branch"].astype(BF16), _tile(t, 256))
    return _proj_ln(merged, p["w_out"].astype(BF16), x, p["ln1_g"], p["ln1_b"], _tile(t, 256))


def _moe(x, p, tm):
    t = x.shape[0]
    idx_t, w_t, rank_t, cnt = _router(x, p["w_router"], p["router_bias"], _tile(t, 512))
    w_tok = w_t.T
    counts = cnt[:, 0].astype(jnp.int32)
    pos, sched = _expert_schedule(idx_t[:TOP_K].T, rank_t[:TOP_K].T, counts, t, tm)
    xs = _dispatch(x, pos, _tile(t, 256))
    ys = _experts(xs, sched, p["w_gate_all"], p["w_up_all"], p["w_down_all"], p["layer"], tm)
    return _combine(x, w_tok, pos, ys, p["ws_gate"].astype(BF16), p["ws_up"].astype(BF16),
                    p["ws_down"].astype(BF16), p["ln2_g"], p["ln2_b"], _tile(t, 128))


def _layer(x, x16, p, tm_expert):
    x1 = _mixer(x, x16, p)
    return _moe(x1, p, tm_expert)


_STACKED = ("w_gate", "w_up", "w_down")


def _layer_params(params, l):
    p = {k: v[l] for k, v in params.items() if k not in _STACKED}
    p.update({k + "_all": params[k] for k in _STACKED})
    p["layer"] = l
    return p


def kernel(x, w_in, gm_norm_g, gm_norm_b, gm_w_s, gm_b_s, gdn_conv_w, gdn_a_log, gdn_dt_bias, gdn_norm_g, ml_i_bias, ml_f_bias, ml_norm_g, w_branch, w_out, ln1_g, ln1_b, w_router, router_bias, w_gate, w_up, w_down, ws_gate, ws_up, ws_down, ln2_g, ln2_b):
    params = dict(w_in=w_in, gm_norm_g=gm_norm_g, gm_norm_b=gm_norm_b, gm_w_s=gm_w_s, gm_b_s=gm_b_s,
                  gdn_conv_w=gdn_conv_w, gdn_a_log=gdn_a_log, gdn_dt_bias=gdn_dt_bias,
                  gdn_norm_g=gdn_norm_g, ml_i_bias=ml_i_bias, ml_f_bias=ml_f_bias, ml_norm_g=ml_norm_g,
                  w_branch=w_branch, w_out=w_out, ln1_g=ln1_g, ln1_b=ln1_b, w_router=w_router,
                  router_bias=router_bias, w_gate=w_gate, w_up=w_up, w_down=w_down, ws_gate=ws_gate,
                  ws_up=ws_up, ws_down=ws_down, ln2_g=ln2_g, ln2_b=ln2_b)
    b, t, d = x.shape
    h = x.reshape(b * t, d)
    h16 = h.astype(BF16)
    for l in range(DEPTH):
        h, h16 = _layer(h, h16, _layer_params(params, l), 256)
    return h.reshape(b, t, d)
```

```python
import functools

import jax
import jax.numpy as jnp
from jax import lax
from jax.experimental import pallas as pl
from jax.experimental.pallas import tpu as pltpu

F32 = jnp.float32
BF16 = jnp.bfloat16

D_MODEL = 2048
N_BRANCHES = 4
BRANCH_WIDTH = 512
HEADS = 4
HEAD_DIM = 128
GM_CHUNK = 128
GDN_CHUNK = 64
GDN_CONV = 4
ML_CHUNK = 64
ML_QK = 64
N_EXPERTS = 64
N_GROUPS = 8
GROUP_SIZE = 8
TOPK_GROUPS = 4
TOP_K = 6
EXPERT_DIM = 512
SHARED_DIM = 512
ROUTED_SCALE = 2.5
DEPTH = 2
ALPHA = (2 * DEPTH) ** 0.25
NORM_EPS = 1e-5
LOG2E = 1.4426950408889634

ABC_COLS = (0, 4608)
A_Q, A_K, A_V = 0, 512, 1024
B_U, B_V = 1536, 2048
C_Q, C_K, C_V, C_G = 2560, 3072, 3584, 4096
D_COLS = (4616, 6152)
D_Q, D_K, D_V, D_O = 0, 256, 512, 1024
GATE_COLS = (6160, 6160 + N_BRANCHES * D_MODEL)
SMALL_SRC = ((4608, 4616), (6152, 6160))
S_BETA, S_A, S_I, S_F = 0, 4, 8, 12
SMALL_COLS = 128

VMEM_LIMIT = 56 * 1024 * 1024


def _cparams(n_axes):
    return pltpu.CompilerParams(
        dimension_semantics=("arbitrary",) * n_axes, vmem_limit_bytes=VMEM_LIMIT)


def _dot(a, b):
    return jnp.dot(a, b, preferred_element_type=F32)


def _dot_nt(a, b):
    return lax.dot_general(a, b, (((1,), (1,)), ((), ())), preferred_element_type=F32)


def _dot_tn(a, b):
    return lax.dot_general(a, b, (((0,), (0,)), ((), ())), preferred_element_type=F32)


def _split2(a):
    hi = a.astype(BF16)
    lo = (a - hi.astype(F32)).astype(BF16)
    return hi, lo


def _split3(a):
    hi = a.astype(BF16)
    r = a - hi.astype(F32)
    mid = r.astype(BF16)
    lo = (r - mid.astype(F32)).astype(BF16)
    return hi, mid, lo


def _dot_sel_lhs(sel, a):
    sel3 = jnp.concatenate([sel, sel, sel], axis=1)
    return _dot(sel3, jnp.concatenate(_split3(a), axis=0))


def _dot_sel_rhs(a, sel):
    hi, lo = _split2(a)
    return _dot(hi, sel) + _dot(lo, sel)


def _dot_f32(a, b):
    ah, al = _split2(a)
    bh, bl = _split2(b)
    return _dot(jnp.concatenate([ah, ah, al], axis=1), jnp.concatenate([bh, bl, bh], axis=0))


def _softplus(x):
    return jnp.maximum(x, 0.0) + jnp.log1p(jnp.exp(-jnp.abs(x)))


def _sigmoid(x):
    return 1.0 / (1.0 + jnp.exp(-x))


def _silu(x):
    return x * _sigmoid(x)


def _layer_norm(x, g, b):
    mu = jnp.mean(x, axis=-1, keepdims=True)
    xc = x - mu
    var = jnp.mean(xc * xc, axis=-1, keepdims=True)
    return xc * lax.rsqrt(var + NORM_EPS) * g + b


def _mm_kernel(x_ref, w_ref, o_ref):
    o_ref[...] = _dot(x_ref[...], w_ref[...]).astype(o_ref.dtype)


def _matmul(x, w, out_dtype, tm, tn, name):
    m, k = x.shape
    n = w.shape[1]
    return pl.pallas_call(
        _mm_kernel,
        out_shape=jax.ShapeDtypeStruct((m, n), out_dtype),
        grid=(n // tn, m // tm),
        in_specs=[pl.BlockSpec((tm, k), lambda j, i: (i, 0)),
                  pl.BlockSpec((k, tn), lambda j, i: (0, j))],
        out_specs=pl.BlockSpec((tm, tn), lambda j, i: (i, j)),
        compiler_params=_cparams(2),
        name=name,
    )(x, w)


def _mm_nt_kernel(x_ref, wt_ref, o_ref):
    o_ref[...] = _dot_nt(x_ref[...], wt_ref[...]).astype(o_ref.dtype)


def _matmul_nt(x, wt, out_dtype, tm, tn, name):
    m, k = x.shape
    n = wt.shape[0]
    return pl.pallas_call(
        _mm_nt_kernel,
        out_shape=jax.ShapeDtypeStruct((m, n), out_dtype),
        grid=(n // tn, m // tm),
        in_specs=[pl.BlockSpec((tm, k), lambda j, i: (i, 0)),
                  pl.BlockSpec((tn, k), lambda j, i: (j, 0))],
        out_specs=pl.BlockSpec((tm, tn), lambda j, i: (i, j)),
        compiler_params=_cparams(2),
        name=name,
    )(x, wt)


def _sb_kernel(q_ref, k_ref, v_ref, o_ref, *, tq, tk, per_trip):
    i = pl.program_id(1)
    q = (q_ref[...].astype(F32) * (HEAD_DIM ** -0.5 * LOG2E)).astype(BF16)
    half = tk // 2
    row = i * tq + lax.broadcasted_iota(jnp.int32, (tq, tk), 0)
    col0 = lax.broadcasted_iota(jnp.int32, (tq, tk), 1)
    jj = lax.broadcasted_iota(jnp.int32, (tk, tk), 0)
    ss = lax.broadcasted_iota(jnp.int32, (tk, tk), 1)
    sel = jnp.where((jj >= ss) & ((jj < half) == (ss < half)), 1.0, 0.0).astype(BF16)

    def blocks(starts, carry, on_diagonal):
        zs = [_dot_nt(q, k_ref[pl.ds(ks, tk), :]) for ks in starts]
        sps = [jnp.maximum(z, 0.0) + jnp.log(1.0 + jnp.exp2(-jnp.abs(z))) * LOG2E for z in zs]
        if on_diagonal:
            masks = [(col0 + ks) < row for ks in starts]
            sps = [jnp.where(m, sp, 0.0) for m, sp in zip(masks, sps)]
        tails = [_dot(sp.astype(BF16), sel) for sp in sps]
        bases = [z - tail for z, tail in zip(zs, tails)]
        late_total = [jnp.broadcast_to(tail[:, half:half + 1], (tq, half)) for tail in tails]
        early_total = [jnp.broadcast_to(tail[:, 0:1], (tq, half)) for tail in tails]
        acc, run = carry
        for n, ks in enumerate(starts):
            run_mid = run + late_total[n]
            a = jnp.exp2(jnp.concatenate([bases[n][:, :half] - run_mid, bases[n][:, half:] - run], axis=1))
            if on_diagonal:
                a = jnp.where(masks[n], a, 0.0)
            acc = acc + _dot(a.astype(BF16), v_ref[pl.ds(ks, tk), :])
            run = run_mid + early_total[n]
        return acc, run

    carry = blocks([pl.multiple_of(i * tq, tq)],
                   (jnp.zeros((tq, HEAD_DIM), F32), jnp.zeros((tq, half), F32)), True)

    def run_groups(size, top, trips, carry):
        def group(step, carry):
            starts = [pl.multiple_of((top - 1 - size * step - s) * tk, tk) for s in range(size)]
            return blocks(starts, carry, False)
        return lax.fori_loop(0, trips, group, carry)

    carry = run_groups(per_trip, i, i // per_trip, carry)
    left = i % per_trip
    size = per_trip // 2
    while size >= 1:
        carry = run_groups(size, left, left // size, carry)
        left = left % size
        size //= 2
    acc, _ = carry
    o_ref[...] = acc.astype(o_ref.dtype)


def _sb_attention(zm, tq, tk, per_trip):
    t = zm.shape[0]
    assert tq == tk and tk % 256 == 0
    cq, ck, cv = A_Q // HEAD_DIM, A_K // HEAD_DIM, A_V // HEAD_DIM
    return pl.pallas_call(
        functools.partial(_sb_kernel, tq=tq, tk=tk, per_trip=per_trip),
        out_shape=jax.ShapeDtypeStruct((t, BRANCH_WIDTH), BF16),
        grid=(HEADS, t // tq),
        in_specs=[pl.BlockSpec((tq, HEAD_DIM), lambda h, i: (i, cq + h)),
                  pl.BlockSpec((t, HEAD_DIM), lambda h, i: (0, ck + h)),
                  pl.BlockSpec((t, HEAD_DIM), lambda h, i: (0, cv + h))],
        out_specs=pl.BlockSpec((tq, HEAD_DIM), lambda h, i: (i, h)),
        compiler_params=_cparams(2),
        name="sb_attention",
    )(zm, zm, zm)


def _gm_kernel(u_ref, v_ref, ng_ref, nb_ref, ws_ref, bs_ref, o_ref, *, n_chunk):
    u = jax.nn.gelu(u_ref[...].astype(F32))
    v = jax.nn.gelu(v_ref[...].astype(F32))
    v = _layer_norm(v, ng_ref[...], nb_ref[...])
    ii = lax.broadcasted_iota(jnp.int32, (GM_CHUNK, GM_CHUNK), 0)
    jj = lax.broadcasted_iota(jnp.int32, (GM_CHUNK, GM_CHUNK), 1)
    for g in range(HEADS):
        w = jnp.where(ii >= jj, ws_ref[g], 0.0).astype(BF16)
        bias = bs_ref[g]
        for c in range(n_chunk):
            rows = slice(c * GM_CHUNK, (c + 1) * GM_CHUNK)
            cols = slice(g * HEAD_DIM, (g + 1) * HEAD_DIM)
            mixed = _dot(w, v[rows, cols].astype(BF16)) + bias
            o_ref[rows, cols] = (u[rows, cols] * mixed).astype(o_ref.dtype)


def _gmlp(zm, norm_g, norm_b, w_s, b_s, tb):
    t = zm.shape[0]
    bias = jnp.broadcast_to(b_s[:, :, None], (HEADS, GM_CHUNK, HEAD_DIM))
    cu, cv = B_U // BRANCH_WIDTH, B_V // BRANCH_WIDTH
    return pl.pallas_call(
        functools.partial(_gm_kernel, n_chunk=tb // GM_CHUNK),
        out_shape=jax.ShapeDtypeStruct((t, BRANCH_WIDTH), BF16),
        grid=(t // tb,),
        in_specs=[pl.BlockSpec((tb, BRANCH_WIDTH), lambda i: (i, cu)),
                  pl.BlockSpec((tb, BRANCH_WIDTH), lambda i: (i, cv)),
                  pl.BlockSpec((1, BRANCH_WIDTH), lambda i: (0, 0)),
                  pl.BlockSpec((1, BRANCH_WIDTH), lambda i: (0, 0)),
                  pl.BlockSpec((HEADS, GM_CHUNK, GM_CHUNK), lambda i: (0, 0, 0)),
                  pl.BlockSpec((HEADS, GM_CHUNK, HEAD_DIM), lambda i: (0, 0, 0))],
        out_specs=pl.BlockSpec((tb, BRANCH_WIDTH), lambda i: (i, 0)),
        compiler_params=_cparams(1),
        name="gmlp",
    )(zm, zm, norm_g.reshape(1, -1), norm_b.reshape(1, -1), w_s, bias)


def _gdn_prep_kernel(q_ref, k_ref, v_ref, zs_ref, cw_ref, par_ref,
                     u_ref, w_ref, qe_ref, kd_ref, attn_ref, eg_ref, xs_ref, *, n_ch):
    c = GDN_CHUNK
    w3 = 3 * BRANCH_WIDTH
    rows_all = n_ch * c

    @pl.when(pl.program_id(0) == 0)
    def _():
        xs_ref[0:8, :] = jnp.zeros((8, w3), F32)

    xs_ref[8:8 + rows_all, 0:BRANCH_WIDTH] = q_ref[...].astype(F32)
    xs_ref[8:8 + rows_all, BRANCH_WIDTH:2 * BRANCH_WIDTH] = k_ref[...].astype(F32)
    xs_ref[8:8 + rows_all, 2 * BRANCH_WIDTH:w3] = v_ref[...].astype(F32)
    cw = cw_ref[...]
    y_all = xs_ref[5:5 + rows_all, :] * cw[0:1, :]
    for tap in range(1, GDN_CONV):
        y_all = y_all + xs_ref[5 + tap:5 + tap + rows_all, :] * cw[tap:tap + 1, :]
    xs_ref[0:8, :] = xs_ref[rows_all:rows_all + 8, :]
    y_all = _silu(y_all)

    zs_all = zs_ref[...]
    beta_full = _sigmoid(zs_all)
    g_full = -jnp.exp(par_ref[0:1, :]) * _softplus(zs_all + par_ref[1:2, :])

    ii = lax.broadcasted_iota(jnp.int32, (c, c), 0)
    jj = lax.broadcasted_iota(jnp.int32, (c, c), 1)
    tri_incl = jnp.where(ii >= jj, 1.0, 0.0).astype(BF16)
    ones_cc = jnp.ones((c, c), BF16)

    chains = []
    for ch, h in [(ch, h) for ch in range(n_ch) for h in range(HEADS)]:
        rows = slice(ch * c, (ch + 1) * c)
        y = y_all[rows]
        beta_all = beta_full[rows]
        g_all = g_full[rows]
        lanes = slice(h * HEAD_DIM, (h + 1) * HEAD_DIM)
        qh = y[:, h * HEAD_DIM:(h + 1) * HEAD_DIM]
        kh = y[:, BRANCH_WIDTH + h * HEAD_DIM:BRANCH_WIDTH + (h + 1) * HEAD_DIM]
        vh = y[:, 2 * BRANCH_WIDTH + h * HEAD_DIM:2 * BRANCH_WIDTH + (h + 1) * HEAD_DIM]
        qh = qh * lax.rsqrt(jnp.sum(qh * qh, axis=-1, keepdims=True) + 1e-6) * HEAD_DIM ** -0.5
        kh = kh * lax.rsqrt(jnp.sum(kh * kh, axis=-1, keepdims=True) + 1e-6)
        beta = beta_all[:, S_BETA + h:S_BETA + h + 1]
        g_b = jnp.broadcast_to(g_all[:, S_A + h:S_A + h + 1], (c, HEAD_DIM))
        chains.append(dict(ch=ch, h=h, rows=rows, lanes=lanes, qh=qh, kh=kh, vh=vh, beta=beta, g_b=g_b))

    for s in chains:
        s["gcol"] = _dot_sel_lhs(tri_incl, s["g_b"])
        s["grow"] = _dot_sel_lhs(ones_cc, jnp.where(ii <= jj, s["g_b"][:, :c], 0.0))
        s["kh16"] = s["kh"].astype(BF16)
        s["kb"] = s["kh"] * s["beta"]
        s["kk"] = _dot_nt(s["kb"].astype(BF16), s["kh16"])
        s["qk"] = _dot_nt(s["qh"].astype(BF16), s["kh16"])
    for s in chains:
        decay = jnp.where(ii >= jj, jnp.exp(s["gcol"][:, :c] - s["grow"]), 0.0)
        expg = jnp.exp(s["gcol"])
        g_last = s["gcol"][c - 1:c, :]
        rows, lanes = s["rows"], s["lanes"]
        qe_ref[rows, lanes] = (s["qh"] * expg).astype(BF16)
        kd_ref[rows, lanes] = (s["kh"] * jnp.exp(g_last - s["gcol"])).astype(BF16)
        attn_ref[s["h"], rows, :] = (s["qk"] * decay).astype(BF16)
        eg_ref[s["ch"], :, lanes] = jnp.broadcast_to(jnp.exp(g_last), (8, HEAD_DIM))
        s["p"] = -jnp.where(ii > jj, s["kk"] * decay, 0.0)
        s["x"] = jnp.concatenate([s["vh"] * s["beta"], s["kb"] * expg], axis=1)
    for r in range(6):
        for s in chains:
            s["x"] = s["x"] + _dot_f32(s["p"], s["x"])
            if r < 5:
                s["p"] = _dot_f32(s["p"], s["p"])
    for s in chains:
        u_ref[s["rows"], s["lanes"]] = s["x"][:, :HEAD_DIM]
        w_ref[s["rows"], s["lanes"]] = s["x"][:, HEAD_DIM:].astype(BF16)


def _gdn_scan_kernel(u_ref, w_ref, qe_ref, kd_ref, attn_ref, eg_ref, gate_ref, ng_ref, o_ref,
                     s_ref, *, n_ch):
    c = GDN_CHUNK

    @pl.when(pl.program_id(0) == 0)
    def _():
        s_ref[...] = jnp.zeros(s_ref.shape, F32)

    heads = range(HEADS)
    lanes = [slice(h * HEAD_DIM, (h + 1) * HEAD_DIM) for h in heads]
    state = [s_ref[h] for h in heads]
    for ch in range(n_ch):
        rows = slice(ch * c, (ch + 1) * c)
        s16 = [state[h].astype(BF16) for h in heads]
        v_new = [(u_ref[rows, lanes[h]] - _dot(w_ref[rows, lanes[h]], s16[h])).astype(BF16) for h in heads]
        state = [state[h] * eg_ref[ch, 0:1, lanes[h]] + _dot_tn(kd_ref[rows, lanes[h]], v_new[h])
                 for h in heads]
        out = [_dot(qe_ref[rows, lanes[h]], s16[h]) + _dot(attn_ref[h, rows, :], v_new[h]) for h in heads]
        for h in heads:
            o = out[h]
            o = o * lax.rsqrt(jnp.mean(o * o, axis=-1, keepdims=True) + NORM_EPS) * ng_ref[...]
            o_ref[rows, lanes[h]] = (o * _silu(gate_ref[rows, lanes[h]].astype(F32))).astype(o_ref.dtype)
    for h in heads:
        s_ref[h] = state[h]


def _gdn(zm, zs, conv_w, a_log, dt_bias, norm_g, prep_chunks, scan_chunks):
    t = zm.shape[0]
    c = GDN_CHUNK
    par = jnp.zeros((8, SMALL_COLS), F32)
    par = par.at[0, S_A:S_A + HEADS].set(a_log).at[1, S_A:S_A + HEADS].set(dt_bias)

    rp = prep_chunks * c
    blk = lambda col: pl.BlockSpec((rp, BRANCH_WIDTH), lambda n: (n, col // BRANCH_WIDTH))
    wide = pl.BlockSpec((rp, BRANCH_WIDTH), lambda n: (n, 0))
    u, w, qe, kd, attn, eg = pl.pallas_call(
        functools.partial(_gdn_prep_kernel, n_ch=prep_chunks),
        out_shape=(jax.ShapeDtypeStruct((t, BRANCH_WIDTH), F32),
                   jax.ShapeDtypeStruct((t, BRANCH_WIDTH), BF16),
                   jax.ShapeDtypeStruct((t, BRANCH_WIDTH), BF16),
                   jax.ShapeDtypeStruct((t, BRANCH_WIDTH), BF16),
                   jax.ShapeDtypeStruct((HEADS, t, c), BF16),
                   jax.ShapeDtypeStruct((t // c, 8, BRANCH_WIDTH), F32)),
        grid=(t // rp,),
        in_specs=[blk(C_Q), blk(C_K), blk(C_V),
                  pl.BlockSpec((rp, SMALL_COLS), lambda n: (n, 0)),
                  pl.BlockSpec((GDN_CONV, 3 * BRANCH_WIDTH), lambda n: (0, 0)),
                  pl.BlockSpec((8, SMALL_COLS), lambda n: (0, 0))],
        out_specs=(wide, wide, wide, wide,
                   pl.BlockSpec((HEADS, rp, c), lambda n: (0, n, 0)),
                   pl.BlockSpec((prep_chunks, 8, BRANCH_WIDTH), lambda n: (n, 0, 0))),
        scratch_shapes=[pltpu.VMEM((rp + 8, 3 * BRANCH_WIDTH), F32)],
        compiler_params=_cparams(1),
        name="gdn_chunk_prep",
    )(zm, zm, zm, zs, conv_w, par)

    rs = scan_chunks * c
    wide = pl.BlockSpec((rs, BRANCH_WIDTH), lambda n: (n, 0))
    return pl.pallas_call(
        functools.partial(_gdn_scan_kernel, n_ch=scan_chunks),
        out_shape=jax.ShapeDtypeStruct((t, BRANCH_WIDTH), BF16),
        grid=(t // rs,),
        in_specs=[wide, wide, wide, wide,
                  pl.BlockSpec((HEADS, rs, c), lambda n: (0, n, 0)),
                  pl.BlockSpec((scan_chunks, 8, BRANCH_WIDTH), lambda n: (n, 0, 0)),
                  pl.BlockSpec((rs, BRANCH_WIDTH), lambda n: (n, C_G // BRANCH_WIDTH)),
                  pl.BlockSpec((1, HEAD_DIM), lambda n: (0, 0))],
        out_specs=wide,
        scratch_shapes=[pltpu.VMEM((HEADS, HEAD_DIM, HEAD_DIM), F32)],
        compiler_params=_cparams(1),
        name="gdn_state_scan",
    )(u, w, qe, kd, attn, eg, zm, norm_g.reshape(1, -1))


def _ml_kernel(q_ref, k_ref, v_ref, og_ref, zs_ref, par_ref, ng_ref, o_ref,
               c_ref, n_ref, m_ref, *, n_ch):
    c = ML_CHUNK

    @pl.when(pl.program_id(0) == 0)
    def _():
        c_ref[...] = jnp.zeros(c_ref.shape, F32)
        n_ref[...] = jnp.zeros(n_ref.shape, F32)
        m_ref[...] = jnp.zeros(m_ref.shape, F32)

    zz_all = zs_ref[...] + par_ref[2:3, :]
    logf_all = -_softplus(-zz_all)

    ii = lax.broadcasted_iota(jnp.int32, (c, c), 0)
    jj = lax.broadcasted_iota(jnp.int32, (c, c), 1)
    tri_incl = jnp.where(ii >= jj, 1.0, 0.0).astype(BF16)
    ones_cc = jnp.ones((c, c), BF16)
    ones_cl = jnp.ones((c, HEAD_DIM), BF16)

    pairs = []
    for ch in range(n_ch):
        rows = slice(ch * c, (ch + 1) * c)
        for h in range(HEADS):
            lanes = slice(h * HEAD_DIM, (h + 1) * HEAD_DIM)
            qk_lanes = slice(h * ML_QK, (h + 1) * ML_QK)
            pairs.append(dict(
                ch=ch, h=h, rows=rows, lanes=lanes,
                qs=(q_ref[rows, qk_lanes].astype(F32) * ML_QK ** -0.5).astype(BF16),
                kh=k_ref[rows, qk_lanes], vh=v_ref[rows, lanes],
                i_b=jnp.broadcast_to(zz_all[rows, S_I + h:S_I + h + 1], (c, HEAD_DIM)),
                lf_b=jnp.broadcast_to(logf_all[rows, S_F + h:S_F + h + 1], (c, HEAD_DIM))))
    for s in pairs:
        s["bcol"] = _dot_sel_lhs(tri_incl, s["lf_b"])
        s["qk"] = _dot_nt(s["qs"], s["kh"])
    for s in pairs:
        rv = s["i_b"] - s["bcol"]
        s["rrow"] = _dot_sel_lhs(ones_cc, jnp.where(ii == jj, rv[:, :c], 0.0))
        s["b_last"] = s["bcol"][c - 1:c, :]
        s["log_kv"] = s["b_last"] - s["bcol"] + s["i_b"]
        s["kv_max"] = jnp.max(s["log_kv"], axis=0, keepdims=True)
    for s in pairs:
        s["log_intra"] = jnp.where(ii >= jj, s["bcol"][:, :c] + s["rrow"], -jnp.inf)
        s["intra_max"] = jnp.max(s["log_intra"], axis=1, keepdims=True)

    heads = range(HEADS)
    c_mat = [c_ref[h] for h in heads]
    n_b = [n_ref[h] for h in heads]
    m_prev = [m_ref[h][0:1, :] for h in heads]
    for ch in range(n_ch):
        cur = pairs[ch * HEADS:(ch + 1) * HEADS]
        m_out = [jnp.maximum(s["bcol"] + m_prev[s["h"]], s["intra_max"]) for s in cur]
        w_inter = [jnp.exp(s["bcol"] + m_prev[s["h"]] - m_out[s["h"]]) for s in cur]
        sc = [jnp.exp(s["log_intra"] - m_out[s["h"]][:, :c]) * s["qk"] for s in cur]
        num = [w_inter[h] * _dot(cur[h]["qs"], c_mat[h].astype(BF16)) + _dot(sc[h].astype(BF16), cur[h]["vh"])
               for h in heads]
        den = [w_inter[h] * _dot(cur[h]["qs"], n_b[h].astype(BF16)) + jnp.sum(sc[h], axis=1, keepdims=True)
               for h in heads]
        m_new = [jnp.maximum(s["b_last"] + m_prev[s["h"]], s["kv_max"]) for s in cur]
        kw = [(s["kh"].astype(F32) * jnp.exp(s["log_kv"] - m_new[s["h"]])[:, :ML_QK]).astype(BF16) for s in cur]
        decay = [jnp.exp(s["b_last"] + m_prev[s["h"]] - m_new[s["h"]]) for s in cur]
        c_mat = [decay[h] * c_mat[h] + _dot_tn(kw[h], cur[h]["vh"]) for h in heads]
        n_b = [decay[h] * n_b[h] + _dot_tn(kw[h], ones_cl) for h in heads]
        m_prev = m_new
        for h in heads:
            s = cur[h]
            hid = num[h] / jnp.maximum(jnp.abs(den[h]), jnp.exp(-m_out[h]))
            hid = hid * lax.rsqrt(jnp.mean(hid * hid, axis=-1, keepdims=True) + NORM_EPS)
            hid = hid * ng_ref[:, s["lanes"]]
            gate = _sigmoid(og_ref[s["rows"], s["lanes"]].astype(F32))
            o_ref[s["rows"], s["lanes"]] = (hid * gate).astype(o_ref.dtype)
    for h in heads:
        c_ref[h] = c_mat[h]
        n_ref[h] = n_b[h]
        m_ref[h] = jnp.broadcast_to(m_prev[h], (8, HEAD_DIM))


def _mlstm(zm, zs, i_bias, f_bias, norm_g, n_ch):
    t = zm.shape[0]
    c = ML_CHUNK * n_ch
    qk_w = HEADS * ML_QK
    par = jnp.zeros((8, SMALL_COLS), F32)
    par = par.at[2, S_I:S_I + HEADS].set(i_bias).at[2, S_F:S_F + HEADS].set(f_bias)
    return pl.pallas_call(
        functools.partial(_ml_kernel, n_ch=n_ch),
        out_shape=jax.ShapeDtypeStruct((t, BRANCH_WIDTH), BF16),
        grid=(t // c,),
        in_specs=[pl.BlockSpec((c, qk_w), lambda n: (n, D_Q // qk_w)),
                  pl.BlockSpec((c, qk_w), lambda n: (n, D_K // qk_w)),
                  pl.BlockSpec((c, BRANCH_WIDTH), lambda n: (n, D_V // BRANCH_WIDTH)),
                  pl.BlockSpec((c, BRANCH_WIDTH), lambda n: (n, D_O // BRANCH_WIDTH)),
                  pl.BlockSpec((c, SMALL_COLS), lambda n: (n, 0)),
                  pl.BlockSpec((8, SMALL_COLS), lambda n: (0, 0)),
                  pl.BlockSpec((1, BRANCH_WIDTH), lambda n: (0, 0))],
        out_specs=pl.BlockSpec((c, BRANCH_WIDTH), lambda n: (n, 0)),
        scratch_shapes=[pltpu.VMEM((HEADS, ML_QK, HEAD_DIM), F32),
                        pltpu.VMEM((HEADS, ML_QK, HEAD_DIM), F32),
                        pltpu.VMEM((HEADS, 8, HEAD_DIM), F32)],
        compiler_params=_cparams(1),
        name="mlstm",
    )(zm, zm, zm, zm, zs, par, norm_g.reshape(1, -1))


def _merge_kernel(a_ref, b_ref, c_ref, d_ref, g0_ref, g1_ref, g2_ref, g3_ref, wb_ref, o_ref):
    acc = None
    for g, (br, gr) in enumerate(((a_ref, g0_ref), (b_ref, g1_ref), (c_ref, g2_ref), (d_ref, g3_ref))):
        term = _sigmoid(gr[...].astype(F32)) * _dot(br[...], wb_ref[g])
        acc = term if acc is None else acc + term
    o_ref[...] = acc.astype(o_ref.dtype)


def _merge(outs, zm, wb16, tm):
    t = zm.shape[0]
    branch = pl.BlockSpec((tm, BRANCH_WIDTH), lambda i: (i, 0))
    gate = lambda g: pl.BlockSpec((tm, D_MODEL), lambda i: (i, g))
    return pl.pallas_call(
        _merge_kernel,
        out_shape=jax.ShapeDtypeStruct((t, D_MODEL), BF16),
        grid=(t // tm,),
        in_specs=[branch] * 4 + [gate(0), gate(1), gate(2), gate(3),
                                 pl.BlockSpec((N_BRANCHES, BRANCH_WIDTH, D_MODEL), lambda i: (0, 0, 0))],
        out_specs=pl.BlockSpec((tm, D_MODEL), lambda i: (i, 0)),
        compiler_params=_cparams(1),
        name="merge_branches",
    )(*outs, zm, zm, zm, zm, wb16)


def _proj_ln_kernel(a_ref, w_ref, x_ref, g_ref, b_ref, o_ref):
    y = ALPHA * x_ref[...] + _dot(a_ref[...], w_ref[...])
    o_ref[...] = _layer_norm(y, g_ref[...], b_ref[...])


def _proj_ln(a, w16, x, g, b, tm):
    t, k = a.shape
    return pl.pallas_call(
        _proj_ln_kernel,
        out_shape=jax.ShapeDtypeStruct((t, D_MODEL), F32),
        grid=(t // tm,),
        in_specs=[pl.BlockSpec((tm, k), lambda i: (i, 0)),
                  pl.BlockSpec((k, D_MODEL), lambda i: (0, 0)),
                  pl.BlockSpec((tm, D_MODEL), lambda i: (i, 0)),
                  pl.BlockSpec((1, D_MODEL), lambda i: (0, 0)),
                  pl.BlockSpec((1, D_MODEL), lambda i: (0, 0))],
        out_specs=pl.BlockSpec((tm, D_MODEL), lambda i: (i, 0)),
        compiler_params=_cparams(1),
        name="out_proj_layernorm",
    )(a, w16, x, g.reshape(1, -1), b.reshape(1, -1))


def _router_kernel(x_ref, wr_ref, bias_ref, idx_ref, w_ref, rank_ref, cnt_ref, *, tm):
    @pl.when(pl.program_id(0) == 0)
    def _():
        cnt_ref[...] = jnp.zeros(cnt_ref.shape, F32)

    logits = _dot_f32_nt(wr_ref[...], x_ref[...])
    scores = _sigmoid(logits)
    biased = scores + bias_ref[...]
    neg = -jnp.inf

    b3 = biased.reshape(N_GROUPS, GROUP_SIZE, tm)
    pos = lax.broadcasted_iota(jnp.int32, (N_GROUPS, GROUP_SIZE, tm), 1)
    m1 = jnp.max(b3, axis=1, keepdims=True)
    first = jnp.min(jnp.where(b3 == m1, pos, GROUP_SIZE), axis=1, keepdims=True)
    m2 = jnp.max(jnp.where(pos == first, neg, b3), axis=1, keepdims=True)
    gscore = m1 + m2

    gidx = lax.broadcasted_iota(jnp.int32, (N_GROUPS, 1, tm), 0)
    gsel = jnp.zeros((N_GROUPS, 1, tm), F32)
    for _ in range(TOPK_GROUPS):
        gm = jnp.max(gscore, axis=0, keepdims=True)
        gfirst = jnp.min(jnp.where(gscore == gm, gidx, N_GROUPS), axis=0, keepdims=True)
        hit = gidx == gfirst
        gsel = jnp.where(hit, 1.0, gsel)
        gscore = jnp.where(hit, neg, gscore)

    allowed = jnp.broadcast_to(gsel, (N_GROUPS, GROUP_SIZE, tm)) > 0.0
    masked = jnp.where(allowed, b3, neg).reshape(N_EXPERTS, tm)
    eidx = lax.broadcasted_iota(jnp.int32, (N_EXPERTS, tm), 0)
    idx_rows, sel_rows, hits = [], [], []
    for _ in range(TOP_K):
        em = jnp.max(masked, axis=0, keepdims=True)
        efirst = jnp.min(jnp.where(masked == em, eidx, N_EXPERTS), axis=0, keepdims=True)
        hit = eidx == efirst
        hits.append(hit)
        idx_rows.append(efirst)
        sel_rows.append(jnp.sum(jnp.where(hit, scores, 0.0), axis=0, keepdims=True))
        masked = jnp.where(hit, neg, masked)
    total = sel_rows[0]
    for r in sel_rows[1:]:
        total = total + r
    zero_i = jnp.zeros((1, tm), jnp.int32)
    zero_f = jnp.zeros((1, tm), F32)
    idx_ref[...] = jnp.concatenate(idx_rows + [zero_i, zero_i], axis=0)
    w_ref[...] = jnp.concatenate([r / total * ROUTED_SCALE for r in sel_rows] + [zero_f, zero_f], axis=0)

    chosen = jnp.zeros((N_EXPERTS, tm), F32)
    for hit in hits:
        chosen = jnp.where(hit, 1.0, chosen)
    chosen16 = chosen.astype(BF16)
    earlier = (lax.broadcasted_iota(jnp.int32, (tm, tm), 0)
               < lax.broadcasted_iota(jnp.int32, (tm, tm), 1))
    cnt = cnt_ref[...]
    before = _dot(chosen16, jnp.where(earlier, 1.0, 0.0).astype(BF16)) + jnp.tile(cnt, (1, tm // HEAD_DIM))
    ranks = [jnp.sum(jnp.where(hit, before, 0.0), axis=0, keepdims=True).astype(jnp.int32) for hit in hits]
    rank_ref[...] = jnp.concatenate(ranks + [zero_i, zero_i], axis=0)
    cnt_ref[...] = cnt + _dot(chosen16, jnp.ones((tm, HEAD_DIM), BF16))


def _dot_f32_nt(a, b):
    ah, al = _split2(a)
    bh, bl = _split2(b)
    return _dot_nt(ah, bh) + _dot_nt(ah, bl) + _dot_nt(al, bh)


def _router(x, w_router, router_bias, tm):
    t = x.shape[0]
    return pl.pallas_call(
        functools.partial(_router_kernel, tm=tm),
        out_shape=(jax.ShapeDtypeStruct((8, t), jnp.int32), jax.ShapeDtypeStruct((8, t), F32),
                   jax.ShapeDtypeStruct((8, t), jnp.int32),
                   jax.ShapeDtypeStruct((N_EXPERTS, HEAD_DIM), F32)),
        grid=(t // tm,),
        in_specs=[pl.BlockSpec((tm, D_MODEL), lambda i: (i, 0)),
                  pl.BlockSpec((N_EXPERTS, D_MODEL), lambda i: (0, 0)),
                  pl.BlockSpec((N_EXPERTS, 1), lambda i: (0, 0))],
        out_specs=(pl.BlockSpec((8, tm), lambda i: (0, i)), pl.BlockSpec((8, tm), lambda i: (0, i)),
                   pl.BlockSpec((8, tm), lambda i: (0, i)),
                   pl.BlockSpec((N_EXPERTS, HEAD_DIM), lambda i: (0, 0))),
        compiler_params=_cparams(1),
        name="router_topk",
    )(x, w_router.T, router_bias.reshape(N_EXPERTS, 1))


def _dispatch_kernel(pos_ref, x_ref, xs_hbm, sem, *, td):
    for r in range(td):
        for k in range(TOP_K):
            pltpu.make_async_copy(x_ref.at[pl.ds(r, 1), :],
                                  xs_hbm.at[pl.ds(pos_ref[0, r * TOP_K + k], 1), :],
                                  sem).start(priority=k % 2)
    for _ in range(TOP_K):
        pltpu.make_async_copy(x_ref, xs_hbm.at[pl.ds(0, td), :], sem).wait()


def _dispatch(x, pos, td):
    t = x.shape[0]
    return pl.pallas_call(
        functools.partial(_dispatch_kernel, td=td),
        out_shape=jax.ShapeDtypeStruct((t * TOP_K, D_MODEL), F32),
        grid=(t // td,),
        in_specs=[pl.BlockSpec((None, 1, td * TOP_K), lambda i: (i, 0, 0), memory_space=pltpu.SMEM),
                  pl.BlockSpec((td, D_MODEL), lambda i: (i, 0))],
        out_specs=pl.BlockSpec(memory_space=pl.ANY),
        scratch_shapes=[pltpu.SemaphoreType.DMA],
        compiler_params=_cparams(1),
        name="moe_dispatch",
    )(pos.reshape(t // td, 1, td * TOP_K), x)


def _expert_kernel(n_ref, tile_ref, exp_ref, nxt_ref, par_ref, lo_ref, hi_ref,
                   x_ref, wg_hbm, wu_hbm, wd_hbm, y_ref,
                   wgbuf, wubuf, wdbuf, wg16, wu16, wd16, wsem, *, tm, layer):
    j = pl.program_id(0)
    prev = jnp.maximum(j - 1, 0)
    first = (j == 0) | (tile_ref[prev] != tile_ref[j])
    expert = exp_ref[j]
    slot = par_ref[j]

    def copies(ex, s):
        return [pltpu.make_async_copy(hbm.at[layer, ex], buf.at[s], wsem.at[s])
                for hbm, buf in ((wg_hbm, wgbuf), (wu_hbm, wubuf), (wd_hbm, wdbuf))]

    @pl.when(j < n_ref[0])
    def _():
        @pl.when((j == 0) | (exp_ref[prev] != expert))
        def _():
            @pl.when(j == 0)
            def _():
                for cp in copies(expert, slot):
                    cp.start(priority=1)

            @pl.when(nxt_ref[j] != expert)
            def _():
                for cp in copies(nxt_ref[j], 1 - slot):
                    cp.start(priority=1)

            for cp in copies(expert, slot):
                cp.wait()
            wg16[...] = wgbuf[slot].astype(BF16)
            wu16[...] = wubuf[slot].astype(BF16)
            wd16[...] = wdbuf[slot].astype(BF16)

        x = x_ref[...].astype(BF16)
        hidden = _silu(_dot(x, wg16[...])) * _dot(x, wu16[...])
        y = _dot(hidden.astype(BF16), wd16[...])

        @pl.when(first)
        def _():
            y_ref[...] = y

        @pl.when(jnp.logical_not(first))
        def _():
            row = lax.broadcasted_iota(jnp.int32, (tm, 1), 0)
            mine = (row >= lo_ref[j]) & (row < hi_ref[j])
            y_ref[...] = jnp.where(mine, y, y_ref[...])


def _experts(xs, sched, w_gate, w_up, w_down, layer, tm):
    n_max = sched[1].shape[0]
    tile_of = lambda j, n, tl, *rest: tl[j]
    hbm = pl.BlockSpec(memory_space=pl.ANY)
    up_shape, down_shape = (D_MODEL, EXPERT_DIM), (EXPERT_DIM, D_MODEL)
    return pl.pallas_call(
        functools.partial(_expert_kernel, tm=tm, layer=layer),
        out_shape=jax.ShapeDtypeStruct(xs.shape, F32),
        grid_spec=pltpu.PrefetchScalarGridSpec(
            num_scalar_prefetch=len(sched),
            grid=(n_max,),
            in_specs=[pl.BlockSpec((tm, D_MODEL), lambda *a: (tile_of(*a), 0)), hbm, hbm, hbm],
            out_specs=pl.BlockSpec((tm, D_MODEL), lambda *a: (tile_of(*a), 0)),
            scratch_shapes=[pltpu.VMEM((2,) + up_shape, F32), pltpu.VMEM((2,) + up_shape, F32),
                            pltpu.VMEM((2,) + down_shape, F32),
                            pltpu.VMEM(up_shape, BF16), pltpu.VMEM(up_shape, BF16),
                            pltpu.VMEM(down_shape, BF16),
                            pltpu.SemaphoreType.DMA((2,))]),
        compiler_params=_cparams(1),
        name="routed_experts",
    )(*sched, xs, w_gate, w_up, w_down)


def _expert_schedule(idx, rank, counts, t, tm):
    i32 = jnp.int32
    n_rows = t * TOP_K
    n_tiles = n_rows // tm
    n_max = n_tiles + N_EXPERTS - 1
    end = jnp.cumsum(counts)
    start = end - counts
    pos = (jnp.take(start, idx) + rank).reshape(-1).astype(i32)
    first_tile = start // tm
    items_per_e = jnp.where(counts > 0, (end - 1) // tm - first_tile + 1, 0)
    item_end = jnp.cumsum(items_per_e)
    n_items = item_end[-1]
    j = jnp.arange(n_max, dtype=i32)
    e = jnp.minimum(jnp.sum((item_end[None, :] <= j[:, None]).astype(i32), axis=1), N_EXPERTS - 1)
    tile = first_tile[e] + j - (item_end[e] - items_per_e[e])
    lo = jnp.maximum(start[e], tile * tm) - tile * tm
    hi = jnp.minimum(end[e], (tile + 1) * tm) - tile * tm
    valid = j < n_items
    last = jnp.maximum(n_items - 1, 0)
    pick = lambda a: jnp.where(valid, a, a[last]).astype(i32)
    ids = jnp.arange(N_EXPERTS, dtype=i32)
    later = (ids[None, :] > ids[:, None]) & (counts[None, :] > 0)
    nxt = jnp.min(jnp.where(later, ids[None, :], N_EXPERTS), axis=1)
    nxt = jnp.where(nxt == N_EXPERTS, ids, nxt)
    parity = (jnp.cumsum((counts > 0).astype(i32)) - 1) % 2
    return pos, (n_items.reshape(1).astype(i32), pick(tile), pick(e), pick(nxt[e]), pick(parity[e]),
                 pick(lo), pick(hi))


def _combine_kernel(pos_ref, pos_next_ref, x_ref, w_ref, ys_hbm, sg_ref, su_ref, sd_ref, g_ref, b_ref,
                    o_ref, o16_ref, ybuf, sem, *, tt, n_steps):
    i = pl.program_id(0)
    slot = lax.rem(i, 2)

    def gather(idx_ref, s):
        for r in range(tt):
            for k in range(TOP_K):
                pltpu.make_async_copy(ys_hbm.at[pl.ds(idx_ref[0, r * TOP_K + k], 1), :],
                                      ybuf.at[s, k, pl.ds(r, 1), :], sem.at[s]).start()

    @pl.when(i == 0)
    def _():
        gather(pos_ref, 0)

    @pl.when(i + 1 < n_steps)
    def _():
        gather(pos_next_ref, 1 - slot)

    x = x_ref[...]
    x16 = x.astype(BF16)
    hidden = _silu(_dot(x16, sg_ref[...])) * _dot(x16, su_ref[...])
    acc = _dot(hidden.astype(BF16), sd_ref[...])
    for k in range(TOP_K):
        pltpu.make_async_copy(ys_hbm.at[pl.ds(0, tt), :], ybuf.at[slot, k], sem.at[slot]).wait()
    w = w_ref[...]
    for k in range(TOP_K):
        acc = acc + ybuf[slot, k] * w[:, k:k + 1]
    out = _layer_norm(ALPHA * x + acc, g_ref[...], b_ref[...])
    o_ref[...] = out
    o16_ref[...] = out.astype(BF16)


def _combine(x, w_tok, pos, ys, sg16, su16, sd16, g, b, tt):
    t = x.shape[0]
    nb = t // tt
    pos3 = pos.reshape(nb, 1, tt * TOP_K)
    tile = pl.BlockSpec((tt, D_MODEL), lambda i: (i, 0))
    vec = pl.BlockSpec((1, D_MODEL), lambda i: (0, 0))
    idx_spec = lambda f: pl.BlockSpec((None, 1, tt * TOP_K), f, memory_space=pltpu.SMEM)
    return pl.pallas_call(
        functools.partial(_combine_kernel, tt=tt, n_steps=nb),
        out_shape=(jax.ShapeDtypeStruct((t, D_MODEL), F32), jax.ShapeDtypeStruct((t, D_MODEL), BF16)),
        grid=(nb,),
        in_specs=[idx_spec(lambda i: (i, 0, 0)), idx_spec(lambda i: (jnp.minimum(i + 1, nb - 1), 0, 0)),
                  tile, pl.BlockSpec((tt, 8), lambda i: (i, 0)),
                  pl.BlockSpec(memory_space=pl.ANY),
                  pl.BlockSpec((D_MODEL, SHARED_DIM), lambda i: (0, 0)),
                  pl.BlockSpec((D_MODEL, SHARED_DIM), lambda i: (0, 0)),
                  pl.BlockSpec((SHARED_DIM, D_MODEL), lambda i: (0, 0)), vec, vec],
        out_specs=(tile, tile),
        scratch_shapes=[pltpu.VMEM((2, TOP_K, tt, D_MODEL), F32), pltpu.SemaphoreType.DMA((2,))],
        compiler_params=_cparams(1),
        name="moe_combine_layernorm",
    )(pos3, pos3, x, w_tok, ys, sg16, su16, sd16, g.reshape(1, -1), b.reshape(1, -1))


def _tile(t, pref):
    return min(t, pref)


def _in_proj(x16, w_in):
    t = x16.shape[0]
    tm = _tile(t, 1024)
    wide = lambda r, tn, name: _matmul_nt(x16, w_in[:, r[0]:r[1]].T.astype(BF16), BF16, tm, tn, name)
    z_abc = wide(ABC_COLS, 1536, "in_proj_abc")
    z_d = wide(D_COLS, 1536, "in_proj_d")
    z_gate = wide(GATE_COLS, 2048, "in_proj_gate")
    pad = jnp.zeros((D_MODEL, SMALL_COLS - sum(b - a for a, b in SMALL_SRC)), F32)
    w_small = jnp.concatenate([w_in[:, a:b] for a, b in SMALL_SRC] + [pad], axis=1).astype(BF16)
    zs = _matmul(x16, w_small, F32, tm, SMALL_COLS, "in_proj_small")
    return z_abc, z_d, z_gate, zs


def _mixer(x, x16, p):
    t = x.shape[0]
    zm, z_d, z_gate, zs = _in_proj(x16, p["w_in"])
    out_a = _sb_attention(zm, 256, 256, 8)
    out_b = _gmlp(zm, p["gm_norm_g"], p["gm_norm_b"], p["gm_w_s"], p["gm_b_s"], _tile(t, 512))
    out_c = _gdn(zm, zs, p["gdn_conv_w"], p["gdn_a_log"], p["gdn_dt_bias"], p["gdn_norm_g"], 4, 4)
    out_d = _mlstm(z_d, zs, p["ml_i_bias"], p["ml_f_bias"], p["ml_norm_g"], 2)
    merged = _merge((out_a, out_b, out_c, out_d), z_gate, p["w_branch"].astype(BF16), _tile(t, 256))
    return _proj_ln(merged, p["w_out"].astype(BF16), x, p["ln1_g"], p["ln1_b"], _tile(t, 256))


def _moe(x, p, tm):
    t = x.shape[0]
    idx_t, w_t, rank_t, cnt = _router(x, p["w_router"], p["router_bias"], _tile(t, 512))
    w_tok = w_t.T
    counts = cnt[:, 0].astype(jnp.int32)
    pos, sched = _expert_schedule(idx_t[:TOP_K].T, rank_t[:TOP_K].T, counts, t, tm)
    xs = _dispatch(x, pos, _tile(t, 256))
    ys = _experts(xs, sched, p["w_gate_all"], p["w_up_all"], p["w_down_all"], p["layer"], tm)
    return _combine(x, w_tok, pos, ys, p["ws_gate"].astype(BF16), p["ws_up"].astype(BF16),
                    p["ws_down"].astype(BF16), p["ln2_g"], p["ln2_b"], _tile(t, 128))


def _layer(x, x16, p, tm_expert):
    x1 = _mixer(x, x16, p)
    return _moe(x1, p, tm_expert)


_STACKED = ("w_gate", "w_up", "w_down")


def _layer_params(params, l):
    p = {k: v[l] for k, v in params.items() if k not in _STACKED}
    p.update({k + "_all": params[k] for k in _STACKED})
    p["layer"] = l
    return p


def kernel(x, w_in, gm_norm_g, gm_norm_b, gm_w_s, gm_b_s, gdn_conv_w, gdn_a_log, gdn_dt_bias, gdn_norm_g, ml_i_bias, ml_f_bias, ml_norm_g, w_branch, w_out, ln1_g, ln1_b, w_router, router_bias, w_gate, w_up, w_down, ws_gate, ws_up, ws_down, ln2_g, ln2_b):
    params = dict(w_in=w_in, gm_norm_g=gm_norm_g, gm_norm_b=gm_norm_b, gm_w_s=gm_w_s, gm_b_s=gm_b_s,
                  gdn_conv_w=gdn_conv_w, gdn_a_log=gdn_a_log, gdn_dt_bias=gdn_dt_bias,
                  gdn_norm_g=gdn_norm_g, ml_i_bias=ml_i_bias, ml_f_bias=ml_f_bias, ml_norm_g=ml_norm_g,
                  w_branch=w_branch, w_out=w_out, ln1_g=ln1_g, ln1_b=ln1_b, w_router=w_router,
                  router_bias=router_bias, w_gate=w_gate, w_up=w_up, w_down=w_down, ws_gate=ws_gate,
                  ws_up=ws_up, ws_down=ws_down, ln2_g=ln2_g, ln2_b=ln2_b)
    b, t, d = x.shape
    h = x.reshape(b * t, d)
    h16 = h.astype(BF16)
    for l in range(DEPTH):
        h, h16 = _layer(h, h16, _layer_params(params, l), 256)
    return h.reshape(b, t, d)
```

```python
import functools

import jax
import jax.numpy as jnp
from jax import lax
from jax.experimental import pallas as pl
from jax.experimental.pallas import tpu as pltpu

F32 = jnp.float32
BF16 = jnp.bfloat16

D_MODEL = 2048
N_BRANCHES = 4
BRANCH_WIDTH = 512
HEADS = 4
HEAD_DIM = 128
GM_CHUNK = 128
GDN_CHUNK = 64
GDN_CONV = 4
ML_CHUNK = 64
ML_QK = 64
N_EXPERTS = 64
N_GROUPS = 8
GROUP_SIZE = 8
TOPK_GROUPS = 4
TOP_K = 6
EXPERT_DIM = 512
SHARED_DIM = 512
ROUTED_SCALE = 2.5
DEPTH = 2
ALPHA = (2 * DEPTH) ** 0.25
NORM_EPS = 1e-5
LOG2E = 1.4426950408889634

ABC_COLS = (0, 4608)
A_Q, A_K, A_V = 0, 512, 1024
B_U, B_V = 1536, 2048
C_Q, C_K, C_V, C_G = 2560, 3072, 3584, 4096
D_COLS = (4616, 6152)
D_Q, D_K, D_V, D_O = 0, 256, 512, 1024
GATE_COLS = (6160, 6160 + N_BRANCHES * D_MODEL)
SMALL_SRC = ((4608, 4616), (6152, 6160))
S_BETA, S_A, S_I, S_F = 0, 4, 8, 12
SMALL_COLS = 128

VMEM_LIMIT = 56 * 1024 * 1024


def _cparams(n_axes):
    return pltpu.CompilerParams(
        dimension_semantics=("arbitrary",) * n_axes, vmem_limit_bytes=VMEM_LIMIT)


def _dot(a, b):
    return jnp.dot(a, b, preferred_element_type=F32)


def _dot_nt(a, b):
    return lax.dot_general(a, b, (((1,), (1,)), ((), ())), preferred_element_type=F32)


def _dot_tn(a, b):
    return lax.dot_general(a, b, (((0,), (0,)), ((), ())), preferred_element_type=F32)


def _split2(a):
    hi = a.astype(BF16)
    lo = (a - hi.astype(F32)).astype(BF16)
    return hi, lo


def _split3(a):
    hi = a.astype(BF16)
    r = a - hi.astype(F32)
    mid = r.astype(BF16)
    lo = (r - mid.astype(F32)).astype(BF16)
    return hi, mid, lo


def _dot_sel_lhs(sel, a):
    sel3 = jnp.concatenate([sel, sel, sel], axis=1)
    return _dot(sel3, jnp.concatenate(_split3(a), axis=0))


def _dot_sel_rhs(a, sel):
    hi, lo = _split2(a)
    return _dot(hi, sel) + _dot(lo, sel)


def _dot_f32(a, b):
    ah, al = _split2(a)
    bh, bl = _split2(b)
    return _dot(jnp.concatenate([ah, ah, al], axis=1), jnp.concatenate([bh, bl, bh], axis=0))


def _softplus(x):
    return jnp.maximum(x, 0.0) + jnp.log1p(jnp.exp(-jnp.abs(x)))


def _sigmoid(x):
    return 1.0 / (1.0 + jnp.exp(-x))


def _silu(x):
    return x * _sigmoid(x)


def _layer_norm(x, g, b):
    mu = jnp.mean(x, axis=-1, keepdims=True)
    xc = x - mu
    var = jnp.mean(xc * xc, axis=-1, keepdims=True)
    return xc * lax.rsqrt(var + NORM_EPS) * g + b


def _mm_kernel(x_ref, w_ref, o_ref):
    o_ref[...] = _dot(x_ref[...], w_ref[...]).astype(o_ref.dtype)


def _matmul(x, w, out_dtype, tm, tn, name):
    m, k = x.shape
    n = w.shape[1]
    return pl.pallas_call(
        _mm_kernel,
        out_shape=jax.ShapeDtypeStruct((m, n), out_dtype),
        grid=(n // tn, m // tm),
        in_specs=[pl.BlockSpec((tm, k), lambda j, i: (i, 0)),
                  pl.BlockSpec((k, tn), lambda j, i: (0, j))],
        out_specs=pl.BlockSpec((tm, tn), lambda j, i: (i, j)),
        compiler_params=_cparams(2),
        name=name,
    )(x, w)


def _mm_nt_kernel(x_ref, wt_ref, o_ref):
    o_ref[...] = _dot_nt(x_ref[...], wt_ref[...]).astype(o_ref.dtype)


def _matmul_nt(x, wt, out_dtype, tm, tn, name):
    m, k = x.shape
    n = wt.shape[0]
    return pl.pallas_call(
        _mm_nt_kernel,
        out_shape=jax.ShapeDtypeStruct((m, n), out_dtype),
        grid=(n // tn, m // tm),
        in_specs=[pl.BlockSpec((tm, k), lambda j, i: (i, 0)),
                  pl.BlockSpec((tn, k), lambda j, i: (j, 0))],
        out_specs=pl.BlockSpec((tm, tn), lambda j, i: (i, j)),
        compiler_params=_cparams(2),
        name=name,
    )(x, wt)


def _sb_kernel(q_ref, k_ref, v_ref, o_ref, *, tq, tk, per_trip):
    i = pl.program_id(1)
    q = (q_ref[...].astype(F32) * (HEAD_DIM ** -0.5 * LOG2E)).astype(BF16)
    half = tk // 2
    row = i * tq + lax.broadcasted_iota(jnp.int32, (tq, tk), 0)
    col0 = lax.broadcasted_iota(jnp.int32, (tq, tk), 1)
    jj = lax.broadcasted_iota(jnp.int32, (tk, tk), 0)
    ss = lax.broadcasted_iota(jnp.int32, (tk, tk), 1)
    sel = jnp.where((jj >= ss) & ((jj < half) == (ss < half)), 1.0, 0.0).astype(BF16)

    def blocks(starts, carry, on_diagonal):
        zs = [_dot_nt(q, k_ref[pl.ds(ks, tk), :]) for ks in starts]
        sps = [jnp.maximum(z, 0.0) + jnp.log(1.0 + jnp.exp2(-jnp.abs(z))) * LOG2E for z in zs]
        if on_diagonal:
            masks = [(col0 + ks) < row for ks in starts]
            sps = [jnp.where(m, sp, 0.0) for m, sp in zip(masks, sps)]
        tails = [_dot(sp.astype(BF16), sel) for sp in sps]
        bases = [z - tail for z, tail in zip(zs, tails)]
        late_total = [jnp.broadcast_to(tail[:, half:half + 1], (tq, half)) for tail in tails]
        early_total = [jnp.broadcast_to(tail[:, 0:1], (tq, half)) for tail in tails]
        acc, run = carry
        for n, ks in enumerate(starts):
            run_mid = run + late_total[n]
            a = jnp.exp2(jnp.concatenate([bases[n][:, :half] - run_mid, bases[n][:, half:] - run], axis=1))
            if on_diagonal:
                a = jnp.where(masks[n], a, 0.0)
            acc = acc + _dot(a.astype(BF16), v_ref[pl.ds(ks, tk), :])
            run = run_mid + early_total[n]
        return acc, run

    carry = blocks([pl.multiple_of(i * tq, tq)],
                   (jnp.zeros((tq, HEAD_DIM), F32), jnp.zeros((tq, half), F32)), True)

    def run_groups(size, top, trips, carry):
        def group(step, carry):
            starts = [pl.multiple_of((top - 1 - size * step - s) * tk, tk) for s in range(size)]
            return blocks(starts, carry, False)
        return lax.fori_loop(0, trips, group, carry)

    carry = run_groups(per_trip, i, i // per_trip, carry)
    left = i % per_trip
    size = per_trip // 2
    while size >= 1:
        carry = run_groups(size, left, left // size, carry)
        left = left % size
        size //= 2
    acc, _ = carry
    o_ref[...] = acc.astype(o_ref.dtype)


def _sb_attention(zm, tq, tk, per_trip):
    t = zm.shape[0]
    assert tq == tk and tk % 256 == 0
    cq, ck, cv = A_Q // HEAD_DIM, A_K // HEAD_DIM, A_V // HEAD_DIM
    return pl.pallas_call(
        functools.partial(_sb_kernel, tq=tq, tk=tk, per_trip=per_trip),
        out_shape=jax.ShapeDtypeStruct((t, BRANCH_WIDTH), BF16),
        grid=(HEADS, t // tq),
        in_specs=[pl.BlockSpec((tq, HEAD_DIM), lambda h, i: (i, cq + h)),
                  pl.BlockSpec((t, HEAD_DIM), lambda h, i: (0, ck + h)),
                  pl.BlockSpec((t, HEAD_DIM), lambda h, i: (0, cv + h))],
        out_specs=pl.BlockSpec((tq, HEAD_DIM), lambda h, i: (i, h)),
        compiler_params=_cparams(2),
        name="sb_attention",
    )(zm, zm, zm)


def _gm_kernel(u_ref, v_ref, ng_ref, nb_ref, ws_ref, bs_ref, o_ref, *, n_chunk):
    u = jax.nn.gelu(u_ref[...].astype(F32))
    v = jax.nn.gelu(v_ref[...].astype(F32))
    v = _layer_norm(v, ng_ref[...], nb_ref[...])
    ii = lax.broadcasted_iota(jnp.int32, (GM_CHUNK, GM_CHUNK), 0)
    jj = lax.broadcasted_iota(jnp.int32, (GM_CHUNK, GM_CHUNK), 1)
    for g in range(HEADS):
        w = jnp.where(ii >= jj, ws_ref[g], 0.0).astype(BF16)
        bias = bs_ref[g]
        for c in range(n_chunk):
            rows = slice(c * GM_CHUNK, (c + 1) * GM_CHUNK)
            cols = slice(g * HEAD_DIM, (g + 1) * HEAD_DIM)
            mixed = _dot(w, v[rows, cols].astype(BF16)) + bias
            o_ref[rows, cols] = (u[rows, cols] * mixed).astype(o_ref.dtype)


def _gmlp(zm, norm_g, norm_b, w_s, b_s, tb):
    t = zm.shape[0]
    bias = jnp.broadcast_to(b_s[:, :, None], (HEADS, GM_CHUNK, HEAD_DIM))
    cu, cv = B_U // BRANCH_WIDTH, B_V // BRANCH_WIDTH
    return pl.pallas_call(
        functools.partial(_gm_kernel, n_chunk=tb // GM_CHUNK),
        out_shape=jax.ShapeDtypeStruct((t, BRANCH_WIDTH), BF16),
        grid=(t // tb,),
        in_specs=[pl.BlockSpec((tb, BRANCH_WIDTH), lambda i: (i, cu)),
                  pl.BlockSpec((tb, BRANCH_WIDTH), lambda i: (i, cv)),
                  pl.BlockSpec((1, BRANCH_WIDTH), lambda i: (0, 0)),
                  pl.BlockSpec((1, BRANCH_WIDTH), lambda i: (0, 0)),
                  pl.BlockSpec((HEADS, GM_CHUNK, GM_CHUNK), lambda i: (0, 0, 0)),
                  pl.BlockSpec((HEADS, GM_CHUNK, HEAD_DIM), lambda i: (0, 0, 0))],
        out_specs=pl.BlockSpec((tb, BRANCH_WIDTH), lambda i: (i, 0)),
        compiler_params=_cparams(1),
        name="gmlp",
    )(zm, zm, norm_g.reshape(1, -1), norm_b.reshape(1, -1), w_s, bias)


def _gdn_prep_kernel(q_ref, k_ref, v_ref, zs_ref, cw_ref, par_ref,
                     u_ref, w_ref, qe_ref, kd_ref, attn_ref, eg_ref, xs_ref, *, n_ch):
    c = GDN_CHUNK
    w3 = 3 * BRANCH_WIDTH
    rows_all = n_ch * c

    @pl.when(pl.program_id(0) == 0)
    def _():
        xs_ref[0:8, :] = jnp.zeros((8, w3), F32)

    xs_ref[8:8 + rows_all, 0:BRANCH_WIDTH] = q_ref[...].astype(F32)
    xs_ref[8:8 + rows_all, BRANCH_WIDTH:2 * BRANCH_WIDTH] = k_ref[...].astype(F32)
    xs_ref[8:8 + rows_all, 2 * BRANCH_WIDTH:w3] = v_ref[...].astype(F32)
    cw = cw_ref[...]
    y_all = xs_ref[5:5 + rows_all, :] * cw[0:1, :]
    for tap in range(1, GDN_CONV):
        y_all = y_all + xs_ref[5 + tap:5 + tap + rows_all, :] * cw[tap:tap + 1, :]
    xs_ref[0:8, :] = xs_ref[rows_all:rows_all + 8, :]
    y_all = _silu(y_all)

    zs_all = zs_ref[...]
    beta_full = _sigmoid(zs_all)
    g_full = -jnp.exp(par_ref[0:1, :]) * _softplus(zs_all + par_ref[1:2, :])

    ii = lax.broadcasted_iota(jnp.int32, (c, c), 0)
    jj = lax.broadcasted_iota(jnp.int32, (c, c), 1)
    tri_incl = jnp.where(ii >= jj, 1.0, 0.0).astype(BF16)
    ones_cc = jnp.ones((c, c), BF16)

    chains = []
    for ch, h in [(ch, h) for ch in range(n_ch) for h in range(HEADS)]:
        rows = slice(ch * c, (ch + 1) * c)
        y = y_all[rows]
        beta_all = beta_full[rows]
        g_all = g_full[rows]
        lanes = slice(h * HEAD_DIM, (h + 1) * HEAD_DIM)
        qh = y[:, h * HEAD_DIM:(h + 1) * HEAD_DIM]
        kh = y[:, BRANCH_WIDTH + h * HEAD_DIM:BRANCH_WIDTH + (h + 1) * HEAD_DIM]
        vh = y[:, 2 * BRANCH_WIDTH + h * HEAD_DIM:2 * BRANCH_WIDTH + (h + 1) * HEAD_DIM]
        qh = qh * lax.rsqrt(jnp.sum(qh * qh, axis=-1, keepdims=True) + 1e-6) * HEAD_DIM ** -0.5
        kh = kh * lax.rsqrt(jnp.sum(kh * kh, axis=-1, keepdims=True) + 1e-6)
        beta = beta_all[:, S_BETA + h:S_BETA + h + 1]
        g_b = jnp.broadcast_to(g_all[:, S_A + h:S_A + h + 1], (c, HEAD_DIM))
        chains.append(dict(ch=ch, h=h, rows=rows, lanes=lanes, qh=qh, kh=kh, vh=vh, beta=beta, g_b=g_b))

    for s in chains:
        s["gcol"] = _dot_sel_lhs(tri_incl, s["g_b"])
        s["grow"] = _dot_sel_lhs(ones_cc, jnp.where(ii <= jj, s["g_b"][:, :c], 0.0))
        s["kh16"] = s["kh"].astype(BF16)
        s["kb"] = s["kh"] * s["beta"]
        s["kk"] = _dot_nt(s["kb"].astype(BF16), s["kh16"])
        s["qk"] = _dot_nt(s["qh"].astype(BF16), s["kh16"])
    for s in chains:
        decay = jnp.where(ii >= jj, jnp.exp(s["gcol"][:, :c] - s["grow"]), 0.0)
        expg = jnp.exp(s["gcol"])
        g_last = s["gcol"][c - 1:c, :]
        rows, lanes = s["rows"], s["lanes"]
        qe_ref[rows, lanes] = (s["qh"] * expg).astype(BF16)
        kd_ref[rows, lanes] = (s["kh"] * jnp.exp(g_last - s["gcol"])).astype(BF16)
        attn_ref[s["h"], rows, :] = (s["qk"] * decay).astype(BF16)
        eg_ref[s["ch"], :, lanes] = jnp.broadcast_to(jnp.exp(g_last), (8, HEAD_DIM))
        s["p"] = -jnp.where(ii > jj, s["kk"] * decay, 0.0)
        s["x"] = jnp.concatenate([s["vh"] * s["beta"], s["kb"] * expg], axis=1)
    for r in range(6):
        for s in chains:
            s["x"] = s["x"] + _dot_f32(s["p"], s["x"])
            if r < 5:
                s["p"] = _dot_f32(s["p"], s["p"])
    for s in chains:
        u_ref[s["rows"], s["lanes"]] = s["x"][:, :HEAD_DIM]
        w_ref[s["rows"], s["lanes"]] = s["x"][:, HEAD_DIM:].astype(BF16)


def _gdn_scan_kernel(u_ref, w_ref, qe_ref, kd_ref, attn_ref, eg_ref, gate_ref, ng_ref, o_ref,
                     s_ref, *, n_ch):
    c = GDN_CHUNK

    @pl.when(pl.program_id(0) == 0)
    def _():
        s_ref[...] = jnp.zeros(s_ref.shape, F32)

    heads = range(HEADS)
    lanes = [slice(h * HEAD_DIM, (h + 1) * HEAD_DIM) for h in heads]
    state = [s_ref[h] for h in heads]
    for ch in range(n_ch):
        rows = slice(ch * c, (ch + 1) * c)
        s16 = [state[h].astype(BF16) for h in heads]
        v_new = [(u_ref[rows, lanes[h]] - _dot(w_ref[rows, lanes[h]], s16[h])).astype(BF16) for h in heads]
        state = [state[h] * eg_ref[ch, 0:1, lanes[h]] + _dot_tn(kd_ref[rows, lanes[h]], v_new[h])
                 for h in heads]
        out = [_dot(qe_ref[rows, lanes[h]], s16[h]) + _dot(attn_ref[h, rows, :], v_new[h]) for h in heads]
        for h in heads:
            o = out[h]
            o = o * lax.rsqrt(jnp.mean(o * o, axis=-1, keepdims=True) + NORM_EPS) * ng_ref[...]
            o_ref[rows, lanes[h]] = (o * _silu(gate_ref[rows, lanes[h]].astype(F32))).astype(o_ref.dtype)
    for h in heads:
        s_ref[h] = state[h]


def _gdn(zm, zs, conv_w, a_log, dt_bias, norm_g, prep_chunks, scan_chunks):
    t = zm.shape[0]
    c = GDN_CHUNK
    par = jnp.zeros((8, SMALL_COLS), F32)
    par = par.at[0, S_A:S_A + HEADS].set(a_log).at[1, S_A:S_A + HEADS].set(dt_bias)

    rp = prep_chunks * c
    blk = lambda col: pl.BlockSpec((rp, BRANCH_WIDTH), lambda n: (n, col // BRANCH_WIDTH))
    wide = pl.BlockSpec((rp, BRANCH_WIDTH), lambda n: (n, 0))
    u, w, qe, kd, attn, eg = pl.pallas_call(
        functools.partial(_gdn_prep_kernel, n_ch=prep_chunks),
        out_shape=(jax.ShapeDtypeStruct((t, BRANCH_WIDTH), F32),
                   jax.ShapeDtypeStruct((t, BRANCH_WIDTH), BF16),
                   jax.ShapeDtypeStruct((t, BRANCH_WIDTH), BF16),
                   jax.ShapeDtypeStruct((t, BRANCH_WIDTH), BF16),
                   jax.ShapeDtypeStruct((HEADS, t, c), BF16),
                   jax.ShapeDtypeStruct((t // c, 8, BRANCH_WIDTH), F32)),
        grid=(t // rp,),
        in_specs=[blk(C_Q), blk(C_K), blk(C_V),
                  pl.BlockSpec((rp, SMALL_COLS), lambda n: (n, 0)),
                  pl.BlockSpec((GDN_CONV, 3 * BRANCH_WIDTH), lambda n: (0, 0)),
                  pl.BlockSpec((8, SMALL_COLS), lambda n: (0, 0))],
        out_specs=(wide, wide, wide, wide,
                   pl.BlockSpec((HEADS, rp, c), lambda n: (0, n, 0)),
                   pl.BlockSpec((prep_chunks, 8, BRANCH_WIDTH), lambda n: (n, 0, 0))),
        scratch_shapes=[pltpu.VMEM((rp + 8, 3 * BRANCH_WIDTH), F32)],
        compiler_params=_cparams(1),
        name="gdn_chunk_prep",
    )(zm, zm, zm, zs, conv_w, par)

    rs = scan_chunks * c
    wide = pl.BlockSpec((rs, BRANCH_WIDTH), lambda n: (n, 0))
    return pl.pallas_call(
        functools.partial(_gdn_scan_kernel, n_ch=scan_chunks),
        out_shape=jax.ShapeDtypeStruct((t, BRANCH_WIDTH), BF16),
        grid=(t // rs,),
        in_specs=[wide, wide, wide, wide,
                  pl.BlockSpec((HEADS, rs, c), lambda n: (0, n, 0)),
                  pl.BlockSpec((scan_chunks, 8, BRANCH_WIDTH), lambda n: (n, 0, 0)),
                  pl.BlockSpec((rs, BRANCH_WIDTH), lambda n: (n, C_G // BRANCH_WIDTH)),
                  pl.BlockSpec((1, HEAD_DIM), lambda n: (0, 0))],
        out_specs=wide,
        scratch_shapes=[pltpu.VMEM((HEADS, HEAD_DIM, HEAD_DIM), F32)],
        compiler_params=_cparams(1),
        name="gdn_state_scan",
    )(u, w, qe, kd, attn, eg, zm, norm_g.reshape(1, -1))


def _ml_kernel(q_ref, k_ref, v_ref, og_ref, zs_ref, par_ref, ng_ref, o_ref,
               c_ref, n_ref, m_ref, *, n_ch):
    c = ML_CHUNK

    @pl.when(pl.program_id(0) == 0)
    def _():
        c_ref[...] = jnp.zeros(c_ref.shape, F32)
        n_ref[...] = jnp.zeros(n_ref.shape, F32)
        m_ref[...] = jnp.zeros(m_ref.shape, F32)

    zz_all = zs_ref[...] + par_ref[2:3, :]
    logf_all = -_softplus(-zz_all)

    ii = lax.broadcasted_iota(jnp.int32, (c, c), 0)
    jj = lax.broadcasted_iota(jnp.int32, (c, c), 1)
    tri_incl = jnp.where(ii >= jj, 1.0, 0.0).astype(BF16)
    ones_cc = jnp.ones((c, c), BF16)
    ones_cl = jnp.ones((c, HEAD_DIM), BF16)

    pairs = []
    for ch in range(n_ch):
        rows = slice(ch * c, (ch + 1) * c)
        for h in range(HEADS):
            lanes = slice(h * HEAD_DIM, (h + 1) * HEAD_DIM)
            qk_lanes = slice(h * ML_QK, (h + 1) * ML_QK)
            pairs.append(dict(
                ch=ch, h=h, rows=rows, lanes=lanes,
                qs=(q_ref[rows, qk_lanes].astype(F32) * ML_QK ** -0.5).astype(BF16),
                kh=k_ref[rows, qk_lanes], vh=v_ref[rows, lanes],
                i_b=jnp.broadcast_to(zz_all[rows, S_I + h:S_I + h + 1], (c, HEAD_DIM)),
                lf_b=jnp.broadcast_to(logf_all[rows, S_F + h:S_F + h + 1], (c, HEAD_DIM))))
    for s in pairs:
        s["bcol"] = _dot_sel_lhs(tri_incl, s["lf_b"])
        s["qk"] = _dot_nt(s["qs"], s["kh"])
    for s in pairs:
        rv = s["i_b"] - s["bcol"]
        s["rrow"] = _dot_sel_lhs(ones_cc, jnp.where(ii == jj, rv[:, :c], 0.0))
        s["b_last"] = s["bcol"][c - 1:c, :]
        s["log_kv"] = s["b_last"] - s["bcol"] + s["i_b"]
        s["kv_max"] = jnp.max(s["log_kv"], axis=0, keepdims=True)
    for s in pairs:
        s["log_intra"] = jnp.where(ii >= jj, s["bcol"][:, :c] + s["rrow"], -jnp.inf)
        s["intra_max"] = jnp.max(s["log_intra"], axis=1, keepdims=True)

    heads = range(HEADS)
    c_mat = [c_ref[h] for h in heads]
    n_b = [n_ref[h] for h in heads]
    m_prev = [m_ref[h][0:1, :] for h in heads]
    for ch in range(n_ch):
        cur = pairs[ch * HEADS:(ch + 1) * HEADS]
        m_out = [jnp.maximum(s["bcol"] + m_prev[s["h"]], s["intra_max"]) for s in cur]
        w_inter = [jnp.exp(s["bcol"] + m_prev[s["h"]] - m_out[s["h"]]) for s in cur]
        sc = [jnp.exp(s["log_intra"] - m_out[s["h"]][:, :c]) * s["qk"] for s in cur]
        num = [w_inter[h] * _dot(cur[h]["qs"], c_mat[h].astype(BF16)) + _dot(sc[h].astype(BF16), cur[h]["vh"])
               for h in heads]
        den = [w_inter[h] * _dot(cur[h]["qs"], n_b[h].astype(BF16)) + jnp.sum(sc[h], axis=1, keepdims=True)
               for h in heads]
        m_new = [jnp.maximum(s["b_last"] + m_prev[s["h"]], s["kv_max"]) for s in cur]
        kw = [(s["kh"].astype(F32) * jnp.exp(s["log_kv"] - m_new[s["h"]])[:, :ML_QK]).astype(BF16) for s in cur]
        decay = [jnp.exp(s["b_last"] + m_prev[s["h"]] - m_new[s["h"]]) for s in cur]
        c_mat = [decay[h] * c_mat[h] + _dot_tn(kw[h], cur[h]["vh"]) for h in heads]
        n_b = [decay[h] * n_b[h] + _dot_tn(kw[h], ones_cl) for h in heads]
        m_prev = m_new
        for h in heads:
            s = cur[h]
            hid = num[h] / jnp.maximum(jnp.abs(den[h]), jnp.exp(-m_out[h]))
            hid = hid * lax.rsqrt(jnp.mean(hid * hid, axis=-1, keepdims=True) + NORM_EPS)
            hid = hid * ng_ref[:, s["lanes"]]
            gate = _sigmoid(og_ref[s["rows"], s["lanes"]].astype(F32))
            o_ref[s["rows"], s["lanes"]] = (hid * gate).astype(o_ref.dtype)
    for h in heads:
        c_ref[h] = c_mat[h]
        n_ref[h] = n_b[h]
        m_ref[h] = jnp.broadcast_to(m_prev[h], (8, HEAD_DIM))


def _mlstm(zm, zs, i_bias, f_bias, norm_g, n_ch):
    t = zm.shape[0]
    c = ML_CHUNK * n_ch
    qk_w = HEADS * ML_QK
    par = jnp.zeros((8, SMALL_COLS), F32)
    par = par.at[2, S_I:S_I + HEADS].set(i_bias).at[2, S_F:S_F + HEADS].set(f_bias)
    return pl.pallas_call(
        functools.partial(_ml_kernel, n_ch=n_ch),
        out_shape=jax.ShapeDtypeStruct((t, BRANCH_WIDTH), BF16),
        grid=(t // c,),
        in_specs=[pl.BlockSpec((c, qk_w), lambda n: (n, D_Q // qk_w)),
                  pl.BlockSpec((c, qk_w), lambda n: (n, D_K // qk_w)),
                  pl.BlockSpec((c, BRANCH_WIDTH), lambda n: (n, D_V // BRANCH_WIDTH)),
                  pl.BlockSpec((c, BRANCH_WIDTH), lambda n: (n, D_O // BRANCH_WIDTH)),
                  pl.BlockSpec((c, SMALL_COLS), lambda n: (n, 0)),
                  pl.BlockSpec((8, SMALL_COLS), lambda n: (0, 0)),
                  pl.BlockSpec((1, BRANCH_WIDTH), lambda n: (0, 0))],
        out_specs=pl.BlockSpec((c, BRANCH_WIDTH), lambda n: (n, 0)),
        scratch_shapes=[pltpu.VMEM((HEADS, ML_QK, HEAD_DIM), F32),
                        pltpu.VMEM((HEADS, ML_QK, HEAD_DIM), F32),
                        pltpu.VMEM((HEADS, 8, HEAD_DIM), F32)],
        compiler_params=_cparams(1),
        name="mlstm",
    )(zm, zm, zm, zm, zs, par, norm_g.reshape(1, -1))


def _merge_kernel(a_ref, b_ref, c_ref, d_ref, g0_ref, g1_ref, g2_ref, g3_ref, wb_ref, o_ref):
    acc = None
    for g, (br, gr) in enumerate(((a_ref, g0_ref), (b_ref, g1_ref), (c_ref, g2_ref), (d_ref, g3_ref))):
        term = _sigmoid(gr[...].astype(F32)) * _dot(br[...], wb_ref[g])
        acc = term if acc is None else acc + term
    o_ref[...] = acc.astype(o_ref.dtype)


def _merge(outs, zm, wb16, tm):
    t = zm.shape[0]
    branch = pl.BlockSpec((tm, BRANCH_WIDTH), lambda i: (i, 0))
    gate = lambda g: pl.BlockSpec((tm, D_MODEL), lambda i: (i, g))
    return pl.pallas_call(
        _merge_kernel,
        out_shape=jax.ShapeDtypeStruct((t, D_MODEL), BF16),
        grid=(t // tm,),
        in_specs=[branch] * 4 + [gate(0), gate(1), gate(2), gate(3),
                                 pl.BlockSpec((N_BRANCHES, BRANCH_WIDTH, D_MODEL), lambda i: (0, 0, 0))],
        out_specs=pl.BlockSpec((tm, D_MODEL), lambda i: (i, 0)),
        compiler_params=_cparams(1),
        name="merge_branches",
    )(*outs, zm, zm, zm, zm, wb16)


def _proj_ln_kernel(a_ref, w_ref, x_ref, g_ref, b_ref, o_ref):
    y = ALPHA * x_ref[...] + _dot(a_ref[...], w_ref[...])
    o_ref[...] = _layer_norm(y, g_ref[...], b_ref[...])


def _proj_ln(a, w16, x, g, b, tm):
    t, k = a.shape
    return pl.pallas_call(
        _proj_ln_kernel,
        out_shape=jax.ShapeDtypeStruct((t, D_MODEL), F32),
        grid=(t // tm,),
        in_specs=[pl.BlockSpec((tm, k), lambda i: (i, 0)),
                  pl.BlockSpec((k, D_MODEL), lambda i: (0, 0)),
                  pl.BlockSpec((tm, D_MODEL), lambda i: (i, 0)),
                  pl.BlockSpec((1, D_MODEL), lambda i: (0, 0)),
                  pl.BlockSpec((1, D_MODEL), lambda i: (0, 0))],
        out_specs=pl.BlockSpec((tm, D_MODEL), lambda i: (i, 0)),
        compiler_params=_cparams(1),
        name="out_proj_layernorm",
    )(a, w16, x, g.reshape(1, -1), b.reshape(1, -1))


def _router_kernel(x_ref, wr_ref, bias_ref, idx_ref, w_ref, rank_ref, cnt_ref, *, tm):
    @pl.when(pl.program_id(0) == 0)
    def _():
        cnt_ref[...] = jnp.zeros(cnt_ref.shape, F32)

    logits = _dot_f32_nt(wr_ref[...], x_ref[...])
    scores = _sigmoid(logits)
    biased = scores + bias_ref[...]
    neg = -jnp.inf

    b3 = biased.reshape(N_GROUPS, GROUP_SIZE, tm)
    pos = lax.broadcasted_iota(jnp.int32, (N_GROUPS, GROUP_SIZE, tm), 1)
    m1 = jnp.max(b3, axis=1, keepdims=True)
    first = jnp.min(jnp.where(b3 == m1, pos, GROUP_SIZE), axis=1, keepdims=True)
    m2 = jnp.max(jnp.where(pos == first, neg, b3), axis=1, keepdims=True)
    gscore = m1 + m2

    gidx = lax.broadcasted_iota(jnp.int32, (N_GROUPS, 1, tm), 0)
    gsel = jnp.zeros((N_GROUPS, 1, tm), F32)
    for _ in range(TOPK_GROUPS):
        gm = jnp.max(gscore, axis=0, keepdims=True)
        gfirst = jnp.min(jnp.where(gscore == gm, gidx, N_GROUPS), axis=0, keepdims=True)
        hit = gidx == gfirst
        gsel = jnp.where(hit, 1.0, gsel)
        gscore = jnp.where(hit, neg, gscore)

    allowed = jnp.broadcast_to(gsel, (N_GROUPS, GROUP_SIZE, tm)) > 0.0
    masked = jnp.where(allowed, b3, neg).reshape(N_EXPERTS, tm)
    eidx = lax.broadcasted_iota(jnp.int32, (N_EXPERTS, tm), 0)
    idx_rows, sel_rows, hits = [], [], []
    for _ in range(TOP_K):
        em = jnp.max(masked, axis=0, keepdims=True)
        efirst = jnp.min(jnp.where(masked == em, eidx, N_EXPERTS), axis=0, keepdims=True)
        hit = eidx == efirst
        hits.append(hit)
        idx_rows.append(efirst)
        sel_rows.append(jnp.sum(jnp.where(hit, scores, 0.0), axis=0, keepdims=True))
        masked = jnp.where(hit, neg, masked)
    total = sel_rows[0]
    for r in sel_rows[1:]:
        total = total + r
    zero_i = jnp.zeros((1, tm), jnp.int32)
    zero_f = jnp.zeros((1, tm), F32)
    idx_ref[...] = jnp.concatenate(idx_rows + [zero_i, zero_i], axis=0)
    w_ref[...] = jnp.concatenate([r / total * ROUTED_SCALE for r in sel_rows] + [zero_f, zero_f], axis=0)

    chosen = jnp.zeros((N_EXPERTS, tm), F32)
    for hit in hits:
        chosen = jnp.where(hit, 1.0, chosen)
    chosen16 = chosen.astype(BF16)
    earlier = (lax.broadcasted_iota(jnp.int32, (tm, tm), 0)
               < lax.broadcasted_iota(jnp.int32, (tm, tm), 1))
    cnt = cnt_ref[...]
    before = _dot(chosen16, jnp.where(earlier, 1.0, 0.0).astype(BF16)) + jnp.tile(cnt, (1, tm // HEAD_DIM))
    ranks = [jnp.sum(jnp.where(hit, before, 0.0), axis=0, keepdims=True).astype(jnp.int32) for hit in hits]
    rank_ref[...] = jnp.concatenate(ranks + [zero_i, zero_i], axis=0)
    cnt_ref[...] = cnt + _dot(chosen16, jnp.ones((tm, HEAD_DIM), BF16))


def _dot_f32_nt(a, b):
    ah, al = _split2(a)
    bh, bl = _split2(b)
    return _dot_nt(ah, bh) + _dot_nt(ah, bl) + _dot_nt(al, bh)


def _router(x, w_router, router_bias, tm):
    t = x.shape[0]
    return pl.pallas_call(
        functools.partial(_router_kernel, tm=tm),
        out_shape=(jax.ShapeDtypeStruct((8, t), jnp.int32), jax.ShapeDtypeStruct((8, t), F32),
                   jax.ShapeDtypeStruct((8, t), jnp.int32),
                   jax.ShapeDtypeStruct((N_EXPERTS, HEAD_DIM), F32)),
        grid=(t // tm,),
        in_specs=[pl.BlockSpec((tm, D_MODEL), lambda i: (i, 0)),
                  pl.BlockSpec((N_EXPERTS, D_MODEL), lambda i: (0, 0)),
                  pl.BlockSpec((N_EXPERTS, 1), lambda i: (0, 0))],
        out_specs=(pl.BlockSpec((8, tm), lambda i: (0, i)), pl.BlockSpec((8, tm), lambda i: (0, i)),
                   pl.BlockSpec((8, tm), lambda i: (0, i)),
                   pl.BlockSpec((N_EXPERTS, HEAD_DIM), lambda i: (0, 0))),
        compiler_params=_cparams(1),
        name="router_topk",
    )(x, w_router.T, router_bias.reshape(N_EXPERTS, 1))


def _dispatch_kernel(pos_ref, x_ref, xs_hbm, sem, *, td):
    for r in range(td):
        for k in range(TOP_K):
            pltpu.make_async_copy(x_ref.at[pl.ds(r, 1), :],
                                  xs_hbm.at[pl.ds(pos_ref[0, r * TOP_K + k], 1), :],
                                  sem).start(priority=k % 2)
    for _ in range(TOP_K):
        pltpu.make_async_copy(x_ref, xs_hbm.at[pl.ds(0, td), :], sem).wait()


def _dispatch(x, pos, td):
    t = x.shape[0]
    return pl.pallas_call(
        functools.partial(_dispatch_kernel, td=td),
        out_shape=jax.ShapeDtypeStruct((t * TOP_K, D_MODEL), F32),
        grid=(t // td,),
        in_specs=[pl.BlockSpec((None, 1, td * TOP_K), lambda i: (i, 0, 0), memory_space=pltpu.SMEM),
                  pl.BlockSpec((td, D_MODEL), lambda i: (i, 0))],
        out_specs=pl.BlockSpec(memory_space=pl.ANY),
        scratch_shapes=[pltpu.SemaphoreType.DMA],
        compiler_params=_cparams(1),
        name="moe_dispatch",
    )(pos.reshape(t // td, 1, td * TOP_K), x)


def _expert_kernel(n_ref, tile_ref, exp_ref, nxt_ref, par_ref, lo_ref, hi_ref,
                   x_ref, wg_hbm, wu_hbm, wd_hbm, y_ref,
                   wgbuf, wubuf, wdbuf, wg16, wu16, wd16, wsem, *, tm, layer):
    j = pl.program_id(0)
    prev = jnp.maximum(j - 1, 0)
    first = (j == 0) | (tile_ref[prev] != tile_ref[j])
    expert = exp_ref[j]
    slot = par_ref[j]

    def copies(ex, s):
        return [pltpu.make_async_copy(hbm.at[layer, ex], buf.at[s], wsem.at[s])
                for hbm, buf in ((wg_hbm, wgbuf), (wu_hbm, wubuf), (wd_hbm, wdbuf))]

    @pl.when(j < n_ref[0])
    def _():
        @pl.when((j == 0) | (exp_ref[prev] != expert))
        def _():
            @pl.when(j == 0)
            def _():
                for cp in copies(expert, slot):
                    cp.start(priority=1)

            @pl.when(nxt_ref[j] != expert)
            def _():
                for cp in copies(nxt_ref[j], 1 - slot):
                    cp.start(priority=1)

            for cp in copies(expert, slot):
                cp.wait()
            wg16[...] = wgbuf[slot].astype(BF16)
            wu16[...] = wubuf[slot].astype(BF16)
            wd16[...] = wdbuf[slot].astype(BF16)

        x = x_ref[...].astype(BF16)
        hidden = _silu(_dot(x, wg16[...])) * _dot(x, wu16[...])
        y = _dot(hidden.astype(BF16), wd16[...])

        @pl.when(first)
        def _():
            y_ref[...] = y

        @pl.when(jnp.logical_not(first))
        def _():
            row = lax.broadcasted_iota(jnp.int32, (tm, 1), 0)
            mine = (row >= lo_ref[j]) & (row < hi_ref[j])
            y_ref[...] = jnp.where(mine, y, y_ref[...])


def _experts(xs, sched, w_gate, w_up, w_down, layer, tm):
    n_max = sched[1].shape[0]
    tile_of = lambda j, n, tl, *rest: tl[j]
    hbm = pl.BlockSpec(memory_space=pl.ANY)
    up_shape, down_shape = (D_MODEL, EXPERT_DIM), (EXPERT_DIM, D_MODEL)
    return pl.pallas_call(
        functools.partial(_expert_kernel, tm=tm, layer=layer),
        out_shape=jax.ShapeDtypeStruct(xs.shape, F32),
        grid_spec=pltpu.PrefetchScalarGridSpec(
            num_scalar_prefetch=len(sched),
            grid=(n_max,),
            in_specs=[pl.BlockSpec((tm, D_MODEL), lambda *a: (tile_of(*a), 0)), hbm, hbm, hbm],
            out_specs=pl.BlockSpec((tm, D_MODEL), lambda *a: (tile_of(*a), 0)),
            scratch_shapes=[pltpu.VMEM((2,) + up_shape, F32), pltpu.VMEM((2,) + up_shape, F32),
                            pltpu.VMEM((2,) + down_shape, F32),
                            pltpu.VMEM(up_shape, BF16), pltpu.VMEM(up_shape, BF16),
                            pltpu.VMEM(down_shape, BF16),
                            pltpu.SemaphoreType.DMA((2,))]),
        compiler_params=_cparams(1),
        name="routed_experts",
    )(*sched, xs, w_gate, w_up, w_down)


def _expert_schedule(idx, rank, counts, t, tm):
    i32 = jnp.int32
    n_rows = t * TOP_K
    n_tiles = n_rows // tm
    n_max = n_tiles + N_EXPERTS - 1
    end = jnp.cumsum(counts)
    start = end - counts
    ids = jnp.arange(N_EXPERTS, dtype=i32)
    start_of = jnp.sum(jnp.where(idx[..., None] == ids, start, 0), axis=-1)
    pos = (start_of + rank).reshape(-1).astype(i32)
    first_tile = start // tm
    items_per_e = jnp.where(counts > 0, (end - 1) // tm - first_tile + 1, 0)
    item_end = jnp.cumsum(items_per_e)
    n_items = item_end[-1]
    j = jnp.arange(n_max, dtype=i32)
    e = jnp.minimum(jnp.sum((item_end[None, :] <= j[:, None]).astype(i32), axis=1), N_EXPERTS - 1)
    tile = first_tile[e] + j - (item_end[e] - items_per_e[e])
    lo = jnp.maximum(start[e], tile * tm) - tile * tm
    hi = jnp.minimum(end[e], (tile + 1) * tm) - tile * tm
    valid = j < n_items
    last = jnp.maximum(n_items - 1, 0)
    pick = lambda a: jnp.where(valid, a, a[last]).astype(i32)
    ids = jnp.arange(N_EXPERTS, dtype=i32)
    later = (ids[None, :] > ids[:, None]) & (counts[None, :] > 0)
    nxt = jnp.min(jnp.where(later, ids[None, :], N_EXPERTS), axis=1)
    nxt = jnp.where(nxt == N_EXPERTS, ids, nxt)
    parity = (jnp.cumsum((counts > 0).astype(i32)) - 1) % 2
    return pos, (n_items.reshape(1).astype(i32), pick(tile), pick(e), pick(nxt[e]), pick(parity[e]),
                 pick(lo), pick(hi))


def _combine_kernel(pos_ref, pos_next_ref, x_ref, w_ref, ys_hbm, sg_ref, su_ref, sd_ref, g_ref, b_ref,
                    o_ref, o16_ref, ybuf, sem, *, tt, n_steps):
    i = pl.program_id(0)
    slot = lax.rem(i, 2)

    def gather(idx_ref, s):
        for r in range(tt):
            for k in range(TOP_K):
                pltpu.make_async_copy(ys_hbm.at[pl.ds(idx_ref[0, r * TOP_K + k], 1), :],
                                      ybuf.at[s, k, pl.ds(r, 1), :], sem.at[s]).start()

    @pl.when(i == 0)
    def _():
        gather(pos_ref, 0)

    @pl.when(i + 1 < n_steps)
    def _():
        gather(pos_next_ref, 1 - slot)

    x = x_ref[...]
    x16 = x.astype(BF16)
    hidden = _silu(_dot(x16, sg_ref[...])) * _dot(x16, su_ref[...])
    acc = _dot(hidden.astype(BF16), sd_ref[...])
    for k in range(TOP_K):
        pltpu.make_async_copy(ys_hbm.at[pl.ds(0, tt), :], ybuf.at[slot, k], sem.at[slot]).wait()
    w = w_ref[...]
    for k in range(TOP_K):
        acc = acc + ybuf[slot, k] * w[:, k:k + 1]
    out = _layer_norm(ALPHA * x + acc, g_ref[...], b_ref[...])
    o_ref[...] = out
    o16_ref[...] = out.astype(BF16)


def _combine(x, w_tok, pos, ys, sg16, su16, sd16, g, b, tt):
    t = x.shape[0]
    nb = t // tt
    pos3 = pos.reshape(nb, 1, tt * TOP_K)
    tile = pl.BlockSpec((tt, D_MODEL), lambda i: (i, 0))
    vec = pl.BlockSpec((1, D_MODEL), lambda i: (0, 0))
    idx_spec = lambda f: pl.BlockSpec((None, 1, tt * TOP_K), f, memory_space=pltpu.SMEM)
    return pl.pallas_call(
        functools.partial(_combine_kernel, tt=tt, n_steps=nb),
        out_shape=(jax.ShapeDtypeStruct((t, D_MODEL), F32), jax.ShapeDtypeStruct((t, D_MODEL), BF16)),
        grid=(nb,),
        in_specs=[idx_spec(lambda i: (i, 0, 0)), idx_spec(lambda i: (jnp.minimum(i + 1, nb - 1), 0, 0)),
                  tile, pl.BlockSpec((tt, 8), lambda i: (i, 0)),
                  pl.BlockSpec(memory_space=pl.ANY),
                  pl.BlockSpec((D_MODEL, SHARED_DIM), lambda i: (0, 0)),
                  pl.BlockSpec((D_MODEL, SHARED_DIM), lambda i: (0, 0)),
                  pl.BlockSpec((SHARED_DIM, D_MODEL), lambda i: (0, 0)), vec, vec],
        out_specs=(tile, tile),
        scratch_shapes=[pltpu.VMEM((2, TOP_K, tt, D_MODEL), F32), pltpu.SemaphoreType.DMA((2,))],
        compiler_params=_cparams(1),
        name="moe_combine_layernorm",
    )(pos3, pos3, x, w_tok, ys, sg16, su16, sd16, g.reshape(1, -1), b.reshape(1, -1))


def _tile(t, pref):
    return min(t, pref)


def _in_proj(x16, w_in):
    t = x16.shape[0]
    tm = _tile(t, 1024)
    wide = lambda r, tn, name: _matmul_nt(x16, w_in[:, r[0]:r[1]].T.astype(BF16), BF16, tm, tn, name)
    z_abc = wide(ABC_COLS, 1536, "in_proj_abc")
    z_d = wide(D_COLS, 1536, "in_proj_d")
    z_gate = wide(GATE_COLS, 2048, "in_proj_gate")
    pad = jnp.zeros((D_MODEL, SMALL_COLS - sum(b - a for a, b in SMALL_SRC)), F32)
    w_small = jnp.concatenate([w_in[:, a:b] for a, b in SMALL_SRC] + [pad], axis=1).astype(BF16)
    zs = _matmul(x16, w_small, F32, tm, SMALL_COLS, "in_proj_small")
    return z_abc, z_d, z_gate, zs


def _mixer(x, x16, p):
    t = x.shape[0]
    zm, z_d, z_gate, zs = _in_proj(x16, p["w_in"])
    out_a = _sb_attention(zm, 256, 256, 8)
    out_b = _gmlp(zm, p["gm_norm_g"], p["gm_norm_b"], p["gm_w_s"], p["gm_b_s"], _tile(t, 512))
    out_c = _gdn(zm, zs, p["gdn_conv_w"], p["gdn_a_log"], p["gdn_dt_bias"], p["gdn_norm_g"], 4, 4)
    out_d = _mlstm(z_d, zs, p["ml_i_bias"], p["ml_f_bias"], p["ml_norm_g"], 4)
    merged = _merge((out_a, out_b, out_c, out_d), z_gate, p["w_branch"].astype(BF16), _tile(t, 512))
    return _proj_ln(merged, p["w_out"].astype(BF16), x, p["ln1_g"], p["ln1_b"], _tile(t, 512))


def _moe(x, p, tm):
    t = x.shape[0]
    idx_t, w_t, rank_t, cnt = _router(x, p["w_router"], p["router_bias"], _tile(t, 512))
    w_tok = w_t.T
    counts = cnt[:, 0].astype(jnp.int32)
    pos, sched = _expert_schedule(idx_t[:TOP_K].T, rank_t[:TOP_K].T, counts, t, tm)
    xs = _dispatch(x, pos, _tile(t, 256))
    ys = _experts(xs, sched, p["w_gate_all"], p["w_up_all"], p["w_down_all"], p["layer"], tm)
    return _combine(x, w_tok, pos, ys, p["ws_gate"].astype(BF16), p["ws_up"].astype(BF16),
                    p["ws_down"].astype(BF16), p["ln2_g"], p["ln2_b"], _tile(t, 128))


def _layer(x, x16, p, tm_expert):
    x1 = _mixer(x, x16, p)
    return _moe(x1, p, tm_expert)


_STACKED = ("w_gate", "w_up", "w_down")


def _layer_params(params, l):
    p = {k: v[l] for k, v in params.items() if k not in _STACKED}
    p.update({k + "_all": params[k] for k in _STACKED})
    p["layer"] = l
    return p


def kernel(x, w_in, gm_norm_g, gm_norm_b, gm_w_s, gm_b_s, gdn_conv_w, gdn_a_log, gdn_dt_bias, gdn_norm_g, ml_i_bias, ml_f_bias, ml_norm_g, w_branch, w_out, ln1_g, ln1_b, w_router, router_bias, w_gate, w_up, w_down, ws_gate, ws_up, ws_down, ln2_g, ln2_b):
    params = dict(w_in=w_in, gm_norm_g=gm_norm_g, gm_norm_b=gm_norm_b, gm_w_s=gm_w_s, gm_b_s=gm_b_s,
                  gdn_conv_w=gdn_conv_w, gdn_a_log=gdn_a_log, gdn_dt_bias=gdn_dt_bias,
                  gdn_norm_g=gdn_norm_g, ml_i_bias=ml_i_bias, ml_f_bias=ml_f_bias, ml_norm_g=ml_norm_g,
                  w_branch=w_branch, w_out=w_out, ln1_g=ln1_g, ln1_b=ln1_b, w_router=w_router,
                  router_bias=router_bias, w_gate=w_gate, w_up=w_up, w_down=w_down, ws_gate=ws_gate,
                  ws_up=ws_up, ws_down=ws_down, ln2_g=ln2_g, ln2_b=ln2_b)
    b, t, d = x.shape
    h = x.reshape(b * t, d)
    h16 = h.astype(BF16)
    for l in range(DEPTH):
        h, h16 = _layer(h, h16, _layer_params(params, l), 256)
    return h.reshape(b, t, d)
```

```python
import functools

import jax
import jax.numpy as jnp
from jax import lax
from jax.experimental import pallas as pl
from jax.experimental.pallas import tpu as pltpu

F32 = jnp.float32
BF16 = jnp.bfloat16

D_MODEL = 2048
N_BRANCHES = 4
BRANCH_WIDTH = 512
HEADS = 4
HEAD_DIM = 128
GM_CHUNK = 128
GDN_CHUNK = 64
GDN_CONV = 4
ML_CHUNK = 64
ML_QK = 64
N_EXPERTS = 64
N_GROUPS = 8
GROUP_SIZE = 8
TOPK_GROUPS = 4
TOP_K = 6
EXPERT_DIM = 512
SHARED_DIM = 512
ROUTED_SCALE = 2.5
DEPTH = 2
ALPHA = (2 * DEPTH) ** 0.25
NORM_EPS = 1e-5
LOG2E = 1.4426950408889634

ABC_COLS = (0, 4608)
A_Q, A_K, A_V = 0, 512, 1024
B_U, B_V = 1536, 2048
C_Q, C_K, C_V, C_G = 2560, 3072, 3584, 4096
D_COLS = (4616, 6152)
D_Q, D_K, D_V, D_O = 0, 256, 512, 1024
GATE_COLS = (6160, 6160 + N_BRANCHES * D_MODEL)
SMALL_SRC = ((4608, 4616), (6152, 6160))
S_BETA, S_A, S_I, S_F = 0, 4, 8, 12
SMALL_COLS = 128

VMEM_LIMIT = 56 * 1024 * 1024


def _cparams(n_axes):
    return pltpu.CompilerParams(
        dimension_semantics=("arbitrary",) * n_axes, vmem_limit_bytes=VMEM_LIMIT)


def _dot(a, b):
    return jnp.dot(a, b, preferred_element_type=F32)


def _dot_nt(a, b):
    return lax.dot_general(a, b, (((1,), (1,)), ((), ())), preferred_element_type=F32)


def _dot_tn(a, b):
    return lax.dot_general(a, b, (((0,), (0,)), ((), ())), preferred_element_type=F32)


def _split2(a):
    hi = a.astype(BF16)
    lo = (a - hi.astype(F32)).astype(BF16)
    return hi, lo


def _split3(a):
    hi = a.astype(BF16)
    r = a - hi.astype(F32)
    mid = r.astype(BF16)
    lo = (r - mid.astype(F32)).astype(BF16)
    return hi, mid, lo


def _dot_sel_lhs(sel, a):
    sel3 = jnp.concatenate([sel, sel, sel], axis=1)
    return _dot(sel3, jnp.concatenate(_split3(a), axis=0))


def _dot_sel_rhs(a, sel):
    hi, lo = _split2(a)
    return _dot(hi, sel) + _dot(lo, sel)


def _dot_f32(a, b):
    ah, al = _split2(a)
    bh, bl = _split2(b)
    return _dot(jnp.concatenate([ah, ah, al], axis=1), jnp.concatenate([bh, bl, bh], axis=0))


def _softplus(x):
    return jnp.maximum(x, 0.0) + jnp.log1p(jnp.exp(-jnp.abs(x)))


def _sigmoid(x):
    return 1.0 / (1.0 + jnp.exp(-x))


def _silu(x):
    return x * _sigmoid(x)


def _layer_norm(x, g, b):
    mu = jnp.mean(x, axis=-1, keepdims=True)
    xc = x - mu
    var = jnp.mean(xc * xc, axis=-1, keepdims=True)
    return xc * lax.rsqrt(var + NORM_EPS) * g + b


def _mm_kernel(x_ref, w_ref, o_ref):
    o_ref[...] = _dot(x_ref[...], w_ref[...]).astype(o_ref.dtype)


def _matmul(x, w, out_dtype, tm, tn, name):
    m, k = x.shape
    n = w.shape[1]
    return pl.pallas_call(
        _mm_kernel,
        out_shape=jax.ShapeDtypeStruct((m, n), out_dtype),
        grid=(n // tn, m // tm),
        in_specs=[pl.BlockSpec((tm, k), lambda j, i: (i, 0)),
                  pl.BlockSpec((k, tn), lambda j, i: (0, j))],
        out_specs=pl.BlockSpec((tm, tn), lambda j, i: (i, j)),
        compiler_params=_cparams(2),
        name=name,
    )(x, w)


def _mm_nt_kernel(x_ref, wt_ref, o_ref):
    o_ref[...] = _dot_nt(x_ref[...], wt_ref[...]).astype(o_ref.dtype)


def _matmul_nt(x, wt, out_dtype, tm, tn, name):
    m, k = x.shape
    n = wt.shape[0]
    return pl.pallas_call(
        _mm_nt_kernel,
        out_shape=jax.ShapeDtypeStruct((m, n), out_dtype),
        grid=(n // tn, m // tm),
        in_specs=[pl.BlockSpec((tm, k), lambda j, i: (i, 0)),
                  pl.BlockSpec((tn, k), lambda j, i: (j, 0))],
        out_specs=pl.BlockSpec((tm, tn), lambda j, i: (i, j)),
        compiler_params=_cparams(2),
        name=name,
    )(x, wt)


def _sb_kernel(q_ref, k_ref, v_ref, o_ref, *, tq, tk, per_trip):
    i = pl.program_id(1)
    q = (q_ref[...].astype(F32) * (HEAD_DIM ** -0.5 * LOG2E)).astype(BF16)
    half = tk // 2
    row = i * tq + lax.broadcasted_iota(jnp.int32, (tq, tk), 0)
    col0 = lax.broadcasted_iota(jnp.int32, (tq, tk), 1)
    jj = lax.broadcasted_iota(jnp.int32, (tk, tk), 0)
    ss = lax.broadcasted_iota(jnp.int32, (tk, tk), 1)
    sel = jnp.where((jj >= ss) & ((jj < half) == (ss < half)), 1.0, 0.0).astype(BF16)

    def blocks(starts, carry, on_diagonal):
        zs = [_dot_nt(q, k_ref[pl.ds(ks, tk), :]) for ks in starts]
        sps = [jnp.maximum(z, 0.0) + jnp.log(1.0 + jnp.exp2(-jnp.abs(z))) * LOG2E for z in zs]
        if on_diagonal:
            masks = [(col0 + ks) < row for ks in starts]
            sps = [jnp.where(m, sp, 0.0) for m, sp in zip(masks, sps)]
        tails = [_dot(sp.astype(BF16), sel) for sp in sps]
        bases = [z - tail for z, tail in zip(zs, tails)]
        late_total = [jnp.broadcast_to(tail[:, half:half + 1], (tq, half)) for tail in tails]
        early_total = [jnp.broadcast_to(tail[:, 0:1], (tq, half)) for tail in tails]
        acc, run = carry
        for n, ks in enumerate(starts):
            run_mid = run + late_total[n]
            a = jnp.exp2(jnp.concatenate([bases[n][:, :half] - run_mid, bases[n][:, half:] - run], axis=1))
            if on_diagonal:
                a = jnp.where(masks[n], a, 0.0)
            acc = acc + _dot(a.astype(BF16), v_ref[pl.ds(ks, tk), :])
            run = run_mid + early_total[n]
        return acc, run

    carry = blocks([pl.multiple_of(i * tq, tq)],
                   (jnp.zeros((tq, HEAD_DIM), F32), jnp.zeros((tq, half), F32)), True)

    def run_groups(size, top, trips, carry):
        def group(step, carry):
            starts = [pl.multiple_of((top - 1 - size * step - s) * tk, tk) for s in range(size)]
            return blocks(starts, carry, False)
        return lax.fori_loop(0, trips, group, carry)

    carry = run_groups(per_trip, i, i // per_trip, carry)
    left = i % per_trip
    size = per_trip // 2
    while size >= 1:
        carry = run_groups(size, left, left // size, carry)
        left = left % size
        size //= 2
    acc, _ = carry
    o_ref[...] = acc.astype(o_ref.dtype)


def _sb_attention(zm, tq, tk, per_trip):
    t = zm.shape[0]
    assert tq == tk and tk % 256 == 0
    cq, ck, cv = A_Q // HEAD_DIM, A_K // HEAD_DIM, A_V // HEAD_DIM
    return pl.pallas_call(
        functools.partial(_sb_kernel, tq=tq, tk=tk, per_trip=per_trip),
        out_shape=jax.ShapeDtypeStruct((t, BRANCH_WIDTH), BF16),
        grid=(HEADS, t // tq),
        in_specs=[pl.BlockSpec((tq, HEAD_DIM), lambda h, i: (i, cq + h)),
                  pl.BlockSpec((t, HEAD_DIM), lambda h, i: (0, ck + h)),
                  pl.BlockSpec((t, HEAD_DIM), lambda h, i: (0, cv + h))],
        out_specs=pl.BlockSpec((tq, HEAD_DIM), lambda h, i: (i, h)),
        compiler_params=_cparams(2),
        name="sb_attention",
    )(zm, zm, zm)


def _gm_kernel(u_ref, v_ref, ng_ref, nb_ref, ws_ref, bs_ref, o_ref, *, n_chunk):
    u = jax.nn.gelu(u_ref[...].astype(F32))
    v = jax.nn.gelu(v_ref[...].astype(F32))
    v = _layer_norm(v, ng_ref[...], nb_ref[...])
    ii = lax.broadcasted_iota(jnp.int32, (GM_CHUNK, GM_CHUNK), 0)
    jj = lax.broadcasted_iota(jnp.int32, (GM_CHUNK, GM_CHUNK), 1)
    for g in range(HEADS):
        w = jnp.where(ii >= jj, ws_ref[g], 0.0).astype(BF16)
        bias = bs_ref[g]
        for c in range(n_chunk):
            rows = slice(c * GM_CHUNK, (c + 1) * GM_CHUNK)
            cols = slice(g * HEAD_DIM, (g + 1) * HEAD_DIM)
            mixed = _dot(w, v[rows, cols].astype(BF16)) + bias
            o_ref[rows, cols] = (u[rows, cols] * mixed).astype(o_ref.dtype)


def _gmlp(zm, norm_g, norm_b, w_s, b_s, tb):
    t = zm.shape[0]
    bias = jnp.broadcast_to(b_s[:, :, None], (HEADS, GM_CHUNK, HEAD_DIM))
    cu, cv = B_U // BRANCH_WIDTH, B_V // BRANCH_WIDTH
    return pl.pallas_call(
        functools.partial(_gm_kernel, n_chunk=tb // GM_CHUNK),
        out_shape=jax.ShapeDtypeStruct((t, BRANCH_WIDTH), BF16),
        grid=(t // tb,),
        in_specs=[pl.BlockSpec((tb, BRANCH_WIDTH), lambda i: (i, cu)),
                  pl.BlockSpec((tb, BRANCH_WIDTH), lambda i: (i, cv)),
                  pl.BlockSpec((1, BRANCH_WIDTH), lambda i: (0, 0)),
                  pl.BlockSpec((1, BRANCH_WIDTH), lambda i: (0, 0)),
                  pl.BlockSpec((HEADS, GM_CHUNK, GM_CHUNK), lambda i: (0, 0, 0)),
                  pl.BlockSpec((HEADS, GM_CHUNK, HEAD_DIM), lambda i: (0, 0, 0))],
        out_specs=pl.BlockSpec((tb, BRANCH_WIDTH), lambda i: (i, 0)),
        compiler_params=_cparams(1),
        name="gmlp",
    )(zm, zm, norm_g.reshape(1, -1), norm_b.reshape(1, -1), w_s, bias)


def _gdn_prep_kernel(q_ref, k_ref, v_ref, zs_ref, cw_ref, par_ref,
                     u_ref, w_ref, qe_ref, kd_ref, attn_ref, eg_ref, xs_ref, *, n_ch):
    c = GDN_CHUNK
    w3 = 3 * BRANCH_WIDTH
    rows_all = n_ch * c

    @pl.when(pl.program_id(0) == 0)
    def _():
        xs_ref[0:8, :] = jnp.zeros((8, w3), F32)

    xs_ref[8:8 + rows_all, 0:BRANCH_WIDTH] = q_ref[...].astype(F32)
    xs_ref[8:8 + rows_all, BRANCH_WIDTH:2 * BRANCH_WIDTH] = k_ref[...].astype(F32)
    xs_ref[8:8 + rows_all, 2 * BRANCH_WIDTH:w3] = v_ref[...].astype(F32)
    cw = cw_ref[...]
    y_all = xs_ref[5:5 + rows_all, :] * cw[0:1, :]
    for tap in range(1, GDN_CONV):
        y_all = y_all + xs_ref[5 + tap:5 + tap + rows_all, :] * cw[tap:tap + 1, :]
    xs_ref[0:8, :] = xs_ref[rows_all:rows_all + 8, :]
    y_all = _silu(y_all)

    zs_all = zs_ref[...]
    beta_full = _sigmoid(zs_all)
    g_full = -jnp.exp(par_ref[0:1, :]) * _softplus(zs_all + par_ref[1:2, :])

    ii = lax.broadcasted_iota(jnp.int32, (c, c), 0)
    jj = lax.broadcasted_iota(jnp.int32, (c, c), 1)
    tri_incl = jnp.where(ii >= jj, 1.0, 0.0).astype(BF16)
    ones_cc = jnp.ones((c, c), BF16)

    chains = []
    for ch, h in [(ch, h) for ch in range(n_ch) for h in range(HEADS)]:
        rows = slice(ch * c, (ch + 1) * c)
        y = y_all[rows]
        beta_all = beta_full[rows]
        g_all = g_full[rows]
        lanes = slice(h * HEAD_DIM, (h + 1) * HEAD_DIM)
        qh = y[:, h * HEAD_DIM:(h + 1) * HEAD_DIM]
        kh = y[:, BRANCH_WIDTH + h * HEAD_DIM:BRANCH_WIDTH + (h + 1) * HEAD_DIM]
        vh = y[:, 2 * BRANCH_WIDTH + h * HEAD_DIM:2 * BRANCH_WIDTH + (h + 1) * HEAD_DIM]
        qh = qh * lax.rsqrt(jnp.sum(qh * qh, axis=-1, keepdims=True) + 1e-6) * HEAD_DIM ** -0.5
        kh = kh * lax.rsqrt(jnp.sum(kh * kh, axis=-1, keepdims=True) + 1e-6)
        beta = beta_all[:, S_BETA + h:S_BETA + h + 1]
        g_b = jnp.broadcast_to(g_all[:, S_A + h:S_A + h + 1], (c, HEAD_DIM))
        chains.append(dict(ch=ch, h=h, rows=rows, lanes=lanes, qh=qh, kh=kh, vh=vh, beta=beta, g_b=g_b))

    for s in chains:
        s["gcol"] = _dot_sel_lhs(tri_incl, s["g_b"])
        s["grow"] = _dot_sel_lhs(ones_cc, jnp.where(ii <= jj, s["g_b"][:, :c], 0.0))
        s["kh16"] = s["kh"].astype(BF16)
        s["kb"] = s["kh"] * s["beta"]
        s["kk"] = _dot_nt(s["kb"].astype(BF16), s["kh16"])
        s["qk"] = _dot_nt(s["qh"].astype(BF16), s["kh16"])
    for s in chains:
        decay = jnp.where(ii >= jj, jnp.exp(s["gcol"][:, :c] - s["grow"]), 0.0)
        expg = jnp.exp(s["gcol"])
        g_last = s["gcol"][c - 1:c, :]
        rows, lanes = s["rows"], s["lanes"]
        qe_ref[rows, lanes] = (s["qh"] * expg).astype(BF16)
        kd_ref[rows, lanes] = (s["kh"] * jnp.exp(g_last - s["gcol"])).astype(BF16)
        attn_ref[s["h"], rows, :] = (s["qk"] * decay).astype(BF16)
        eg_ref[s["ch"], :, lanes] = jnp.broadcast_to(jnp.exp(g_last), (8, HEAD_DIM))
        s["p"] = -jnp.where(ii > jj, s["kk"] * decay, 0.0)
        s["x"] = jnp.concatenate([s["vh"] * s["beta"], s["kb"] * expg], axis=1)
    for r in range(6):
        for s in chains:
            s["x"] = s["x"] + _dot_f32(s["p"], s["x"])
            if r < 5:
                s["p"] = _dot_f32(s["p"], s["p"])
    for s in chains:
        u_ref[s["rows"], s["lanes"]] = s["x"][:, :HEAD_DIM]
        w_ref[s["rows"], s["lanes"]] = s["x"][:, HEAD_DIM:].astype(BF16)


def _gdn_scan_kernel(u_ref, w_ref, qe_ref, kd_ref, attn_ref, eg_ref, gate_ref, ng_ref, o_ref,
                     s_ref, *, n_ch):
    c = GDN_CHUNK

    @pl.when(pl.program_id(0) == 0)
    def _():
        s_ref[...] = jnp.zeros(s_ref.shape, F32)

    heads = range(HEADS)
    lanes = [slice(h * HEAD_DIM, (h + 1) * HEAD_DIM) for h in heads]
    state = [s_ref[h] for h in heads]
    for ch in range(n_ch):
        rows = slice(ch * c, (ch + 1) * c)
        s16 = [state[h].astype(BF16) for h in heads]
        v_new = [(u_ref[rows, lanes[h]] - _dot(w_ref[rows, lanes[h]], s16[h])).astype(BF16) for h in heads]
        state = [state[h] * eg_ref[ch, 0:1, lanes[h]] + _dot_tn(kd_ref[rows, lanes[h]], v_new[h])
                 for h in heads]
        out = [_dot(qe_ref[rows, lanes[h]], s16[h]) + _dot(attn_ref[h, rows, :], v_new[h]) for h in heads]
        for h in heads:
            o = out[h]
            o = o * lax.rsqrt(jnp.mean(o * o, axis=-1, keepdims=True) + NORM_EPS) * ng_ref[...]
            o_ref[rows, lanes[h]] = (o * _silu(gate_ref[rows, lanes[h]].astype(F32))).astype(o_ref.dtype)
    for h in heads:
        s_ref[h] = state[h]


def _gdn(zm, zs, conv_w, a_log, dt_bias, norm_g, prep_chunks, scan_chunks):
    t = zm.shape[0]
    c = GDN_CHUNK
    par = jnp.zeros((8, SMALL_COLS), F32)
    par = par.at[0, S_A:S_A + HEADS].set(a_log).at[1, S_A:S_A + HEADS].set(dt_bias)

    rp = prep_chunks * c
    blk = lambda col: pl.BlockSpec((rp, BRANCH_WIDTH), lambda n: (n, col // BRANCH_WIDTH))
    wide = pl.BlockSpec((rp, BRANCH_WIDTH), lambda n: (n, 0))
    u, w, qe, kd, attn, eg = pl.pallas_call(
        functools.partial(_gdn_prep_kernel, n_ch=prep_chunks),
        out_shape=(jax.ShapeDtypeStruct((t, BRANCH_WIDTH), F32),
                   jax.ShapeDtypeStruct((t, BRANCH_WIDTH), BF16),
                   jax.ShapeDtypeStruct((t, BRANCH_WIDTH), BF16),
                   jax.ShapeDtypeStruct((t, BRANCH_WIDTH), BF16),
                   jax.ShapeDtypeStruct((HEADS, t, c), BF16),
                   jax.ShapeDtypeStruct((t // c, 8, BRANCH_WIDTH), F32)),
        grid=(t // rp,),
        in_specs=[blk(C_Q), blk(C_K), blk(C_V),
                  pl.BlockSpec((rp, SMALL_COLS), lambda n: (n, 0)),
                  pl.BlockSpec((GDN_CONV, 3 * BRANCH_WIDTH), lambda n: (0, 0)),
                  pl.BlockSpec((8, SMALL_COLS), lambda n: (0, 0))],
        out_specs=(wide, wide, wide, wide,
                   pl.BlockSpec((HEADS, rp, c), lambda n: (0, n, 0)),
                   pl.BlockSpec((prep_chunks, 8, BRANCH_WIDTH), lambda n: (n, 0, 0))),
        scratch_shapes=[pltpu.VMEM((rp + 8, 3 * BRANCH_WIDTH), F32)],
        compiler_params=_cparams(1),
        name="gdn_chunk_prep",
    )(zm, zm, zm, zs, conv_w, par)

    rs = scan_chunks * c
    wide = pl.BlockSpec((rs, BRANCH_WIDTH), lambda n: (n, 0))
    return pl.pallas_call(
        functools.partial(_gdn_scan_kernel, n_ch=scan_chunks),
        out_shape=jax.ShapeDtypeStruct((t, BRANCH_WIDTH), BF16),
        grid=(t // rs,),
        in_specs=[wide, wide, wide, wide,
                  pl.BlockSpec((HEADS, rs, c), lambda n: (0, n, 0)),
                  pl.BlockSpec((scan_chunks, 8, BRANCH_WIDTH), lambda n: (n, 0, 0)),
                  pl.BlockSpec((rs, BRANCH_WIDTH), lambda n: (n, C_G // BRANCH_WIDTH)),
                  pl.BlockSpec((1, HEAD_DIM), lambda n: (0, 0))],
        out_specs=wide,
        scratch_shapes=[pltpu.VMEM((HEADS, HEAD_DIM, HEAD_DIM), F32)],
        compiler_params=_cparams(1),
        name="gdn_state_scan",
    )(u, w, qe, kd, attn, eg, zm, norm_g.reshape(1, -1))


def _ml_kernel(q_ref, k_ref, v_ref, og_ref, zs_ref, par_ref, ng_ref, o_ref,
               c_ref, n_ref, m_ref, *, n_ch):
    c = ML_CHUNK

    @pl.when(pl.program_id(0) == 0)
    def _():
        c_ref[...] = jnp.zeros(c_ref.shape, F32)
        n_ref[...] = jnp.zeros(n_ref.shape, F32)
        m_ref[...] = jnp.zeros(m_ref.shape, F32)

    zz_all = zs_ref[...] + par_ref[2:3, :]
    logf_all = -_softplus(-zz_all)

    ii = lax.broadcasted_iota(jnp.int32, (c, c), 0)
    jj = lax.broadcasted_iota(jnp.int32, (c, c), 1)
    tri_incl = jnp.where(ii >= jj, 1.0, 0.0).astype(BF16)
    ones_cc = jnp.ones((c, c), BF16)
    ones_cl = jnp.ones((c, HEAD_DIM), BF16)

    pairs = []
    for ch in range(n_ch):
        rows = slice(ch * c, (ch + 1) * c)
        for h in range(HEADS):
            lanes = slice(h * HEAD_DIM, (h + 1) * HEAD_DIM)
            qk_lanes = slice(h * ML_QK, (h + 1) * ML_QK)
            pairs.append(dict(
                ch=ch, h=h, rows=rows, lanes=lanes,
                qs=(q_ref[rows, qk_lanes].astype(F32) * ML_QK ** -0.5).astype(BF16),
                kh=k_ref[rows, qk_lanes], vh=v_ref[rows, lanes],
                i_b=jnp.broadcast_to(zz_all[rows, S_I + h:S_I + h + 1], (c, HEAD_DIM)),
                lf_b=jnp.broadcast_to(logf_all[rows, S_F + h:S_F + h + 1], (c, HEAD_DIM))))
    for s in pairs:
        s["bcol"] = _dot_sel_lhs(tri_incl, s["lf_b"])
        s["qk"] = _dot_nt(s["qs"], s["kh"])
    for s in pairs:
        rv = s["i_b"] - s["bcol"]
        s["rrow"] = _dot_sel_lhs(ones_cc, jnp.where(ii == jj, rv[:, :c], 0.0))
        s["b_last"] = s["bcol"][c - 1:c, :]
        s["log_kv"] = s["b_last"] - s["bcol"] + s["i_b"]
        s["kv_max"] = jnp.max(s["log_kv"], axis=0, keepdims=True)
    for s in pairs:
        s["log_intra"] = jnp.where(ii >= jj, s["bcol"][:, :c] + s["rrow"], -jnp.inf)
        s["intra_max"] = jnp.max(s["log_intra"], axis=1, keepdims=True)

    heads = range(HEADS)
    c_mat = [c_ref[h] for h in heads]
    n_b = [n_ref[h] for h in heads]
    m_prev = [m_ref[h][0:1, :] for h in heads]
    for ch in range(n_ch):
        cur = pairs[ch * HEADS:(ch + 1) * HEADS]
        m_out = [jnp.maximum(s["bcol"] + m_prev[s["h"]], s["intra_max"]) for s in cur]
        w_inter = [jnp.exp(s["bcol"] + m_prev[s["h"]] - m_out[s["h"]]) for s in cur]
        sc = [jnp.exp(s["log_intra"] - m_out[s["h"]][:, :c]) * s["qk"] for s in cur]
        num = [w_inter[h] * _dot(cur[h]["qs"], c_mat[h].astype(BF16)) + _dot(sc[h].astype(BF16), cur[h]["vh"])
               for h in heads]
        den = [w_inter[h] * _dot(cur[h]["qs"], n_b[h].astype(BF16)) + jnp.sum(sc[h], axis=1, keepdims=True)
               for h in heads]
        m_new = [jnp.maximum(s["b_last"] + m_prev[s["h"]], s["kv_max"]) for s in cur]
        kw = [(s["kh"].astype(F32) * jnp.exp(s["log_kv"] - m_new[s["h"]])[:, :ML_QK]).astype(BF16) for s in cur]
        decay = [jnp.exp(s["b_last"] + m_prev[s["h"]] - m_new[s["h"]]) for s in cur]
        c_mat = [decay[h] * c_mat[h] + _dot_tn(kw[h], cur[h]["vh"]) for h in heads]
        n_b = [decay[h] * n_b[h] + _dot_tn(kw[h], ones_cl) for h in heads]
        m_prev = m_new
        for h in heads:
            s = cur[h]
            hid = num[h] / jnp.maximum(jnp.abs(den[h]), jnp.exp(-m_out[h]))
            hid = hid * lax.rsqrt(jnp.mean(hid * hid, axis=-1, keepdims=True) + NORM_EPS)
            hid = hid * ng_ref[:, s["lanes"]]
            gate = _sigmoid(og_ref[s["rows"], s["lanes"]].astype(F32))
            o_ref[s["rows"], s["lanes"]] = (hid * gate).astype(o_ref.dtype)
    for h in heads:
        c_ref[h] = c_mat[h]
        n_ref[h] = n_b[h]
        m_ref[h] = jnp.broadcast_to(m_prev[h], (8, HEAD_DIM))


def _mlstm(zm, zs, i_bias, f_bias, norm_g, n_ch):
    t = zm.shape[0]
    c = ML_CHUNK * n_ch
    qk_w = HEADS * ML_QK
    par = jnp.zeros((8, SMALL_COLS), F32)
    par = par.at[2, S_I:S_I + HEADS].set(i_bias).at[2, S_F:S_F + HEADS].set(f_bias)
    return pl.pallas_call(
        functools.partial(_ml_kernel, n_ch=n_ch),
        out_shape=jax.ShapeDtypeStruct((t, BRANCH_WIDTH), BF16),
        grid=(t // c,),
        in_specs=[pl.BlockSpec((c, qk_w), lambda n: (n, D_Q // qk_w)),
                  pl.BlockSpec((c, qk_w), lambda n: (n, D_K // qk_w)),
                  pl.BlockSpec((c, BRANCH_WIDTH), lambda n: (n, D_V // BRANCH_WIDTH)),
                  pl.BlockSpec((c, BRANCH_WIDTH), lambda n: (n, D_O // BRANCH_WIDTH)),
                  pl.BlockSpec((c, SMALL_COLS), lambda n: (n, 0)),
                  pl.BlockSpec((8, SMALL_COLS), lambda n: (0, 0)),
                  pl.BlockSpec((1, BRANCH_WIDTH), lambda n: (0, 0))],
        out_specs=pl.BlockSpec((c, BRANCH_WIDTH), lambda n: (n, 0)),
        scratch_shapes=[pltpu.VMEM((HEADS, ML_QK, HEAD_DIM), F32),
                        pltpu.VMEM((HEADS, ML_QK, HEAD_DIM), F32),
                        pltpu.VMEM((HEADS, 8, HEAD_DIM), F32)],
        compiler_params=_cparams(1),
        name="mlstm",
    )(zm, zm, zm, zm, zs, par, norm_g.reshape(1, -1))


def _merge_kernel(a_ref, b_ref, c_ref, d_ref, g0_ref, g1_ref, g2_ref, g3_ref, wb_ref, o_ref):
    acc = None
    for g, (br, gr) in enumerate(((a_ref, g0_ref), (b_ref, g1_ref), (c_ref, g2_ref), (d_ref, g3_ref))):
        term = _sigmoid(gr[...].astype(F32)) * _dot(br[...], wb_ref[g])
        acc = term if acc is None else acc + term
    o_ref[...] = acc.astype(o_ref.dtype)


def _merge(outs, zm, wb16, tm):
    t = zm.shape[0]
    branch = pl.BlockSpec((tm, BRANCH_WIDTH), lambda i: (i, 0))
    gate = lambda g: pl.BlockSpec((tm, D_MODEL), lambda i: (i, g))
    return pl.pallas_call(
        _merge_kernel,
        out_shape=jax.ShapeDtypeStruct((t, D_MODEL), BF16),
        grid=(t // tm,),
        in_specs=[branch] * 4 + [gate(0), gate(1), gate(2), gate(3),
                                 pl.BlockSpec((N_BRANCHES, BRANCH_WIDTH, D_MODEL), lambda i: (0, 0, 0))],
        out_specs=pl.BlockSpec((tm, D_MODEL), lambda i: (i, 0)),
        compiler_params=_cparams(1),
        name="merge_branches",
    )(*outs, zm, zm, zm, zm, wb16)


def _proj_ln_kernel(a_ref, w_ref, x_ref, g_ref, b_ref, o_ref):
    y = ALPHA * x_ref[...] + _dot(a_ref[...], w_ref[...])
    o_ref[...] = _layer_norm(y, g_ref[...], b_ref[...])


def _proj_ln(a, w16, x, g, b, tm):
    t, k = a.shape
    return pl.pallas_call(
        _proj_ln_kernel,
        out_shape=jax.ShapeDtypeStruct((t, D_MODEL), F32),
        grid=(t // tm,),
        in_specs=[pl.BlockSpec((tm, k), lambda i: (i, 0)),
                  pl.BlockSpec((k, D_MODEL), lambda i: (0, 0)),
                  pl.BlockSpec((tm, D_MODEL), lambda i: (i, 0)),
                  pl.BlockSpec((1, D_MODEL), lambda i: (0, 0)),
                  pl.BlockSpec((1, D_MODEL), lambda i: (0, 0))],
        out_specs=pl.BlockSpec((tm, D_MODEL), lambda i: (i, 0)),
        compiler_params=_cparams(1),
        name="out_proj_layernorm",
    )(a, w16, x, g.reshape(1, -1), b.reshape(1, -1))


def _router_kernel(x_ref, wr_ref, bias_ref, idx_ref, w_ref, rank_ref, cnt_ref, *, tm):
    @pl.when(pl.program_id(0) == 0)
    def _():
        cnt_ref[...] = jnp.zeros(cnt_ref.shape, F32)

    logits = _dot_f32_nt(wr_ref[...], x_ref[...])
    scores = _sigmoid(logits)
    biased = scores + bias_ref[...]
    neg = -jnp.inf

    b3 = biased.reshape(N_GROUPS, GROUP_SIZE, tm)
    pos = lax.broadcasted_iota(jnp.int32, (N_GROUPS, GROUP_SIZE, tm), 1)
    m1 = jnp.max(b3, axis=1, keepdims=True)
    first = jnp.min(jnp.where(b3 == m1, pos, GROUP_SIZE), axis=1, keepdims=True)
    m2 = jnp.max(jnp.where(pos == first, neg, b3), axis=1, keepdims=True)
    gscore = m1 + m2

    gidx = lax.broadcasted_iota(jnp.int32, (N_GROUPS, 1, tm), 0)
    gsel = jnp.zeros((N_GROUPS, 1, tm), F32)
    for _ in range(TOPK_GROUPS):
        gm = jnp.max(gscore, axis=0, keepdims=True)
        gfirst = jnp.min(jnp.where(gscore == gm, gidx, N_GROUPS), axis=0, keepdims=True)
        hit = gidx == gfirst
        gsel = jnp.where(hit, 1.0, gsel)
        gscore = jnp.where(hit, neg, gscore)

    allowed = jnp.broadcast_to(gsel, (N_GROUPS, GROUP_SIZE, tm)) > 0.0
    masked = jnp.where(allowed, b3, neg).reshape(N_EXPERTS, tm)
    eidx = lax.broadcasted_iota(jnp.int32, (N_EXPERTS, tm), 0)
    idx_rows, sel_rows, hits = [], [], []
    for _ in range(TOP_K):
        em = jnp.max(masked, axis=0, keepdims=True)
        efirst = jnp.min(jnp.where(masked == em, eidx, N_EXPERTS), axis=0, keepdims=True)
        hit = eidx == efirst
        hits.append(hit)
        idx_rows.append(efirst)
        sel_rows.append(jnp.sum(jnp.where(hit, scores, 0.0), axis=0, keepdims=True))
        masked = jnp.where(hit, neg, masked)
    total = sel_rows[0]
    for r in sel_rows[1:]:
        total = total + r
    zero_i = jnp.zeros((1, tm), jnp.int32)
    zero_f = jnp.zeros((1, tm), F32)
    idx_ref[...] = jnp.concatenate(idx_rows + [zero_i, zero_i], axis=0)
    w_ref[...] = jnp.concatenate([r / total * ROUTED_SCALE for r in sel_rows] + [zero_f, zero_f], axis=0)

    chosen = jnp.zeros((N_EXPERTS, tm), F32)
    for hit in hits:
        chosen = jnp.where(hit, 1.0, chosen)
    chosen16 = chosen.astype(BF16)
    earlier = (lax.broadcasted_iota(jnp.int32, (tm, tm), 0)
               < lax.broadcasted_iota(jnp.int32, (tm, tm), 1))
    cnt = cnt_ref[...]
    before = _dot(chosen16, jnp.where(earlier, 1.0, 0.0).astype(BF16)) + jnp.tile(cnt, (1, tm // HEAD_DIM))
    ranks = [jnp.sum(jnp.where(hit, before, 0.0), axis=0, keepdims=True).astype(jnp.int32) for hit in hits]
    rank_ref[...] = jnp.concatenate(ranks + [zero_i, zero_i], axis=0)
    cnt_ref[...] = cnt + _dot(chosen16, jnp.ones((tm, HEAD_DIM), BF16))


def _dot_f32_nt(a, b):
    ah, al = _split2(a)
    bh, bl = _split2(b)
    return _dot_nt(ah, bh) + _dot_nt(ah, bl) + _dot_nt(al, bh)


def _router(x, w_router, router_bias, tm):
    t = x.shape[0]
    return pl.pallas_call(
        functools.partial(_router_kernel, tm=tm),
        out_shape=(jax.ShapeDtypeStruct((8, t), jnp.int32), jax.ShapeDtypeStruct((8, t), F32),
                   jax.ShapeDtypeStruct((8, t), jnp.int32),
                   jax.ShapeDtypeStruct((N_EXPERTS, HEAD_DIM), F32)),
        grid=(t // tm,),
        in_specs=[pl.BlockSpec((tm, D_MODEL), lambda i: (i, 0)),
                  pl.BlockSpec((N_EXPERTS, D_MODEL), lambda i: (0, 0)),
                  pl.BlockSpec((N_EXPERTS, 1), lambda i: (0, 0))],
        out_specs=(pl.BlockSpec((8, tm), lambda i: (0, i)), pl.BlockSpec((8, tm), lambda i: (0, i)),
                   pl.BlockSpec((8, tm), lambda i: (0, i)),
                   pl.BlockSpec((N_EXPERTS, HEAD_DIM), lambda i: (0, 0))),
        compiler_params=_cparams(1),
        name="router_topk",
    )(x, w_router.T, router_bias.reshape(N_EXPERTS, 1))


def _dispatch_kernel(pos_ref, x_ref, xs_hbm, sem, *, td):
    for r in range(td):
        for k in range(TOP_K):
            pltpu.make_async_copy(x_ref.at[pl.ds(r, 1), :],
                                  xs_hbm.at[pl.ds(pos_ref[0, r * TOP_K + k], 1), :],
                                  sem).start(priority=k % 2)
    for _ in range(TOP_K):
        pltpu.make_async_copy(x_ref, xs_hbm.at[pl.ds(0, td), :], sem).wait()


def _dispatch(x, pos, td):
    t = x.shape[0]
    return pl.pallas_call(
        functools.partial(_dispatch_kernel, td=td),
        out_shape=jax.ShapeDtypeStruct((t * TOP_K, D_MODEL), F32),
        grid=(t // td,),
        in_specs=[pl.BlockSpec((None, 1, td * TOP_K), lambda i: (i, 0, 0), memory_space=pltpu.SMEM),
                  pl.BlockSpec((td, D_MODEL), lambda i: (i, 0))],
        out_specs=pl.BlockSpec(memory_space=pl.ANY),
        scratch_shapes=[pltpu.SemaphoreType.DMA],
        compiler_params=_cparams(1),
        name="moe_dispatch",
    )(pos.reshape(t // td, 1, td * TOP_K), x)


def _expert_kernel(n_ref, tile_ref, exp_ref, nxt_ref, par_ref, lo_ref, hi_ref,
                   x_ref, wg_hbm, wu_hbm, wd_hbm, y_ref,
                   wgbuf, wubuf, wdbuf, wg16, wu16, wd16, wsem, *, tm, layer):
    j = pl.program_id(0)
    prev = jnp.maximum(j - 1, 0)
    first = (j == 0) | (tile_ref[prev] != tile_ref[j])
    expert = exp_ref[j]
    slot = par_ref[j]

    def copies(ex, s):
        return [pltpu.make_async_copy(hbm.at[layer, ex], buf.at[s], wsem.at[s])
                for hbm, buf in ((wg_hbm, wgbuf), (wu_hbm, wubuf), (wd_hbm, wdbuf))]

    @pl.when(j < n_ref[0])
    def _():
        @pl.when((j == 0) | (exp_ref[prev] != expert))
        def _():
            @pl.when(j == 0)
            def _():
                for cp in copies(expert, slot):
                    cp.start(priority=1)

            @pl.when(nxt_ref[j] != expert)
            def _():
                for cp in copies(nxt_ref[j], 1 - slot):
                    cp.start(priority=1)

            for cp in copies(expert, slot):
                cp.wait()
            wg16[...] = wgbuf[slot].astype(BF16)
            wu16[...] = wubuf[slot].astype(BF16)
            wd16[...] = wdbuf[slot].astype(BF16)

        x = x_ref[...].astype(BF16)
        hidden = _silu(_dot(x, wg16[...])) * _dot(x, wu16[...])
        y = _dot(hidden.astype(BF16), wd16[...])

        @pl.when(first)
        def _():
            y_ref[...] = y

        @pl.when(jnp.logical_not(first))
        def _():
            row = lax.broadcasted_iota(jnp.int32, (tm, 1), 0)
            mine = (row >= lo_ref[j]) & (row < hi_ref[j])
            y_ref[...] = jnp.where(mine, y, y_ref[...])


def _experts(xs, sched, w_gate, w_up, w_down, layer, tm):
    n_max = sched[1].shape[0]
    tile_of = lambda j, n, tl, *rest: tl[j]
    hbm = pl.BlockSpec(memory_space=pl.ANY)
    up_shape, down_shape = (D_MODEL, EXPERT_DIM), (EXPERT_DIM, D_MODEL)
    return pl.pallas_call(
        functools.partial(_expert_kernel, tm=tm, layer=layer),
        out_shape=jax.ShapeDtypeStruct(xs.shape, F32),
        grid_spec=pltpu.PrefetchScalarGridSpec(
            num_scalar_prefetch=len(sched),
            grid=(n_max,),
            in_specs=[pl.BlockSpec((tm, D_MODEL), lambda *a: (tile_of(*a), 0)), hbm, hbm, hbm],
            out_specs=pl.BlockSpec((tm, D_MODEL), lambda *a: (tile_of(*a), 0)),
            scratch_shapes=[pltpu.VMEM((2,) + up_shape, F32), pltpu.VMEM((2,) + up_shape, F32),
                            pltpu.VMEM((2,) + down_shape, F32),
                            pltpu.VMEM(up_shape, BF16), pltpu.VMEM(up_shape, BF16),
                            pltpu.VMEM(down_shape, BF16),
                            pltpu.SemaphoreType.DMA((2,))]),
        compiler_params=_cparams(1),
        name="routed_experts",
    )(*sched, xs, w_gate, w_up, w_down)


def _expert_schedule(idx, rank, counts, t, tm):
    i32 = jnp.int32
    n_rows = t * TOP_K
    n_tiles = n_rows // tm
    n_max = n_tiles + N_EXPERTS - 1
    end = jnp.cumsum(counts)
    start = end - counts
    ids = jnp.arange(N_EXPERTS, dtype=i32)
    start_of = jnp.sum(jnp.where(idx[..., None] == ids, start, 0), axis=-1)
    pos = (start_of + rank).reshape(-1).astype(i32)
    first_tile = start // tm
    items_per_e = jnp.where(counts > 0, (end - 1) // tm - first_tile + 1, 0)
    item_end = jnp.cumsum(items_per_e)
    n_items = item_end[-1]
    j = jnp.arange(n_max, dtype=i32)
    e = jnp.minimum(jnp.sum((item_end[None, :] <= j[:, None]).astype(i32), axis=1), N_EXPERTS - 1)
    tile = first_tile[e] + j - (item_end[e] - items_per_e[e])
    lo = jnp.maximum(start[e], tile * tm) - tile * tm
    hi = jnp.minimum(end[e], (tile + 1) * tm) - tile * tm
    valid = j < n_items
    last = jnp.maximum(n_items - 1, 0)
    pick = lambda a: jnp.where(valid, a, a[last]).astype(i32)
    ids = jnp.arange(N_EXPERTS, dtype=i32)
    later = (ids[None, :] > ids[:, None]) & (counts[None, :] > 0)
    nxt = jnp.min(jnp.where(later, ids[None, :], N_EXPERTS), axis=1)
    nxt = jnp.where(nxt == N_EXPERTS, ids, nxt)
    parity = (jnp.cumsum((counts > 0).astype(i32)) - 1) % 2
    return pos, (n_items.reshape(1).astype(i32), pick(tile), pick(e), pick(nxt[e]), pick(parity[e]),
                 pick(lo), pick(hi))


def _combine_kernel(pos_ref, pos_next_ref, x_ref, w_ref, ys_hbm, sg_ref, su_ref, sd_ref, g_ref, b_ref,
                    o_ref, o16_ref, ybuf, sem, *, tt, n_steps):
    i = pl.program_id(0)
    slot = lax.rem(i, 2)

    def gather(idx_ref, s):
        for r in range(tt):
            for k in range(TOP_K):
                pltpu.make_async_copy(ys_hbm.at[pl.ds(idx_ref[0, r * TOP_K + k], 1), :],
                                      ybuf.at[s, k, pl.ds(r, 1), :], sem.at[s]).start()

    @pl.when(i == 0)
    def _():
        gather(pos_ref, 0)

    @pl.when(i + 1 < n_steps)
    def _():
        gather(pos_next_ref, 1 - slot)

    x = x_ref[...]
    x16 = x.astype(BF16)
    hidden = _silu(_dot(x16, sg_ref[...])) * _dot(x16, su_ref[...])
    acc = _dot(hidden.astype(BF16), sd_ref[...])
    for k in range(TOP_K):
        pltpu.make_async_copy(ys_hbm.at[pl.ds(0, tt), :], ybuf.at[slot, k], sem.at[slot]).wait()
    w = w_ref[...]
    for k in range(TOP_K):
        acc = acc + ybuf[slot, k] * w[:, k:k + 1]
    out = _layer_norm(ALPHA * x + acc, g_ref[...], b_ref[...])
    o_ref[...] = out
    o16_ref[...] = out.astype(BF16)


def _combine(x, w_tok, pos, ys, sg16, su16, sd16, g, b, tt):
    t = x.shape[0]
    nb = t // tt
    pos3 = pos.reshape(nb, 1, tt * TOP_K)
    tile = pl.BlockSpec((tt, D_MODEL), lambda i: (i, 0))
    vec = pl.BlockSpec((1, D_MODEL), lambda i: (0, 0))
    idx_spec = lambda f: pl.BlockSpec((None, 1, tt * TOP_K), f, memory_space=pltpu.SMEM)
    return pl.pallas_call(
        functools.partial(_combine_kernel, tt=tt, n_steps=nb),
        out_shape=(jax.ShapeDtypeStruct((t, D_MODEL), F32), jax.ShapeDtypeStruct((t, D_MODEL), BF16)),
        grid=(nb,),
        in_specs=[idx_spec(lambda i: (i, 0, 0)), idx_spec(lambda i: (jnp.minimum(i + 1, nb - 1), 0, 0)),
                  tile, pl.BlockSpec((tt, 8), lambda i: (i, 0)),
                  pl.BlockSpec(memory_space=pl.ANY),
                  pl.BlockSpec((D_MODEL, SHARED_DIM), lambda i: (0, 0)),
                  pl.BlockSpec((D_MODEL, SHARED_DIM), lambda i: (0, 0)),
                  pl.BlockSpec((SHARED_DIM, D_MODEL), lambda i: (0, 0)), vec, vec],
        out_specs=(tile, tile),
        scratch_shapes=[pltpu.VMEM((2, TOP_K, tt, D_MODEL), F32), pltpu.SemaphoreType.DMA((2,))],
        compiler_params=_cparams(1),
        name="moe_combine_layernorm",
    )(pos3, pos3, x, w_tok, ys, sg16, su16, sd16, g.reshape(1, -1), b.reshape(1, -1))


def _tile(t, pref):
    return min(t, pref)


def _in_proj(x16, w_in):
    t = x16.shape[0]
    tm = _tile(t, 1024)
    wide = lambda r, tn, name: _matmul_nt(x16, w_in[:, r[0]:r[1]].T.astype(BF16), BF16, tm, tn, name)
    z_abc = wide(ABC_COLS, 1536, "in_proj_abc")
    z_d = wide(D_COLS, 1536, "in_proj_d")
    z_gate = wide(GATE_COLS, 2048, "in_proj_gate")
    pad = jnp.zeros((D_MODEL, SMALL_COLS - sum(b - a for a, b in SMALL_SRC)), F32)
    w_small = jnp.concatenate([w_in[:, a:b] for a, b in SMALL_SRC] + [pad], axis=1).astype(BF16)
    zs = _matmul(x16, w_small, F32, tm, SMALL_COLS, "in_proj_small")
    return z_abc, z_d, z_gate, zs


def _mixer(x, x16, p):
    t = x.shape[0]
    zm, z_d, z_gate, zs = _in_proj(x16, p["w_in"])
    out_a = _sb_attention(zm, 256, 256, 8)
    out_b = _gmlp(zm, p["gm_norm_g"], p["gm_norm_b"], p["gm_w_s"], p["gm_b_s"], _tile(t, 512))
    out_c = _gdn(zm, zs, p["gdn_conv_w"], p["gdn_a_log"], p["gdn_dt_bias"], p["gdn_norm_g"], 4, 4)
    out_d = _mlstm(z_d, zs, p["ml_i_bias"], p["ml_f_bias"], p["ml_norm_g"], 8)
    merged = _merge((out_a, out_b, out_c, out_d), z_gate, p["w_branch"].astype(BF16), _tile(t, 256))
    return _proj_ln(merged, p["w_out"].astype(BF16), x, p["ln1_g"], p["ln1_b"], _tile(t, 512))


def _moe(x, p, tm):
    t = x.shape[0]
    idx_t, w_t, rank_t, cnt = _router(x, p["w_router"], p["router_bias"], _tile(t, 512))
    w_tok = w_t.T
    counts = cnt[:, 0].astype(jnp.int32)
    pos, sched = _expert_schedule(idx_t[:TOP_K].T, rank_t[:TOP_K].T, counts, t, tm)
    xs = _dispatch(x, pos, _tile(t, 512))
    ys = _experts(xs, sched, p["w_gate_all"], p["w_up_all"], p["w_down_all"], p["layer"], tm)
    return _combine(x, w_tok, pos, ys, p["ws_gate"].astype(BF16), p["ws_up"].astype(BF16),
                    p["ws_down"].astype(BF16), p["ln2_g"], p["ln2_b"], _tile(t, 128))


def _layer(x, x16, p, tm_expert):
    x1 = _mixer(x, x16, p)
    return _moe(x1, p, tm_expert)


_STACKED = ("w_gate", "w_up", "w_down")


def _layer_params(params, l):
    p = {k: v[l] for k, v in params.items() if k not in _STACKED}
    p.update({k + "_all": params[k] for k in _STACKED})
    p["layer"] = l
    return p


def kernel(x, w_in, gm_norm_g, gm_norm_b, gm_w_s, gm_b_s, gdn_conv_w, gdn_a_log, gdn_dt_bias, gdn_norm_g, ml_i_bias, ml_f_bias, ml_norm_g, w_branch, w_out, ln1_g, ln1_b, w_router, router_bias, w_gate, w_up, w_down, ws_gate, ws_up, ws_down, ln2_g, ln2_b):
    params = dict(w_in=w_in, gm_norm_g=gm_norm_g, gm_norm_b=gm_norm_b, gm_w_s=gm_w_s, gm_b_s=gm_b_s,
                  gdn_conv_w=gdn_conv_w, gdn_a_log=gdn_a_log, gdn_dt_bias=gdn_dt_bias,
                  gdn_norm_g=gdn_norm_g, ml_i_bias=ml_i_bias, ml_f_bias=ml_f_bias, ml_norm_g=ml_norm_g,
                  w_branch=w_branch, w_out=w_out, ln1_g=ln1_g, ln1_b=ln1_b, w_router=w_router,
                  router_bias=router_bias, w_gate=w_gate, w_up=w_up, w_down=w_down, ws_gate=ws_gate,
                  ws_up=ws_up, ws_down=ws_down, ln2_g=ln2_g, ln2_b=ln2_b)
    b, t, d = x.shape
    h = x.reshape(b * t, d)
    h16 = h.astype(BF16)
    for l in range(DEPTH):
        h, h16 = _layer(h, h16, _layer_params(params, l), 256)
    return h.reshape(b, t, d)
```
